```python
import math, functools
import jax, jax.numpy as jnp
from jax import lax
import numpy as np

D_MODEL = 1024
BATCH = 8
SEQ = 2048
DEPTH = 4
DEC_BATCH = 32
DEC_SEQ = 4
PAST_LEN = 8192
PAGE_SIZE = 128

D_MIX = D_MODEL
D_GROUP = D_MIX // 4
D_A = D_GROUP
CONV_W = 31
D_B = D_GROUP
CHUNK = 128
SGU_HEADS = 4
SGU_HD = D_B // SGU_HEADS
H_C = 4
DK_C = D_GROUP // (2 * H_C)
DV_C = 2 * DK_C
D_C = H_C * DV_C
Q_BLOCK = 128
D_D = D_GROUP
POOL_WINDOWS = (2, 4, 8, 16)
POOL_GROUPS = len(POOL_WINDOWS)
POOL_GD = D_D // POOL_GROUPS
POOL_MAX = max(POOL_WINDOWS)
D_FF = -(-8 * D_MODEL // (3 * 256)) * 256
QK_C = H_C * 2 * DK_C
IN_WIDTHS = (D_A, D_A, D_B, D_B, QK_C, QK_C, D_C, D_D)
D_IN = sum(IN_WIDTHS)
IN_SPLITS = tuple(int(s) for s in np.cumsum(IN_WIDTHS)[:-1])
RMS_EPS = 1e-6
LN_EPS = 1e-5

kernel_name = "hybrid_parallel_groups_decoder_step"


def rms_norm(x, g, eps=RMS_EPS):
    xf = x.astype(jnp.float32)
    y = xf * lax.rsqrt(jnp.mean(xf * xf, axis=-1, keepdims=True) + eps)
    return (y * g.astype(jnp.float32)).astype(x.dtype)


def layer_norm(x, g, b, eps=LN_EPS):
    xf = x.astype(jnp.float32)
    mu = jnp.mean(xf, axis=-1, keepdims=True)
    var = jnp.mean(jnp.square(xf - mu), axis=-1, keepdims=True)
    y = (xf - mu) * lax.rsqrt(var + eps) * g.astype(jnp.float32) + b.astype(jnp.float32)
    return y.astype(x.dtype)


def conformer_conv(a, gate, prev, w_dw, b_dw, ln_g, ln_b):
    h = a * jax.nn.sigmoid(gate)
    ext = jnp.concatenate([prev.astype(h.dtype), h], axis=1)
    y = lax.conv_general_dilated(ext, w_dw[:, None, :].astype(h.dtype), window_strides=(1,),
                                 padding='VALID', dimension_numbers=('NWC', 'WIO', 'NWC'),
                                 feature_group_count=D_A) + b_dw
    y = jax.nn.silu(layer_norm(y, ln_g, ln_b))
    return y, ext[:, -(CONV_W - 1):]


def chunk_sgu(u, v, w_s, b_s, ln_g, ln_b):
    v = layer_norm(v, ln_g, ln_b)
    B_, T, _ = v.shape
    L = min(T, CHUNK)
    n = T // L
    vc = v.reshape(B_, n, L, SGU_HEADS, SGU_HD)
    causal = jnp.tril(jnp.ones((L, L), dtype=bool))
    w = jnp.where(causal[None], w_s[:, :L, :L], 0.0)
    mixed = jnp.einsum('gts,bnsgc->bntgc', w, vc) + b_s[:, :L].T[None, None, :, :, None]
    return u * mixed.reshape(B_, T, D_B), v


def diff_lambda(lq1, lk1, lq2, lk2, lam_init):
    f32 = jnp.float32
    return (jnp.exp(jnp.sum(lq1.astype(f32) * lk1.astype(f32)))
            - jnp.exp(jnp.sum(lq2.astype(f32) * lk2.astype(f32))) + lam_init)


def diff_attend(q, k, v, q_pos, k_pos, lam):
    s = jnp.einsum('bqhcd,bkhcd->bhcqk', q, k).astype(jnp.float32) * (DK_C ** -0.5)
    dist = (q_pos[:, None] - k_pos[None, :]).astype(jnp.float32)
    slopes = 2.0 ** (-8.0 * jnp.arange(1, H_C + 1, dtype=jnp.float32) / H_C)
    bias = -slopes[:, None, None, None] * dist
    s = jnp.where(dist >= 0, s + bias, -jnp.inf)
    p = jax.nn.softmax(s, axis=-1)
    p = p[:, :, 0] - lam * p[:, :, 1]
    return jnp.einsum('bhqk,bkhe->bqhe', p.astype(v.dtype), v)


def diff_attn_prompt(q, k, v, lam):
    B_, S = q.shape[0], q.shape[1]
    nb = S // Q_BLOCK
    qb = jnp.moveaxis(q.reshape(B_, nb, Q_BLOCK, H_C, 2, DK_C), 1, 0)
    pos = jnp.arange(S)
    qpos = pos.reshape(nb, Q_BLOCK)
    out = lax.map(lambda a: diff_attend(a[0], k, v, a[1], pos, lam), (qb, qpos))
    return jnp.moveaxis(out, 0, 1).reshape(B_, S, H_C, DV_C)


def diff_attn_sample(q, k, v, lam, k_past, v_past, past_len):
    T = q.shape[1]
    kk = jnp.concatenate([k_past.astype(k.dtype), k], axis=1)
    vv = jnp.concatenate([v_past.astype(v.dtype), v], axis=1)
    q_pos = past_len + jnp.arange(T)
    k_pos = jnp.arange(past_len + T)
    return diff_attend(q, kk, vv, q_pos, k_pos, lam)


def pool_mix(xd, prev, pos, w_pool, scale):
    B_, T, _ = xd.shape
    P = POOL_MAX - 1
    ext_raw = jnp.concatenate([prev.astype(xd.dtype), xd], axis=1)
    ext = ext_raw.astype(jnp.float32)
    cs = jnp.pad(jnp.cumsum(ext, axis=1), ((0, 0), (1, 0), (0, 0)))
    outs = []
    for gi, w in enumerate(POOL_WINDOWS):
        sl = slice(gi * POOL_GD, (gi + 1) * POOL_GD)
        win = cs[:, P + 1:P + 1 + T, sl] - cs[:, P + 1 - w:P + 1 - w + T, sl]
        cnt = jnp.minimum(pos + 1, w).astype(jnp.float32)[None, :, None]
        outs.append(win / cnt - ext[:, P:, sl])
    d = jnp.stack(outs, axis=2)
    y = jnp.einsum('btgc,gce->btge', d, w_pool.astype(jnp.float32)).reshape(B_, T, D_D)
    return (y * scale.astype(jnp.float32)).astype(xd.dtype), ext_raw[:, -P:]


def hybrid_layer(x, pos, conv_prev, pool_prev, attend, lam_init, p):
    B_, T, _ = x.shape
    h = rms_norm(x, p['g_mix_pre'])
    z = h @ p['w_in']
    a, ga, u, vb, q, k, v, xd = jnp.split(z, IN_SPLITS, axis=-1)
    ya, conv_new = conformer_conv(a, ga, conv_prev, p['conv_w'], p['conv_b'], p['conv_ln_g'], p['conv_ln_b'])
    yb, v_rows = chunk_sgu(jax.nn.gelu(u, approximate=False), jax.nn.gelu(vb, approximate=False),
                           p['sgu_w'], p['sgu_b'], p['sgu_ln_g'], p['sgu_ln_b'])
    lam = diff_lambda(p['lambda_q1'], p['lambda_k1'], p['lambda_q2'], p['lambda_k2'], lam_init)
    q = q.reshape(B_, T, H_C, 2, DK_C)
    k = k.reshape(B_, T, H_C, 2, DK_C)
    v = v.reshape(B_, T, H_C, DV_C)
    o = attend(q, k, v, lam)
    yc = (rms_norm(o, p['attn_sub_g'], LN_EPS) * (1.0 - lam_init)).reshape(B_, T, D_C)
    yd, pool_new = pool_mix(xd, pool_prev, pos, p['pool_w'], p['pool_scale'])
    mix = jnp.concatenate([ya, yb, yc, yd], axis=-1) @ p['w_out']
    x = x + rms_norm(mix, p['g_mix_post'])
    gate, up = jnp.split(rms_norm(x, p['g_ffn_pre']) @ p['w_gu'], 2, axis=-1)
    x = x + rms_norm((jax.nn.silu(gate) * up) @ p['w_down'], p['g_ffn_post'])
    return x, k, v, conv_new, pool_new, v_rows


def setup_inputs(seed: int = 0) -> dict:
    key = jax.random.key(seed)
    ks = iter(jax.random.split(key, 40))
    f32 = jnp.float32
    nrm = lambda shape, s: jax.random.normal(next(ks), shape, f32) * s
    n_pages = PAST_LEN // PAGE_SIZE
    used = DEC_BATCH * n_pages
    n_pool = used + max(1, used // 4)
    page_table = jax.random.permutation(next(ks), n_pool)[:used].reshape(DEC_BATCH, n_pages).astype(jnp.int32)
    return {
        'x_prompt': nrm((BATCH, SEQ, D_MODEL), 1.0),
        'x_sample': nrm((DEC_BATCH, DEC_SEQ, D_MODEL), 1.0),
        'cache_k': nrm((DEPTH, n_pool, PAGE_SIZE, H_C, 2, DK_C), 1.0),
        'cache_v': nrm((DEPTH, n_pool, PAGE_SIZE, H_C, DV_C), 1.0),
        'state_conv': nrm((DEPTH, DEC_BATCH, CONV_W - 1, D_A), 0.5),
        'state_pool': nrm((DEPTH, DEC_BATCH, POOL_MAX - 1, D_D), 1.0),
        'page_table': page_table,
        'w_in': nrm((DEPTH, D_MODEL, D_IN), D_MODEL ** -0.5),
        'w_out': nrm((DEPTH, D_MIX, D_MODEL), D_MIX ** -0.5),
        'conv_w': nrm((DEPTH, CONV_W, D_A), CONV_W ** -0.5),
        'conv_b': nrm((DEPTH, D_A), 0.01),
        'conv_ln_g': 1.0 + nrm((DEPTH, D_A), 0.02),
        'conv_ln_b': nrm((DEPTH, D_A), 0.01),
        'sgu_w': nrm((DEPTH, SGU_HEADS, CHUNK, CHUNK), CHUNK ** -0.5),
        'sgu_b': 1.0 + nrm((DEPTH, SGU_HEADS, CHUNK), 0.02),
        'sgu_ln_g': 1.0 + nrm((DEPTH, D_B), 0.02),
        'sgu_ln_b': nrm((DEPTH, D_B), 0.01),
        'lambda_q1': nrm((DEPTH, DK_C), 0.1),
        'lambda_k1': nrm((DEPTH, DK_C), 0.1),
        'lambda_q2': nrm((DEPTH, DK_C), 0.1),
        'lambda_k2': nrm((DEPTH, DK_C), 0.1),
        'attn_sub_g': 1.0 + nrm((DEPTH, DV_C), 0.02),
        'pool_w': nrm((DEPTH, POOL_GROUPS, POOL_GD, POOL_GD), POOL_GD ** -0.5),
        'pool_scale': 1.0 + nrm((DEPTH, D_D), 0.1),
        'g_mix_pre': 1.0 + nrm((DEPTH, D_MODEL), 0.02),
        'g_mix_post': 1.0 + nrm((DEPTH, D_MODEL), 0.02),
        'g_ffn_pre': 1.0 + nrm((DEPTH, D_MODEL), 0.02),
        'g_ffn_post': 1.0 + nrm((DEPTH, D_MODEL), 0.02),
        'w_gu': nrm((DEPTH, D_MODEL, 2 * D_FF), D_MODEL ** -0.5),
        'w_down': nrm((DEPTH, D_FF, D_MODEL), D_FF ** -0.5),
    }


def reference(x_prompt, x_sample, cache_k, cache_v, state_conv, state_pool, page_table,
              w_in, w_out, conv_w, conv_b, conv_ln_g, conv_ln_b, sgu_w, sgu_b, sgu_ln_g, sgu_ln_b,
              lambda_q1, lambda_k1, lambda_q2, lambda_k2, attn_sub_g, pool_w, pool_scale,
              g_mix_pre, g_mix_post, g_ffn_pre, g_ffn_post, w_gu, w_down):
    B_, S, _ = x_prompt.shape
    DB, T, _ = x_sample.shape
    past_len = page_table.shape[1] * cache_k.shape[2]
    pos_p = jnp.arange(S)
    pos_s = past_len + jnp.arange(T)
    conv_zero = jnp.zeros((B_, CONV_W - 1, D_A), x_prompt.dtype)
    pool_zero = jnp.zeros((B_, POOL_MAX - 1, D_D), x_prompt.dtype)
    xp, xs = x_prompt, x_sample
    kp_l, vp_l, cp_l, pp_l = [], [], [], []
    ks_l, vs_l, cs_l, ps_l, gs_l = [], [], [], [], []
    for l in range(DEPTH):
        lam_init = 0.8 - 0.6 * math.exp(-0.3 * l)
        prm = {
            'w_in': w_in[l], 'w_out': w_out[l], 'conv_w': conv_w[l], 'conv_b': conv_b[l],
            'conv_ln_g': conv_ln_g[l], 'conv_ln_b': conv_ln_b[l], 'sgu_w': sgu_w[l], 'sgu_b': sgu_b[l],
            'sgu_ln_g': sgu_ln_g[l], 'sgu_ln_b': sgu_ln_b[l], 'lambda_q1': lambda_q1[l],
            'lambda_k1': lambda_k1[l], 'lambda_q2': lambda_q2[l], 'lambda_k2': lambda_k2[l],
            'attn_sub_g': attn_sub_g[l], 'pool_w': pool_w[l], 'pool_scale': pool_scale[l],
            'g_mix_pre': g_mix_pre[l], 'g_mix_post': g_mix_post[l], 'g_ffn_pre': g_ffn_pre[l],
            'g_ffn_post': g_ffn_post[l], 'w_gu': w_gu[l], 'w_down': w_down[l],
        }
        xp, kp, vp, cp, pp, _ = hybrid_layer(xp, pos_p, conv_zero, pool_zero, diff_attn_prompt, lam_init, prm)
        k_past = cache_k[l][page_table].reshape(DB, past_len, H_C, 2, DK_C)
        v_past = cache_v[l][page_table].reshape(DB, past_len, H_C, DV_C)
        attend_s = functools.partial(diff_attn_sample, k_past=k_past, v_past=v_past, past_len=past_len)
        xs, ks_, vs_, cs_, ps_, gs_ = hybrid_layer(xs, pos_s, state_conv[l], state_pool[l], attend_s, lam_init, prm)
        kp_l.append(kp); vp_l.append(vp); cp_l.append(cp); pp_l.append(pp)
        ks_l.append(ks_); vs_l.append(vs_); cs_l.append(cs_); ps_l.append(ps_); gs_l.append(gs_)
    return (xp, xs,
            jnp.stack(kp_l), jnp.stack(vp_l), jnp.stack(cp_l), jnp.stack(pp_l),
            jnp.stack(ks_l), jnp.stack(vs_l), jnp.stack(cs_l), jnp.stack(ps_l), jnp.stack(gs_l))
```

```python
import functools
import math

import jax
import jax.numpy as jnp
from jax import lax
from jax.experimental import pallas as pl
from jax.experimental.pallas import tpu as pltpu

F32 = jnp.float32
BF16 = jnp.bfloat16

D_MODEL = 1024
D_GROUP = 256
CONV_W = 31
CONV_HALO = CONV_W - 1
SGU_CHUNK = 128
SGU_HEADS = 4
N_HEADS = 4
DK = 32
DV = 64
POOL_WINDOWS = (2, 4, 8, 16)
POOL_HALO = max(POOL_WINDOWS) - 1
POOL_GD = D_GROUP // len(POOL_WINDOWS)
D_FF = 2816
FF_CHUNK = 256
N_FF_CHUNKS = D_FF // FF_CHUNK
RMS_EPS = 1e-6
LN_EPS = 1e-5
Q_SCALE = DK ** -0.5

SUBLANES = 8
LANES = 128
VMEM_LIMIT_BYTES = 56 * 1024 * 1024

ROW_TILE = 512
SEQ_TILE = 256
CONV_ROWS = 64
ATT_TILE = 256
PAGES_PER_STEP = 16

NT_DIMS = (((1,), (1,)), ((), ()))


def _rms(x, g, eps):
    return x * lax.rsqrt(jnp.mean(x * x, axis=-1, keepdims=True) + eps) * g


def _layer_norm(x, g, b):
    mu = jnp.mean(x, axis=-1, keepdims=True)
    xc = x - mu
    var = jnp.mean(xc * xc, axis=-1, keepdims=True)
    return xc * lax.rsqrt(var + LN_EPS) * g + b


def _gelu(x):
    return 0.5 * x * (1.0 + lax.erf(x * (2.0 ** -0.5)))


def _sigmoid(x):
    return 1.0 / (1.0 + jnp.exp(-x))


def _lane_group_select(pieces, group_width, shape):
    lane = lax.broadcasted_iota(jnp.int32, shape, len(shape) - 1)
    out = pieces[-1]
    for g in range(len(pieces) - 2, -1, -1):
        out = jnp.where(lane < (g + 1) * group_width, pieces[g], out)
    return out


def _in_proj_kernel(x_ref, g_ref, wm_ref, *rest, widths, emit_kvt):
    if emit_kvt:
        wkv_ref, *out_refs = rest
    else:
        out_refs = rest
    h = _rms(x_ref[...], g_ref[...], RMS_EPS).astype(BF16)
    z = jnp.dot(h, wm_ref[...], preferred_element_type=F32)
    col = 0
    for o_ref, w in zip(out_refs, widths):
        o_ref[...] = z[:, col:col + w]
        col += w
    if emit_kvt:
        kv = lax.dot_general(wkv_ref[...], h, NT_DIMS, preferred_element_type=F32)
        kt_ref, vt_ref = out_refs[len(widths):]
        kt_ref[0] = kv[0:D_GROUP]
        vt_ref[0] = kv[D_GROUP:2 * D_GROUP]


def _in_proj(x2d, g, w_main, w_kvt, *, batch, seq, widths, tm):
    m = batch * seq
    ns = seq // tm
    emit_kvt = w_kvt is not None
    row_map = lambda b, i: (b * ns + i, 0)
    const2 = lambda b, i: (0, 0)
    in_specs = [pl.BlockSpec((tm, D_MODEL), row_map),
                pl.BlockSpec((1, D_MODEL), const2),
                pl.BlockSpec(w_main.shape, const2)]
    args = [x2d, g, w_main]
    out_shape = [jax.ShapeDtypeStruct((m, w), F32) for w in widths]
    out_specs = [pl.BlockSpec((tm, w), row_map) for w in widths]
    if emit_kvt:
        in_specs.append(pl.BlockSpec(w_kvt.shape, const2))
        args.append(w_kvt)
        for _ in range(2):
            out_shape.append(jax.ShapeDtypeStruct((batch, D_GROUP, seq), F32))
            out_specs.append(pl.BlockSpec((1, D_GROUP, tm), lambda b, i: (b, 0, i)))
    return pl.pallas_call(
        functools.partial(_in_proj_kernel, widths=widths, emit_kvt=emit_kvt),
        grid=(batch, ns),
        in_specs=in_specs,
        out_specs=out_specs,
        out_shape=out_shape,
        compiler_params=pltpu.CompilerParams(
            dimension_semantics=("arbitrary", "arbitrary"), vmem_limit_bytes=VMEM_LIMIT_BYTES),
        name="in_proj",
    )(*args)


def _conv_post(acc, clg, clb):
    y = _layer_norm(acc, clg, clb)
    return y * _sigmoid(y)


def _pool_select(wins, shape):
    return _lane_group_select(wins, POOL_GD, shape)


def _mix_prompt_kernel(za_ref, zb_ref, xd_ref, cw_ref, cb_ref, clg_ref, clb_ref, sw_ref, sb_ref, slg_ref,
                       slb_ref, pw_ref, ps_ref, mix_ref, convn_ref, pooln_ref, hext, xext, *, ts):
    i = pl.program_id(1)
    hpad = 32
    xpad = 16

    @pl.when(i == 0)
    def _():
        hext[0:hpad, :] = jnp.zeros((hpad, D_GROUP), F32)
        xext[0:xpad, :] = jnp.zeros((xpad, D_GROUP), F32)

    hext[hpad:hpad + ts, :] = za_ref[:, 0:D_GROUP] * _sigmoid(za_ref[:, D_GROUP:2 * D_GROUP])
    for c in range(ts // CONV_ROWS):
        r0 = c * CONV_ROWS
        acc = jnp.broadcast_to(cb_ref[...], (CONV_ROWS, D_GROUP))
        for j in range(CONV_W):
            start = hpad - CONV_HALO + r0 + j
            acc = acc + cw_ref[j:j + 1, :] * hext[start:start + CONV_ROWS, :]
        mix_ref[r0:r0 + CONV_ROWS, 0:D_GROUP] = _conv_post(acc, clg_ref[...], clb_ref[...]).astype(BF16)

    u = _gelu(zb_ref[:, 0:D_GROUP])
    v = _layer_norm(_gelu(zb_ref[:, D_GROUP:2 * D_GROUP]), slg_ref[...], slb_ref[...])
    tri_r = lax.broadcasted_iota(jnp.int32, (SGU_CHUNK, SGU_CHUNK), 0)
    tri_c = lax.broadcasted_iota(jnp.int32, (SGU_CHUNK, SGU_CHUNK), 1)
    w_heads = [jnp.where(tri_c <= tri_r, sw_ref[g], 0.0).astype(BF16) for g in range(SGU_HEADS)]
    w_stack = jnp.concatenate(w_heads, axis=0)
    hd = D_GROUP // SGU_HEADS
    for c in range(ts // SGU_CHUNK):
        rows = slice(c * SGU_CHUNK, (c + 1) * SGU_CHUNK)
        r = jnp.dot(w_stack, v[rows].astype(BF16), preferred_element_type=F32)
        mixed = _lane_group_select([r[g * SGU_CHUNK:(g + 1) * SGU_CHUNK] for g in range(SGU_HEADS)],
                                   hd, (SGU_CHUNK, D_GROUP))
        mix_ref[rows, D_GROUP:2 * D_GROUP] = (u[rows] * (mixed + sb_ref[...])).astype(BF16)

    xext[xpad:xpad + ts, :] = xd_ref[...]
    wvec = _pool_select([jnp.full((1, D_GROUP), float(w), F32) for w in POOL_WINDOWS], (1, D_GROUP))
    for c in range(ts // CONV_ROWS):
        r0 = c * CONV_ROWS
        x0 = xext[xpad + r0:xpad + r0 + CONV_ROWS, :]
        acc = x0
        wins = []
        for k in range(1, POOL_HALO + 1):
            acc = acc + xext[xpad + r0 - k:xpad + r0 - k + CONV_ROWS, :]
            if k + 1 in POOL_WINDOWS:
                wins.append(acc)
        pos = (i * ts + r0 + lax.broadcasted_iota(jnp.int32, (CONV_ROWS, D_GROUP), 0)).astype(F32)
        cnt = jnp.minimum(pos + 1.0, wvec)
        d = _pool_select(wins, (CONV_ROWS, D_GROUP)) / cnt - x0
        y = jnp.dot(d.astype(BF16), pw_ref[...], preferred_element_type=F32) * ps_ref[...]
        mix_ref[r0:r0 + CONV_ROWS, 2 * D_GROUP:3 * D_GROUP] = y.astype(BF16)

    hext[0:hpad, :] = hext[ts:ts + hpad, :]
    xext[0:xpad, :] = xext[ts:ts + xpad, :]

    @pl.when(i == pl.num_programs(1) - 1)
    def _():
        convn_ref[0] = hext[hpad - CONV_HALO:hpad, :]
        pooln_ref[0] = xext[xpad - POOL_HALO:xpad, :]


def _mix_prompt(za, zb, xd, lw, *, batch, seq):
    ts = SEQ_TILE
    ns = seq // ts
    m = batch * seq
    row_map = lambda b, i: (b * ns + i, 0)
    c2 = lambda b, i: (0, 0)
    c3 = lambda b, i: (0, 0, 0)
    vec = pl.BlockSpec((1, D_GROUP), c2)
    return pl.pallas_call(
        functools.partial(_mix_prompt_kernel, ts=ts),
        grid=(batch, ns),
        in_specs=[pl.BlockSpec((ts, 2 * D_GROUP), row_map),
                  pl.BlockSpec((ts, 2 * D_GROUP), row_map),
                  pl.BlockSpec((ts, D_GROUP), row_map),
                  pl.BlockSpec((CONV_W, D_GROUP), c2), vec, vec, vec,
                  pl.BlockSpec((SGU_HEADS, SGU_CHUNK, SGU_CHUNK), c3),
                  pl.BlockSpec((SGU_CHUNK, D_GROUP), c2), vec, vec,
                  pl.BlockSpec((D_GROUP, D_GROUP), c2), vec],
        out_specs=[pl.BlockSpec((ts, 3 * D_GROUP), row_map),
                   pl.BlockSpec((1, CONV_HALO, D_GROUP), lambda b, i: (b, 0, 0)),
                   pl.BlockSpec((1, POOL_HALO, D_GROUP), lambda b, i: (b, 0, 0))],
        out_shape=[jax.ShapeDtypeStruct((m, 3 * D_GROUP), BF16),
                   jax.ShapeDtypeStruct((batch, CONV_HALO, D_GROUP), F32),
                   jax.ShapeDtypeStruct((batch, POOL_HALO, D_GROUP), F32)],
        scratch_shapes=[pltpu.VMEM((32 + ts, D_GROUP), F32), pltpu.VMEM((16 + ts, D_GROUP), F32)],
        compiler_params=pltpu.CompilerParams(
            dimension_semantics=("arbitrary", "arbitrary"), vmem_limit_bytes=VMEM_LIMIT_BYTES),
        name="mix_prompt",
    )(za, zb, xd, lw["conv_w"], lw["conv_b"], lw["conv_ln_g"], lw["conv_ln_b"], lw["sgu_w"], lw["sgu_b_full"],
      lw["sgu_ln_g"], lw["sgu_ln_b"], lw["pool_w_bd"], lw["pool_scale"])


def _mix_sample_kernel(za_ref, zb_ref, xd_ref, sc_ref, sp_ref, cw_ref, cb_ref, clg_ref, clb_ref, sw4_ref, sb_ref,
                       slg_ref, slb_ref, pw_ref, ps_ref, mix_ref, convn_ref, pooln_ref, sguv_ref, *, db, t_new,
                       past_len):
    rows = lambda t: slice(t * db, (t + 1) * db)

    ext = [sc_ref[j] for j in range(CONV_HALO)]
    ext += [za_ref[rows(t), 0:D_GROUP] * _sigmoid(za_ref[rows(t), D_GROUP:2 * D_GROUP]) for t in range(t_new)]
    for t in range(t_new):
        acc = jnp.broadcast_to(cb_ref[...], (db, D_GROUP))
        for j in range(CONV_W):
            acc = acc + cw_ref[j:j + 1, :] * ext[t + j]
        mix_ref[rows(t), 0:D_GROUP] = _conv_post(acc, clg_ref[...], clb_ref[...]).astype(BF16)
    for j in range(CONV_HALO):
        convn_ref[j] = ext[t_new + j]

    u = _gelu(zb_ref[:, 0:D_GROUP])
    v = _layer_norm(_gelu(zb_ref[:, D_GROUP:2 * D_GROUP]), slg_ref[...], slb_ref[...])
    sguv_ref[...] = v
    for t in range(t_new):
        mixed = jnp.broadcast_to(sb_ref[t:t + 1, :], (db, D_GROUP))
        for s in range(t + 1):
            mixed = mixed + sw4_ref[t * t_new + s:t * t_new + s + 1, :] * v[rows(s)]
        mix_ref[rows(t), D_GROUP:2 * D_GROUP] = (u[rows(t)] * mixed).astype(BF16)

    pext = [sp_ref[j] for j in range(POOL_HALO)] + [xd_ref[rows(t), :] for t in range(t_new)]
    ds = []
    for t in range(t_new):
        x0 = pext[POOL_HALO + t]
        acc = x0
        wins = []
        for k in range(1, POOL_HALO + 1):
            acc = acc + pext[POOL_HALO + t - k]
            if k + 1 in POOL_WINDOWS:
                wins.append(acc)
        cnts = [jnp.full((1, D_GROUP), float(min(past_len + t + 1, w)), F32) for w in POOL_WINDOWS]
        cnt = _pool_select(cnts, (1, D_GROUP))
        ds.append(_pool_select(wins, (db, D_GROUP)) / cnt - x0)
    d = jnp.concatenate(ds, axis=0)
    y = jnp.dot(d.astype(BF16), pw_ref[...], preferred_element_type=F32) * ps_ref[...]
    mix_ref[:, 2 * D_GROUP:3 * D_GROUP] = y.astype(BF16)
    for j in range(POOL_HALO):
        pooln_ref[j] = pext[t_new + j]


def _mix_sample(za, zb, xd, sc_tm, sp_tm, lw, *, db, t_new, past_len):
    m = db * t_new
    return pl.pallas_call(
        functools.partial(_mix_sample_kernel, db=db, t_new=t_new, past_len=past_len),
        out_shape=[jax.ShapeDtypeStruct((m, 3 * D_GROUP), BF16),
                   jax.ShapeDtypeStruct((CONV_HALO, db, D_GROUP), F32),
                   jax.ShapeDtypeStruct((POOL_HALO, db, D_GROUP), F32),
                   jax.ShapeDtypeStruct((m, D_GROUP), F32)],
        compiler_params=pltpu.CompilerParams(vmem_limit_bytes=VMEM_LIMIT_BYTES),
        name="mix_sample",
    )(za, zb, xd, sc_tm, sp_tm, lw["conv_w"], lw["conv_b"], lw["conv_ln_g"], lw["conv_ln_b"], lw["sgu_w4"],
      lw["sgu_b_full"], lw["sgu_ln_g"], lw["sgu_ln_b"], lw["pool_w_bd"], lw["pool_scale"])


def _diff_lambda(lq1, lk1, lq2, lk2, lam_init):
    s1 = jnp.sum(lq1 * lk1, axis=-1, keepdims=True)
    s2 = jnp.sum(lq2 * lk2, axis=-1, keepdims=True)
    return jnp.exp(s1) - jnp.exp(s2) + lam_init


def _alibi_slope(h):
    return 2.0 ** (-8.0 * (h + 1) / N_HEADS)


def _attn_prompt_kernel(q_ref, kt_ref, vt_ref, lq1_ref, lk1_ref, lq2_ref, lk2_ref, subg_ref, o_ref, ktb, vtb, *,
                        tq, nblk, lam_init):
    i = pl.program_id(1)
    hd = 2 * DK

    @pl.when(i == 0)
    def _():
        for j in range(nblk):
            ktb[j] = kt_ref[0, :, j * tq:(j + 1) * tq].astype(BF16)
            vtb[j] = vt_ref[0, :, j * tq:(j + 1) * tq].astype(BF16)

    lam = _diff_lambda(lq1_ref[...], lk1_ref[...], lq2_ref[...], lk2_ref[...], lam_init)
    q = q_ref[...] * Q_SCALE
    lane = lax.broadcasted_iota(jnp.int32, (tq, hd), 1)
    row = lax.broadcasted_iota(jnp.int32, (2 * tq, tq), 0)
    col = lax.broadcasted_iota(jnp.int32, (2 * tq, tq), 1)
    rel = (col - jnp.where(row >= tq, row - tq, row)).astype(F32)

    for h in range(N_HEADS):
        qh = q[:, h * hd:(h + 1) * hd]
        qs = jnp.concatenate([jnp.where(lane < DK, qh, 0.0), jnp.where(lane >= DK, qh, 0.0)],
                             axis=0).astype(BF16)
        slope = _alibi_slope(h)

        def block(j, carry, masked, h=h, qs=qs, slope=slope):
            m, l, acc = carry
            s = jnp.dot(qs, ktb[j, h * hd:(h + 1) * hd, :], preferred_element_type=F32)
            neg_dist = rel + ((j - i) * tq).astype(F32)
            s = s + slope * neg_dist
            if masked:
                s = jnp.where(neg_dist <= 0.0, s, -jnp.inf)
            m_new = jnp.maximum(m, jnp.max(s, axis=-1, keepdims=True))
            alpha = jnp.exp(m - m_new)
            p = jnp.exp(s - m_new)
            l = alpha * l + jnp.sum(p, axis=-1, keepdims=True)
            pv = lax.dot_general(p.astype(BF16), vtb[j, h * hd:(h + 1) * hd, :], NT_DIMS,
                                 preferred_element_type=F32)
            return m_new, l, alpha * acc + pv

        init = (jnp.full((2 * tq, 1), -jnp.inf, F32), jnp.zeros((2 * tq, 1), F32), jnp.zeros((2 * tq, hd), F32))
        carry = lax.fori_loop(0, i, functools.partial(block, masked=False), init)
        _, l, acc = block(i, carry, True)
        o = acc / l
        od = o[0:tq] - lam * o[tq:2 * tq]
        y = _rms(od, subg_ref[...], LN_EPS) * (1.0 - lam_init)
        o_ref[:, h * hd:(h + 1) * hd] = y.astype(BF16)


def _attn_prompt(q, kt, vt, lw, *, batch, seq, lam_init):
    tq = ATT_TILE
    ns = seq // tq
    c2 = lambda b, i: (0, 0)
    lvec = pl.BlockSpec((1, DK), c2)
    return pl.pallas_call(
        functools.partial(_attn_prompt_kernel, tq=tq, nblk=ns, lam_init=lam_init),
        grid=(batch, ns),
        in_specs=[pl.BlockSpec((tq, D_GROUP), lambda b, i: (b * ns + i, 0)),
                  pl.BlockSpec((1, D_GROUP, seq), lambda b, i: (b, 0, 0)),
                  pl.BlockSpec((1, D_GROUP, seq), lambda b, i: (b, 0, 0)),
                  lvec, lvec, lvec, lvec, pl.BlockSpec((1, DV), c2)],
        out_specs=pl.BlockSpec((tq, D_GROUP), lambda b, i: (b * ns + i, 0)),
        out_shape=jax.ShapeDtypeStruct((batch * seq, D_GROUP), BF16),
        scratch_shapes=[pltpu.VMEM((ns, D_GROUP, tq), BF16), pltpu.VMEM((ns, D_GROUP, tq), BF16)],
        compiler_params=pltpu.CompilerParams(
            dimension_semantics=("arbitrary", "arbitrary"), vmem_limit_bytes=VMEM_LIMIT_BYTES),
        name="attn_prompt",
    )(q, kt, vt, lw["lambda_q1"], lw["lambda_k1"], lw["lambda_q2"], lw["lambda_k2"], lw["attn_sub_g"])


def _attn_sample_kernel(pt_ref, q_ref, kn_ref, vn_ref, lq1_ref, lk1_ref, lq2_ref, lk2_ref, subg_ref, *rest, db,
                        t_new, page, g_pages, past_len, lam_init):
    del pt_ref
    k_pages = rest[:g_pages]
    v_pages = rest[g_pages:2 * g_pages]
    o_ref, qbd_s, m_s, l_s, acc_s = rest[2 * g_pages:]
    b = pl.program_id(0)
    g = pl.program_id(1)
    n_rows = N_HEADS * 2 * t_new
    hd = 2 * DK
    row1 = lax.broadcasted_iota(jnp.int32, (n_rows, 1), 0)
    t_row = (row1 % t_new).astype(F32)
    slope_row = jnp.zeros((n_rows, 1), F32)
    for h in range(N_HEADS):
        slope_row = jnp.where(row1 // (2 * t_new) == h, _alibi_slope(h), slope_row)

    def gather_rows(ref):
        return jnp.concatenate([ref[pl.ds(t * db + b, 1), :] for t in range(t_new)], axis=0)

    @pl.when(g == 0)
    def _():
        q4 = gather_rows(q_ref) * Q_SCALE
        qt = jnp.concatenate([q4] * (2 * N_HEADS), axis=0)
        r = lax.broadcasted_iota(jnp.int32, (n_rows, D_GROUP), 0)
        c = lax.broadcasted_iota(jnp.int32, (n_rows, D_GROUP), 1)
        qbd_s[...] = jnp.where(c // DK == r // t_new, qt, 0.0).astype(BF16)
        m_s[...] = jnp.full((n_rows, 1), -jnp.inf, F32)
        l_s[...] = jnp.zeros((n_rows, 1), F32)
        acc_s[...] = jnp.zeros((n_rows, D_GROUP), F32)

    qbd = qbd_s[...]
    s = jnp.concatenate(
        [jnp.dot(qbd, kp[0, 0].astype(BF16), preferred_element_type=F32) for kp in k_pages], axis=1)
    width = g_pages * page
    kpos = (g * width + lax.broadcasted_iota(jnp.int32, (n_rows, width), 1)).astype(F32)
    s = s - slope_row * ((past_len + t_row) - kpos)
    m_prev = m_s[...]
    m_new = jnp.maximum(m_prev, jnp.max(s, axis=-1, keepdims=True))
    alpha = jnp.exp(m_prev - m_new)
    p = jnp.exp(s - m_new)
    l_s[...] = alpha * l_s[...] + jnp.sum(p, axis=-1, keepdims=True)
    m_s[...] = m_new
    pb = p.astype(BF16)
    pv = jnp.zeros((n_rows, D_GROUP), F32)
    for idx, vp in enumerate(v_pages):
        pv = pv + lax.dot_general(pb[:, idx * page:(idx + 1) * page], vp[0, 0].astype(BF16), NT_DIMS,
                                  preferred_element_type=F32)
    acc_s[...] = alpha * acc_s[...] + pv

    @pl.when(g == pl.num_programs(1) - 1)
    def _():
        kn = gather_rows(kn_ref)
        vn = gather_rows(vn_ref)
        kn8 = jnp.concatenate([kn, jnp.zeros((SUBLANES - t_new, D_GROUP), F32)], axis=0).astype(BF16)
        sn = lax.dot_general(qbd, kn8, NT_DIMS, preferred_element_type=F32)
        tk = lax.broadcasted_iota(jnp.int32, (n_rows, SUBLANES), 1).astype(F32)
        dist = t_row - tk
        sn = jnp.where(dist >= 0.0, sn - slope_row * dist, -jnp.inf)
        m_prev = m_s[...]
        m_fin = jnp.maximum(m_prev, jnp.max(sn, axis=-1, keepdims=True))
        alpha = jnp.exp(m_prev - m_fin)
        pn = jnp.exp(sn - m_fin)
        l_fin = alpha * l_s[...] + jnp.sum(pn, axis=-1, keepdims=True)
        acc = alpha * acc_s[...]
        for t in range(t_new):
            acc = acc + pn[:, t:t + 1] * vn[t:t + 1, :]
        o = acc / l_fin
        r = lax.broadcasted_iota(jnp.int32, (n_rows, D_GROUP), 0)
        c = lax.broadcasted_iota(jnp.int32, (n_rows, D_GROUP), 1)
        o = jnp.where(c // DV == r // (2 * t_new), o, 0.0)
        y0 = jnp.zeros((t_new, D_GROUP), F32)
        y1 = jnp.zeros((t_new, D_GROUP), F32)
        for h in range(N_HEADS):
            y0 = y0 + o[h * 2 * t_new:h * 2 * t_new + t_new]
            y1 = y1 + o[h * 2 * t_new + t_new:(h + 1) * 2 * t_new]
        lam = _diff_lambda(lq1_ref[...], lk1_ref[...], lq2_ref[...], lk2_ref[...], lam_init)
        od = y0 - lam * y1
        seg_r = lax.broadcasted_iota(jnp.int32, (D_GROUP, D_GROUP), 0) // DV
        seg_c = lax.broadcasted_iota(jnp.int32, (D_GROUP, D_GROUP), 1) // DV
        seg = jnp.where(seg_r == seg_c, 1.0 / DV, 0.0).astype(F32)
        ms = jnp.dot(od * od, seg, preferred_element_type=F32, precision=lax.Precision.HIGHEST)
        y = od * lax.rsqrt(ms + LN_EPS) * subg_ref[...] * (1.0 - lam_init)
        for t in range(t_new):
            o_ref[pl.ds(t * db + b, 1), :] = y[t:t + 1, :]


def _attn_sample(page_table, q, kn, vn, cache_kt, cache_vt, layer, lw, *, db, t_new, lam_init):
    n_pages = page_table.shape[1]
    page = cache_kt.shape[-1]
    g_pages = min(PAGES_PER_STEP, n_pages)
    m = db * t_new
    n_rows = N_HEADS * 2 * t_new
    full = lambda b, g, pt: (0, 0)
    lvec = pl.BlockSpec((1, DK), full)

    def page_spec(idx):
        return pl.BlockSpec((1, 1, D_GROUP, page), lambda b, g, pt: (layer, pt[b, g * g_pages + idx], 0, 0))

    in_specs = [pl.BlockSpec((m, D_GROUP), full)] * 3 + [lvec] * 4 + [pl.BlockSpec((1, D_GROUP), full)]
    in_specs += [page_spec(idx) for idx in range(g_pages)] * 2
    grid_spec = pltpu.PrefetchScalarGridSpec(
        num_scalar_prefetch=1,
        grid=(db, n_pages // g_pages),
        in_specs=in_specs,
        out_specs=pl.BlockSpec((m, D_GROUP), full),
        scratch_shapes=[pltpu.VMEM((n_rows, D_GROUP), BF16), pltpu.VMEM((n_rows, 1), F32),
                        pltpu.VMEM((n_rows, 1), F32), pltpu.VMEM((n_rows, D_GROUP), F32)],
    )
    return pl.pallas_call(
        functools.partial(_attn_sample_kernel, db=db, t_new=t_new, page=page, g_pages=g_pages,
                          past_len=n_pages * page, lam_init=lam_init),
        grid_spec=grid_spec,
        out_shape=jax.ShapeDtypeStruct((m, D_GROUP), F32),
        compiler_params=pltpu.CompilerParams(
            dimension_semantics=("arbitrary", "arbitrary"), vmem_limit_bytes=VMEM_LIMIT_BYTES),
        name="attn_sample",
    )(page_table, q, kn, vn, lw["lambda_q1"], lw["lambda_k1"], lw["lambda_q2"], lw["lambda_k2"], lw["attn_sub_g4"],
      *([cache_kt] * g_pages), *([cache_vt] * g_pages))


def _post_kernel(x_ref, mix_ref, yc_ref, wo3_ref, woc_ref, gpost_ref, gfpre_ref, wg_ref, wu_ref, wd_ref,
                 gfpost_ref, o_ref, hn_s, acc_s):
    mixo = jnp.dot(mix_ref[...], wo3_ref[...], preferred_element_type=F32)
    mixo = mixo + jnp.dot(yc_ref[...].astype(BF16), woc_ref[...], preferred_element_type=F32)
    x1 = x_ref[...] + _rms(mixo, gpost_ref[...], RMS_EPS)
    hn_s[...] = _rms(x1, gfpre_ref[...], RMS_EPS).astype(BF16)
    acc_s[...] = jnp.zeros(acc_s.shape, F32)

    def chunk(c, carry):
        hn = hn_s[...]
        gate = jnp.dot(hn, wg_ref[c], preferred_element_type=F32)
        up = jnp.dot(hn, wu_ref[c], preferred_element_type=F32)
        act = (gate * _sigmoid(gate) * up).astype(BF16)
        acc_s[...] += jnp.dot(act, wd_ref[c], preferred_element_type=F32)
        return carry

    lax.fori_loop(0, N_FF_CHUNKS, chunk, 0)
    o_ref[...] = x1 + _rms(acc_s[...], gfpost_ref[...], RMS_EPS)


def _post(x2d, mix3, yc, lw, *, tm):
    m = x2d.shape[0]
    row = lambda i: (i, 0)
    c2 = lambda i: (0, 0)
    c3 = lambda i: (0, 0, 0)
    vec = pl.BlockSpec((1, D_MODEL), c2)
    single = pl.Buffered(1)
    return pl.pallas_call(
        _post_kernel,
        grid=(m // tm,),
        in_specs=[pl.BlockSpec((tm, D_MODEL), row),
                  pl.BlockSpec((tm, 3 * D_GROUP), row),
                  pl.BlockSpec((tm, D_GROUP), row),
                  pl.BlockSpec((3 * D_GROUP, D_MODEL), c2, pipeline_mode=single),
                  pl.BlockSpec((D_GROUP, D_MODEL), c2, pipeline_mode=single),
                  vec, vec,
                  pl.BlockSpec((N_FF_CHUNKS, D_MODEL, FF_CHUNK), c3, pipeline_mode=single),
                  pl.BlockSpec((N_FF_CHUNKS, D_MODEL, FF_CHUNK), c3, pipeline_mode=single),
                  pl.BlockSpec((N_FF_CHUNKS, FF_CHUNK, D_MODEL), c3, pipeline_mode=single),
                  vec],
        out_specs=pl.BlockSpec((tm, D_MODEL), row),
        out_shape=jax.ShapeDtypeStruct((m, D_MODEL), F32),
        scratch_shapes=[pltpu.VMEM((tm, D_MODEL), BF16), pltpu.VMEM((tm, D_MODEL), F32)],
        compiler_params=pltpu.CompilerParams(
            dimension_semantics=("arbitrary",), vmem_limit_bytes=VMEM_LIMIT_BYTES),
        name="post",
    )(x2d, mix3, yc, lw["w_out3"], lw["w_outc"], lw["g_mix_post"], lw["g_ffn_pre"], lw["w_gate"], lw["w_up"],
      lw["w_down"], lw["g_ffn_post"])


def _prepare_weights(w_in, w_out, conv_w, conv_b, conv_ln_g, conv_ln_b, sgu_w, sgu_b, sgu_ln_g, sgu_ln_b,
                     lambda_q1, lambda_k1, lambda_q2, lambda_k2, attn_sub_g, pool_w, pool_scale, g_mix_pre,
                     g_mix_post, g_ffn_pre, g_ffn_post, w_gu, w_down, t_new):
    depth = w_in.shape[0]
    k_lo, xd_lo = 5 * D_GROUP, 7 * D_GROUP
    w_in_b = w_in.astype(BF16)
    w_main_prompt = jnp.concatenate([w_in_b[:, :, :k_lo], w_in_b[:, :, xd_lo:]], axis=2)
    w_kvt = jnp.swapaxes(w_in_b[:, :, k_lo:xd_lo], 1, 2)
    w_out_b = w_out.astype(BF16)
    w_out3 = jnp.concatenate([w_out_b[:, :2 * D_GROUP], w_out_b[:, 3 * D_GROUP:]], axis=1)
    w_outc = w_out_b[:, 2 * D_GROUP:3 * D_GROUP]
    w_gu_b = w_gu.astype(BF16)
    to_chunks = lambda w: jnp.swapaxes(w.reshape(depth, D_MODEL, N_FF_CHUNKS, FF_CHUNK), 1, 2)
    w_gate = to_chunks(w_gu_b[:, :, :D_FF])
    w_up = to_chunks(w_gu_b[:, :, D_FF:])
    w_down_c = w_down.astype(BF16).reshape(depth, N_FF_CHUNKS, FF_CHUNK, D_MODEL)
    hd = D_GROUP // SGU_HEADS
    sgu_b_full = jnp.repeat(jnp.swapaxes(sgu_b, 1, 2), hd, axis=2)
    sgu_w4 = jnp.repeat(jnp.transpose(sgu_w[:, :, :t_new, :t_new], (0, 2, 3, 1)), hd, axis=3)
    sgu_w4 = sgu_w4.reshape(depth, t_new * t_new, D_GROUP)
    eye = jnp.eye(len(POOL_WINDOWS), dtype=F32)
    pool_w_bd = jnp.einsum("lgce,gh->lgche", pool_w, eye).reshape(depth, D_GROUP, D_GROUP).astype(BF16)
    row2 = lambda a: a[:, None, :]
    layers = []
    for l in range(depth):
        layers.append({
            "w_main_prompt": w_main_prompt[l], "w_kvt": w_kvt[l], "w_in_full": w_in_b[l],
            "w_out3": w_out3[l], "w_outc": w_outc[l], "w_gate": w_gate[l], "w_up": w_up[l], "w_down": w_down_c[l],
            "conv_w": conv_w[l], "conv_b": row2(conv_b)[l], "conv_ln_g": row2(conv_ln_g)[l],
            "conv_ln_b": row2(conv_ln_b)[l], "sgu_w": sgu_w[l], "sgu_b_full": sgu_b_full[l], "sgu_w4": sgu_w4[l],
            "sgu_ln_g": row2(sgu_ln_g)[l], "sgu_ln_b": row2(sgu_ln_b)[l],
            "lambda_q1": row2(lambda_q1)[l], "lambda_k1": row2(lambda_k1)[l], "lambda_q2": row2(lambda_q2)[l],
            "lambda_k2": row2(lambda_k2)[l], "attn_sub_g": row2(attn_sub_g)[l],
            "attn_sub_g4": jnp.tile(row2(attn_sub_g)[l], (1, N_HEADS)),
            "pool_w_bd": pool_w_bd[l], "pool_scale": row2(pool_scale)[l],
            "g_mix_pre": row2(g_mix_pre)[l], "g_mix_post": row2(g_mix_post)[l], "g_ffn_pre": row2(g_ffn_pre)[l],
            "g_ffn_post": row2(g_ffn_post)[l],
        })
    return layers


PROMPT_WIDTHS = (2 * D_GROUP, 2 * D_GROUP, D_GROUP, D_GROUP)
SAMPLE_WIDTHS = (2 * D_GROUP, 2 * D_GROUP, D_GROUP, D_GROUP, D_GROUP, D_GROUP)


def kernel(x_prompt, x_sample, cache_k, cache_v, state_conv, state_pool, page_table, w_in, w_out, conv_w, conv_b, conv_ln_g, conv_ln_b, sgu_w, sgu_b, sgu_ln_g, sgu_ln_b, lambda_q1, lambda_k1, lambda_q2, lambda_k2, attn_sub_g, pool_w, pool_scale, g_mix_pre, g_mix_post, g_ffn_pre, g_ffn_post, w_gu, w_down):
    batch, seq, _ = x_prompt.shape
    db, t_new, _ = x_sample.shape
    depth, n_pool, page = cache_k.shape[0], cache_k.shape[1], cache_k.shape[2]
    past_len = page_table.shape[1] * page
    assert seq % ROW_TILE == 0 and seq % SEQ_TILE == 0 and seq % ATT_TILE == 0 and db % SUBLANES == 0
    assert page == LANES and page_table.shape[1] % min(PAGES_PER_STEP, page_table.shape[1]) == 0

    layers = _prepare_weights(w_in, w_out, conv_w, conv_b, conv_ln_g, conv_ln_b, sgu_w, sgu_b, sgu_ln_g, sgu_ln_b,
                              lambda_q1, lambda_k1, lambda_q2, lambda_k2, attn_sub_g, pool_w, pool_scale,
                              g_mix_pre, g_mix_post, g_ffn_pre, g_ffn_post, w_gu, w_down, t_new)

    cache_kt = jnp.transpose(cache_k, (0, 1, 3, 4, 5, 2)).reshape(depth, n_pool, D_GROUP, page)
    cache_vt = jnp.transpose(cache_v, (0, 1, 3, 4, 2)).reshape(depth, n_pool, D_GROUP, page)
    ms = db * t_new
    xs = jnp.swapaxes(x_sample, 0, 1).reshape(ms, D_MODEL)
    sc_tm = jnp.swapaxes(state_conv, 1, 2)
    sp_tm = jnp.swapaxes(state_pool, 1, 2)
    xp = x_prompt.reshape(batch * seq, D_MODEL)

    kp_l, vp_l, cp_l, pp_l = [], [], [], []
    ks_l, vs_l, cs_l, ps_l, gs_l = [], [], [], [], []
    for l in range(depth):
        lw = layers[l]
        lam_init = 0.8 - 0.6 * math.exp(-0.3 * l)
        za, zb, q, xd, kt, vt = _in_proj(xp, lw["g_mix_pre"], lw["w_main_prompt"], lw["w_kvt"], batch=batch,
                                         seq=seq, widths=PROMPT_WIDTHS, tm=ROW_TILE)
        mix3, conv_new, pool_new = _mix_prompt(za, zb, xd, lw, batch=batch, seq=seq)
        yc = _attn_prompt(q, kt, vt, lw, batch=batch, seq=seq, lam_init=lam_init)
        xp = _post(xp, mix3, yc, lw, tm=ROW_TILE)
        kp_l.append(kt); vp_l.append(vt); cp_l.append(conv_new); pp_l.append(pool_new)
        za, zb, q, kn, vn, xd = _in_proj(xs, lw["g_mix_pre"], lw["w_in_full"], None, batch=1, seq=ms,
                                         widths=SAMPLE_WIDTHS, tm=ms)
        mix3, conv_new, pool_new, sgu_v = _mix_sample(za, zb, xd, sc_tm[l], sp_tm[l], lw, db=db, t_new=t_new,
                                                      past_len=past_len)
        yc = _attn_sample(page_table, q, kn, vn, cache_kt, cache_vt, l, lw, db=db, t_new=t_new, lam_init=lam_init)
        xs = _post(xs, mix3, yc, lw, tm=ms)
        ks_l.append(kn); vs_l.append(vn); cs_l.append(conv_new); ps_l.append(pool_new); gs_l.append(sgu_v)

    y_prompt = xp.reshape(batch, seq, D_MODEL)
    y_sample = jnp.swapaxes(xs.reshape(t_new, db, D_MODEL), 0, 1)
    k_prompt = jnp.transpose(jnp.stack(kp_l).reshape(depth, batch, N_HEADS, 2, DK, seq), (0, 1, 5, 2, 3, 4))
    v_prompt = jnp.transpose(jnp.stack(vp_l).reshape(depth, batch, N_HEADS, DV, seq), (0, 1, 4, 2, 3))
    tm_to_bm = lambda a: jnp.swapaxes(jnp.stack(a).reshape(depth, t_new, db, D_GROUP), 1, 2)
    k_sample = tm_to_bm(ks_l).reshape(depth, db, t_new, N_HEADS, 2, DK)
    v_sample = tm_to_bm(vs_l).reshape(depth, db, t_new, N_HEADS, DV)
    conv_sample = jnp.swapaxes(jnp.stack(cs_l), 1, 2)
    pool_sample = jnp.swapaxes(jnp.stack(ps_l), 1, 2)
    sgu_v_sample = tm_to_bm(gs_l)
    return (y_prompt, y_sample, k_prompt, v_prompt, jnp.stack(cp_l), jnp.stack(pp_l),
            k_sample, v_sample, conv_sample, pool_sample, sgu_v_sample)
```

```python
import functools
import math

import jax
import jax.numpy as jnp
from jax import lax
from jax.experimental import pallas as pl
from jax.experimental.pallas import tpu as pltpu

F32 = jnp.float32
BF16 = jnp.bfloat16

D_MODEL = 1024
D_GROUP = 256
D_IN = 8 * D_GROUP
CONV_W = 31
CONV_HALO = CONV_W - 1
SGU_CHUNK = 128
SGU_HEADS = 4
N_HEADS = 4
DK = 32
DV = 64
POOL_WINDOWS = (2, 4, 8, 16)
POOL_HALO = max(POOL_WINDOWS) - 1
POOL_GD = D_GROUP // len(POOL_WINDOWS)
D_FF = 2816
FF_CHUNK = 256
N_FF_CHUNKS = D_FF // FF_CHUNK
RMS_EPS = 1e-6
LN_EPS = 1e-5
Q_SCALE = DK ** -0.5
LOG2E = math.log2(math.e)

SUBLANES = 8
LANES = 128
VMEM_LIMIT_BYTES = 56 * 1024 * 1024

ROW_TILE = 512
SEQ_TILE = 256
CONV_ROWS = 64
ATT_TILE = 256
PAGES_PER_STEP = 16
DECODE_STREAMS = 2

NT_DIMS = (((1,), (1,)), ((), ()))


def _rms(x, g, eps):
    return x * lax.rsqrt(jnp.mean(x * x, axis=-1, keepdims=True) + eps) * g


def _layer_norm(x, g, b):
    mu = jnp.mean(x, axis=-1, keepdims=True)
    xc = x - mu
    var = jnp.mean(xc * xc, axis=-1, keepdims=True)
    return xc * lax.rsqrt(var + LN_EPS) * g + b


def _gelu(x):
    return 0.5 * x * (1.0 + lax.erf(x * (2.0 ** -0.5)))


def _sigmoid(x):
    return 1.0 / (1.0 + jnp.exp(-x))


def _lane_group_select(pieces, group_width, shape):
    lane = lax.broadcasted_iota(jnp.int32, shape, len(shape) - 1)
    out = pieces[-1]
    for g in range(len(pieces) - 2, -1, -1):
        out = jnp.where(lane < (g + 1) * group_width, pieces[g], out)
    return out


def _layer_spec(shape, layer):
    zeros = (0,) * len(shape)
    return pl.BlockSpec((1,) + tuple(shape), lambda *_: (layer,) + zeros)


def _layer_spec_single(shape, layer):
    zeros = (0,) * len(shape)
    return pl.BlockSpec((1,) + tuple(shape), lambda *_: (layer,) + zeros, pipeline_mode=pl.Buffered(1))


PROMPT_GROUPS = ((0, 2 * D_GROUP), (2 * D_GROUP, 2 * D_GROUP), (5 * D_GROUP, D_GROUP), (7 * D_GROUP, D_GROUP))
SAMPLE_GROUPS = PROMPT_GROUPS[:2] + ((4 * D_GROUP, D_GROUP), (5 * D_GROUP, D_GROUP), (6 * D_GROUP, D_GROUP),
                                     (7 * D_GROUP, D_GROUP))
Q_COL, K_COL, V_COL = 4 * D_GROUP, 5 * D_GROUP, 6 * D_GROUP


def _in_proj_kernel(x_ref, g_ref, w_ref, *rest, groups, feature_major, n_alias):
    out_refs = rest[n_alias:]
    h = _rms(x_ref[...], g_ref[0], RMS_EPS).astype(BF16)
    z = jnp.dot(h, w_ref[0], preferred_element_type=F32)
    for o_ref, (col, width) in zip(out_refs, groups):
        o_ref[...] = z[:, col:col + width]
    if feature_major:
        qt_ref, kt_ref, vt_ref = out_refs[len(groups):]
        qt_ref[0] = (z[:, Q_COL:Q_COL + D_GROUP] * (Q_SCALE * LOG2E)).T.astype(BF16)
        kt_ref[0, 0] = z[:, K_COL:K_COL + D_GROUP].T
        vt_ref[0, 0] = z[:, V_COL:V_COL + D_GROUP].T


def _in_proj(x2d, g, w_in, kv_prev, *, layer, batch, seq, groups, feature_major, tm):
    m = batch * seq
    ns = seq // tm
    depth = w_in.shape[0]
    row_map = lambda b, i: (b * ns + i, 0)
    in_specs = [pl.BlockSpec((tm, D_MODEL), row_map), _layer_spec((1, D_MODEL), layer),
                _layer_spec((D_MODEL, D_IN), layer)]
    args = [x2d, g, w_in]
    out_shape = [jax.ShapeDtypeStruct((m, w), F32) for _, w in groups]
    out_specs = [pl.BlockSpec((tm, w), row_map) for _, w in groups]
    aliases = {}
    n_alias = 0
    if feature_major:
        out_shape.append(jax.ShapeDtypeStruct((batch, D_GROUP, seq), BF16))
        out_specs.append(pl.BlockSpec((1, D_GROUP, tm), lambda b, i: (b, 0, i)))
        for _ in range(2):
            out_shape.append(jax.ShapeDtypeStruct((depth, batch, D_GROUP, seq), F32))
            out_specs.append(pl.BlockSpec((1, 1, D_GROUP, tm), lambda b, i: (layer, b, 0, i)))
        if kv_prev is not None:
            n_alias = len(kv_prev)
            for a, buf in enumerate(kv_prev):
                aliases[len(args)] = len(groups) + 1 + a
                in_specs.append(pl.BlockSpec(memory_space=pl.ANY))
                args.append(buf)
    return pl.pallas_call(
        functools.partial(_in_proj_kernel, groups=groups, feature_major=feature_major, n_alias=n_alias),
        grid=(batch, ns),
        in_specs=in_specs,
        out_specs=out_specs,
        out_shape=out_shape,
        input_output_aliases=aliases,
        compiler_params=pltpu.CompilerParams(
            dimension_semantics=("arbitrary", "arbitrary"), vmem_limit_bytes=VMEM_LIMIT_BYTES),
        name="in_proj",
    )(*args)


def _conv_post(acc, clg, clb):
    y = _layer_norm(acc, clg, clb)
    return y * _sigmoid(y)


def _pool_select(wins, shape):
    return _lane_group_select(wins, POOL_GD, shape)


HPAD = 32
XPAD = 24


def _mix_prompt_kernel(za_ref, zb_ref, xd_ref, cw_ref, cb_ref, clg_ref, clb_ref, sw_ref, sb_ref, slg_ref,
                       slb_ref, pw_ref, ps_ref, mix_ref, convn_ref, pooln_ref, hext, hshift, xext, s2, s4, s8, *, ts):
    i = pl.program_id(1)

    @pl.when(i == 0)
    def _():
        hext[0:HPAD, :] = jnp.zeros((HPAD, D_GROUP), F32)
        xext[0:XPAD, :] = jnp.zeros((XPAD, D_GROUP), F32)
        for buf in (s2, s4, s8):
            buf[0:SUBLANES, :] = jnp.zeros((SUBLANES, D_GROUP), F32)

    hext[HPAD:HPAD + ts, :] = za_ref[:, 0:D_GROUP] * _sigmoid(za_ref[:, D_GROUP:2 * D_GROUP])
    slab = ts + HPAD - SUBLANES
    for rho in range(1, SUBLANES):
        hshift[rho - 1] = hext[rho:rho + slab, :]
    for c in range(ts // CONV_ROWS):
        r0 = c * CONV_ROWS
        acc = jnp.broadcast_to(cb_ref[0], (CONV_ROWS, D_GROUP))
        for j in range(CONV_W):
            shift = HPAD - CONV_HALO + j
            q8, rho = shift // SUBLANES * SUBLANES, shift % SUBLANES
            if rho == 0:
                rows = hext[r0 + q8:r0 + q8 + CONV_ROWS, :]
            else:
                rows = hshift[rho - 1, r0 + q8:r0 + q8 + CONV_ROWS, :]
            acc = acc + cw_ref[0, j:j + 1, :] * rows
        mix_ref[r0:r0 + CONV_ROWS, 0:D_GROUP] = _conv_post(acc, clg_ref[0], clb_ref[0]).astype(BF16)

    u = _gelu(zb_ref[:, 0:D_GROUP])
    v = _layer_norm(_gelu(zb_ref[:, D_GROUP:2 * D_GROUP]), slg_ref[0], slb_ref[0])
    tri_r = lax.broadcasted_iota(jnp.int32, (SGU_CHUNK, SGU_CHUNK), 0)
    tri_c = lax.broadcasted_iota(jnp.int32, (SGU_CHUNK, SGU_CHUNK), 1)
    w_heads = [jnp.where(tri_c <= tri_r, sw_ref[0, g], 0.0).astype(BF16) for g in range(SGU_HEADS)]
    w_stack = jnp.concatenate(w_heads, axis=0)
    hd = D_GROUP // SGU_HEADS
    for c in range(ts // SGU_CHUNK):
        rows = slice(c * SGU_CHUNK, (c + 1) * SGU_CHUNK)
        r = jnp.dot(w_stack, v[rows].astype(BF16), preferred_element_type=F32)
        mixed = _lane_group_select([r[g * SGU_CHUNK:(g + 1) * SGU_CHUNK] for g in range(SGU_HEADS)],
                                   hd, (SGU_CHUNK, D_GROUP))
        mix_ref[rows, D_GROUP:2 * D_GROUP] = (u[rows] * (mixed + sb_ref[0])).astype(BF16)

    n = XPAD + ts
    xext[XPAD:n, :] = xd_ref[...]
    s2[SUBLANES:n, :] = xext[SUBLANES:n, :] + xext[SUBLANES - 1:n - 1, :]
    s4[SUBLANES:n, :] = s2[SUBLANES:n, :] + s2[SUBLANES - 2:n - 2, :]
    s8[2 * SUBLANES:n, :] = s4[2 * SUBLANES:n, :] + s4[2 * SUBLANES - 4:n - 4, :]
    win16 = s8[XPAD:n, :] + s8[XPAD - 8:n - 8, :]
    x0 = xext[XPAD:n, :]
    wins = _pool_select([s2[XPAD:n, :], s4[XPAD:n, :], s8[XPAD:n, :], win16], (ts, D_GROUP))
    wvec = _pool_select([jnp.full((1, D_GROUP), float(w), F32) for w in POOL_WINDOWS], (1, D_GROUP))
    pos = (i * ts + lax.broadcasted_iota(jnp.int32, (ts, D_GROUP), 0)).astype(F32)
    d = wins / jnp.minimum(pos + 1.0, wvec) - x0
    y = jnp.dot(d.astype(BF16), pw_ref[0], preferred_element_type=F32) * ps_ref[0]
    mix_ref[:, 2 * D_GROUP:3 * D_GROUP] = y.astype(BF16)

    hext[0:HPAD, :] = hext[ts:ts + HPAD, :]
    xext[SUBLANES:XPAD, :] = xext[ts + SUBLANES:ts + XPAD, :]

    @pl.when(i == pl.num_programs(1) - 1)
    def _():
        convn_ref[0] = hext[HPAD - CONV_HALO:HPAD, :]
        pooln_ref[0] = xext[XPAD - POOL_HALO:XPAD, :]


def _mix_prompt(za, zb, xd, wts, layer, *, batch, seq):
    ts = SEQ_TILE
    ns = seq // ts
    m = batch * seq
    row_map = lambda b, i: (b * ns + i, 0)
    vec = _layer_spec((1, D_GROUP), layer)
    ext = lambda pad: pltpu.VMEM((pad + ts, D_GROUP), F32)
    return pl.pallas_call(
        functools.partial(_mix_prompt_kernel, ts=ts),
        grid=(batch, ns),
        in_specs=[pl.BlockSpec((ts, 2 * D_GROUP), row_map),
                  pl.BlockSpec((ts, 2 * D_GROUP), row_map),
                  pl.BlockSpec((ts, D_GROUP), row_map),
                  _layer_spec((CONV_W, D_GROUP), layer), vec, vec, vec,
                  _layer_spec((SGU_HEADS, SGU_CHUNK, SGU_CHUNK), layer),
                  _layer_spec((SGU_CHUNK, D_GROUP), layer), vec, vec,
                  _layer_spec((D_GROUP, D_GROUP), layer), vec],
        out_specs=[pl.BlockSpec((ts, 3 * D_GROUP), row_map),
                   pl.BlockSpec((1, CONV_HALO, D_GROUP), lambda b, i: (b, 0, 0)),
                   pl.BlockSpec((1, POOL_HALO, D_GROUP), lambda b, i: (b, 0, 0))],
        out_shape=[jax.ShapeDtypeStruct((m, 3 * D_GROUP), BF16),
                   jax.ShapeDtypeStruct((batch, CONV_HALO, D_GROUP), F32),
                   jax.ShapeDtypeStruct((batch, POOL_HALO, D_GROUP), F32)],
        scratch_shapes=[ext(HPAD), pltpu.VMEM((SUBLANES - 1, ts + HPAD - SUBLANES, D_GROUP), F32),
                        ext(XPAD), ext(XPAD), ext(XPAD), ext(XPAD)],
        compiler_params=pltpu.CompilerParams(
            dimension_semantics=("arbitrary", "arbitrary"), vmem_limit_bytes=VMEM_LIMIT_BYTES),
        name="mix_prompt",
    )(za, zb, xd, wts["conv_w"], wts["conv_b"], wts["conv_ln_g"], wts["conv_ln_b"], wts["sgu_w"], wts["sgu_b_full"],
      wts["sgu_ln_g"], wts["sgu_ln_b"], wts["pool_w_bd"], wts["pool_scale"])


def _mix_sample_kernel(za_ref, zb_ref, xd_ref, sc_ref, sp_ref, cw_ref, cb_ref, clg_ref, clb_ref, sw4_ref, sb_ref,
                       slg_ref, slb_ref, pw_ref, ps_ref, mix_ref, convn_ref, pooln_ref, sguv_ref, *, db, t_new,
                       past_len):
    rows = lambda t: slice(t * db, (t + 1) * db)

    ext = [sc_ref[0, j] for j in range(CONV_HALO)]
    ext += [za_ref[rows(t), 0:D_GROUP] * _sigmoid(za_ref[rows(t), D_GROUP:2 * D_GROUP]) for t in range(t_new)]
    for t in range(t_new):
        acc = jnp.broadcast_to(cb_ref[0], (db, D_GROUP))
        for j in range(CONV_W):
            acc = acc + cw_ref[0, j:j + 1, :] * ext[t + j]
        mix_ref[rows(t), 0:D_GROUP] = _conv_post(acc, clg_ref[0], clb_ref[0]).astype(BF16)
    for j in range(CONV_HALO):
        convn_ref[j] = ext[t_new + j]

    u = _gelu(zb_ref[:, 0:D_GROUP])
    v = _layer_norm(_gelu(zb_ref[:, D_GROUP:2 * D_GROUP]), slg_ref[0], slb_ref[0])
    sguv_ref[...] = v
    for t in range(t_new):
        mixed = jnp.broadcast_to(sb_ref[0, t:t + 1, :], (db, D_GROUP))
        for s in range(t + 1):
            mixed = mixed + sw4_ref[0, t * t_new + s:t * t_new + s + 1, :] * v[rows(s)]
        mix_ref[rows(t), D_GROUP:2 * D_GROUP] = (u[rows(t)] * mixed).astype(BF16)

    pext = [sp_ref[0, j] for j in range(POOL_HALO)] + [xd_ref[rows(t), :] for t in range(t_new)]
    ds = []
    for t in range(t_new):
        x0 = pext[POOL_HALO + t]
        acc = x0
        wins = []
        for k in range(1, POOL_HALO + 1):
            acc = acc + pext[POOL_HALO + t - k]
            if k + 1 in POOL_WINDOWS:
                wins.append(acc)
        cnts = [jnp.full((1, D_GROUP), float(min(past_len + t + 1, w)), F32) for w in POOL_WINDOWS]
        cnt = _pool_select(cnts, (1, D_GROUP))
        ds.append(_pool_select(wins, (db, D_GROUP)) / cnt - x0)
    d = jnp.concatenate(ds, axis=0)
    y = jnp.dot(d.astype(BF16), pw_ref[0], preferred_element_type=F32) * ps_ref[0]
    mix_ref[:, 2 * D_GROUP:3 * D_GROUP] = y.astype(BF16)
    for j in range(POOL_HALO):
        pooln_ref[j] = pext[t_new + j]


def _mix_sample(za, zb, xd, sc_tm, sp_tm, wts, layer, *, db, t_new, past_len):
    m = db * t_new
    full = lambda shape: pl.BlockSpec(shape, lambda i: (0,) * len(shape))
    vec = _layer_spec((1, D_GROUP), layer)
    return pl.pallas_call(
        functools.partial(_mix_sample_kernel, db=db, t_new=t_new, past_len=past_len),
        grid=(1,),
        in_specs=[full((m, 2 * D_GROUP)), full((m, 2 * D_GROUP)), full((m, D_GROUP)),
                  _layer_spec((CONV_HALO, db, D_GROUP), layer), _layer_spec((POOL_HALO, db, D_GROUP), layer),
                  _layer_spec((CONV_W, D_GROUP), layer), vec, vec, vec,
                  _layer_spec((t_new * t_new, D_GROUP), layer),
                  _layer_spec((SGU_CHUNK, D_GROUP), layer), vec, vec,
                  _layer_spec((D_GROUP, D_GROUP), layer), vec],
        out_specs=[full((m, 3 * D_GROUP)), full((CONV_HALO, db, D_GROUP)), full((POOL_HALO, db, D_GROUP)),
                   full((m, D_GROUP))],
        out_shape=[jax.ShapeDtypeStruct((m, 3 * D_GROUP), BF16),
                   jax.ShapeDtypeStruct((CONV_HALO, db, D_GROUP), F32),
                   jax.ShapeDtypeStruct((POOL_HALO, db, D_GROUP), F32),
                   jax.ShapeDtypeStruct((m, D_GROUP), F32)],
        compiler_params=pltpu.CompilerParams(
            dimension_semantics=("arbitrary",), vmem_limit_bytes=VMEM_LIMIT_BYTES),
        name="mix_sample",
    )(za, zb, xd, sc_tm, sp_tm, wts["conv_w"], wts["conv_b"], wts["conv_ln_g"], wts["conv_ln_b"], wts["sgu_w4"],
      wts["sgu_b_full"], wts["sgu_ln_g"], wts["sgu_ln_b"], wts["pool_w_bd"], wts["pool_scale"])


def _diff_lambda(lq1, lk1, lq2, lk2, lam_init):
    s1 = jnp.sum(lq1 * lk1, axis=-1, keepdims=True)
    s2 = jnp.sum(lq2 * lk2, axis=-1, keepdims=True)
    return jnp.exp(s1) - jnp.exp(s2) + lam_init


def _alibi_slope(h):
    return 2.0 ** (-8.0 * (h + 1) / N_HEADS)


KEY_AUG = LANES
VAL_AUG = 80
N_BIAS = 3


def _attn_prompt_kernel(qt_ref, k_ref, vt_ref, lq1_ref, lk1_ref, lq2_ref, lk2_ref, subg_ref, o_ref, kaug, vaug, *,
                        tq, nblk, lam_init):
    i = pl.program_id(1)
    hd = 2 * DK

    @pl.when(i == 0)
    def _():
        key = lax.broadcasted_iota(jnp.int32, (tq, KEY_AUG), 0).astype(F32)
        lane = lax.broadcasted_iota(jnp.int32, (tq, KEY_AUG), 1)
        ones_row = jnp.where(lax.broadcasted_iota(jnp.int32, (VAL_AUG - hd, tq), 0) == 0, 1.0, 0.0).astype(BF16)
        for h in range(N_HEADS):
            bias = (_alibi_slope(h) * LOG2E) * key
            hi = bias.astype(BF16).astype(F32)
            mid = (bias - hi).astype(BF16).astype(F32)
            lo = ((bias - hi) - mid).astype(BF16).astype(F32)
            aug = jnp.where(lane == hd, hi, jnp.where(lane == hd + 1, mid, jnp.where(lane == hd + 2, lo, 0.0)))
            for j in range(nblk):
                kblk = k_ref[j * tq:(j + 1) * tq, h * hd:(h + 1) * hd]
                kpad = jnp.concatenate([kblk, jnp.zeros((tq, KEY_AUG - hd), F32)], axis=1)
                kaug[j, h] = jnp.where(lane < hd, kpad, aug).astype(BF16)
                vaug[j, h, 0:hd, :] = vt_ref[0, 0, h * hd:(h + 1) * hd, j * tq:(j + 1) * tq].astype(BF16)
                vaug[j, h, hd:VAL_AUG, :] = ones_row

    lam = _diff_lambda(lq1_ref[0], lk1_ref[0], lq2_ref[0], lk2_ref[0], lam_init)
    qt = qt_ref[0]
    zero = jnp.zeros((DK, tq), BF16)
    ones_aug = jnp.where(lax.broadcasted_iota(jnp.int32, (KEY_AUG - hd, 2 * tq), 0) < N_BIAS, 1.0, 0.0).astype(BF16)
    qs = []
    for h in range(N_HEADS):
        q1 = qt[h * hd:h * hd + DK]
        q2 = qt[h * hd + DK:(h + 1) * hd]
        body = jnp.concatenate([jnp.concatenate([q1, zero], axis=0), jnp.concatenate([zero, q2], axis=0)], axis=1)
        qs.append(jnp.concatenate([body, ones_aug], axis=0))
    key = lax.broadcasted_iota(jnp.int32, (tq, 2 * tq), 0)
    qry = lax.broadcasted_iota(jnp.int32, (tq, 2 * tq), 1)
    causal = key <= jnp.where(qry >= tq, qry - tq, qry)

    def block(j, carry, masked):
        scores = [jnp.dot(kaug[j, h], qs[h], preferred_element_type=F32) for h in range(N_HEADS)]
        probs = []
        for h in range(N_HEADS):
            s = jnp.where(causal, scores[h], -jnp.inf) if masked else scores[h]
            c = (_alibi_slope(h) * LOG2E) * ((j - i) * tq).astype(F32)
            m = carry[h][0]
            m_new = jnp.maximum(m, jnp.max(s, axis=0, keepdims=True) + c)
            probs.append((m_new, jnp.exp2(m - m_new), jnp.exp2(s - (m_new - c)).astype(BF16)))
        out = []
        for h in range(N_HEADS):
            m_new, alpha, p = probs[h]
            out.append((m_new, alpha * carry[h][1] + jnp.dot(vaug[j, h], p, preferred_element_type=F32)))
        return tuple(out)

    init = tuple((jnp.full((1, 2 * tq), -jnp.inf, F32), jnp.zeros((VAL_AUG, 2 * tq), F32)) for _ in range(N_HEADS))
    carry = lax.fori_loop(0, i, functools.partial(block, masked=False), init)
    carry = block(i, carry, True)
    ys = []
    for h in range(N_HEADS):
        acc = carry[h][1]
        o = acc[0:hd, :] * (1.0 / acc[hd:hd + 1, :])
        od = o[:, 0:tq] - lam * o[:, tq:2 * tq]
        ms = jnp.mean(od * od, axis=0, keepdims=True)
        ys.append(od * lax.rsqrt(ms + LN_EPS) * subg_ref[0] * (1.0 - lam_init))
    o_ref[...] = jnp.concatenate(ys, axis=0).T.astype(BF16)


def _attn_prompt(qt, k_rows, vt_all, wts, layer, *, batch, seq, lam_init):
    tq = ATT_TILE
    ns = seq // tq
    lvec = _layer_spec((1, DK), layer)
    return pl.pallas_call(
        functools.partial(_attn_prompt_kernel, tq=tq, nblk=ns, lam_init=lam_init),
        grid=(batch, ns),
        in_specs=[pl.BlockSpec((1, D_GROUP, tq), lambda b, i: (b, 0, i)),
                  pl.BlockSpec((seq, D_GROUP), lambda b, i: (b, 0)),
                  pl.BlockSpec((1, 1, D_GROUP, seq), lambda b, i: (layer, b, 0, 0)),
                  lvec, lvec, lvec, lvec, _layer_spec((DV, 1), layer)],
        out_specs=pl.BlockSpec((tq, D_GROUP), lambda b, i: (b * ns + i, 0)),
        out_shape=jax.ShapeDtypeStruct((batch * seq, D_GROUP), BF16),
        scratch_shapes=[pltpu.VMEM((ns, N_HEADS, tq, KEY_AUG), BF16), pltpu.VMEM((ns, N_HEADS, VAL_AUG, tq), BF16)],
        compiler_params=pltpu.CompilerParams(
            dimension_semantics=("arbitrary", "arbitrary"), vmem_limit_bytes=VMEM_LIMIT_BYTES),
        name="attn_prompt",
    )(qt, k_rows, vt_all, wts["lambda_q1"], wts["lambda_k1"], wts["lambda_q2"], wts["lambda_k2"],
      wts["attn_sub_g_col"])


def _attn_sample_kernel(pt_ref, q_ref, kn_ref, vn_ref, lq1_ref, lk1_ref, lq2_ref, lk2_ref, subg_ref, *rest, db,
                        t_new, page, g_pages, past_len, lam_init):
    del pt_ref
    k_pages = rest[:g_pages]
    v_pages = rest[g_pages:2 * g_pages]
    o_ref, qbd_s, m_s, l_s, acc_s = rest[2 * g_pages:]
    b = pl.program_id(0)
    g = pl.program_id(1)
    n_rows = N_HEADS * 2 * t_new
    row1 = lax.broadcasted_iota(jnp.int32, (n_rows, 1), 0)
    t_row = (row1 % t_new).astype(F32)
    slope_row = jnp.zeros((n_rows, 1), F32)
    for h in range(N_HEADS):
        slope_row = jnp.where(row1 // (2 * t_new) == h, _alibi_slope(h), slope_row)

    def gather_rows(ref):
        return jnp.concatenate([ref[pl.ds(t * db + b, 1), :] for t in range(t_new)], axis=0)

    @pl.when(g == 0)
    def _():
        q4 = gather_rows(q_ref) * Q_SCALE
        qt = jnp.concatenate([q4] * (2 * N_HEADS), axis=0)
        r = lax.broadcasted_iota(jnp.int32, (n_rows, D_GROUP), 0)
        c = lax.broadcasted_iota(jnp.int32, (n_rows, D_GROUP), 1)
        qbd_s[...] = jnp.where(c // DK == r // t_new, qt, 0.0).astype(BF16)
        m_s[...] = jnp.full(m_s.shape, -jnp.inf, F32)
        l_s[...] = jnp.zeros(l_s.shape, F32)
        acc_s[...] = jnp.zeros(acc_s.shape, F32)

    qbd = qbd_s[...]
    n_streams = m_s.shape[0]
    per_stream = g_pages // n_streams
    width = per_stream * page
    lane_pos = lax.broadcasted_iota(jnp.int32, (n_rows, width), 1)
    scores = [jnp.concatenate([jnp.dot(qbd, kp[0, 0].astype(BF16), preferred_element_type=F32)
                               for kp in k_pages[st * per_stream:(st + 1) * per_stream]], axis=1)
              for st in range(n_streams)]
    probs = []
    for st in range(n_streams):
        kpos = ((g * g_pages + st * per_stream) * page + lane_pos).astype(F32)
        s = scores[st] - slope_row * ((past_len + t_row) - kpos)
        m_prev = m_s[st]
        m_new = jnp.maximum(m_prev, jnp.max(s, axis=-1, keepdims=True))
        alpha = jnp.exp(m_prev - m_new)
        p = jnp.exp(s - m_new)
        l_s[st] = alpha * l_s[st] + jnp.sum(p, axis=-1, keepdims=True)
        m_s[st] = m_new
        probs.append((alpha, p.astype(BF16)))
    for st in range(n_streams):
        alpha, pb = probs[st]
        pv = jnp.zeros((n_rows, D_GROUP), F32)
        for idx, vp in enumerate(v_pages[st * per_stream:(st + 1) * per_stream]):
            pv = pv + lax.dot_general(pb[:, idx * page:(idx + 1) * page], vp[0, 0].astype(BF16), NT_DIMS,
                                      preferred_element_type=F32)
        acc_s[st] = alpha * acc_s[st] + pv

    @pl.when(g == pl.num_programs(1) - 1)
    def _():
        kn = gather_rows(kn_ref)
        vn = gather_rows(vn_ref)
        kn8 = jnp.concatenate([kn, jnp.zeros((SUBLANES - t_new, D_GROUP), F32)], axis=0).astype(BF16)
        sn = lax.dot_general(qbd, kn8, NT_DIMS, preferred_element_type=F32)
        tk = lax.broadcasted_iota(jnp.int32, (n_rows, SUBLANES), 1).astype(F32)
        dist = t_row - tk
        sn = jnp.where(dist >= 0.0, sn - slope_row * dist, -jnp.inf)
        m_fin = jnp.max(sn, axis=-1, keepdims=True)
        for st in range(n_streams):
            m_fin = jnp.maximum(m_fin, m_s[st])
        pn = jnp.exp(sn - m_fin)
        l_fin = jnp.sum(pn, axis=-1, keepdims=True)
        acc = jnp.zeros((n_rows, D_GROUP), F32)
        for st in range(n_streams):
            alpha = jnp.exp(m_s[st] - m_fin)
            l_fin = l_fin + alpha * l_s[st]
            acc = acc + alpha * acc_s[st]
        for t in range(t_new):
            acc = acc + pn[:, t:t + 1] * vn[t:t + 1, :]
        o = acc / l_fin
        r = lax.broadcasted_iota(jnp.int32, (n_rows, D_GROUP), 0)
        c = lax.broadcasted_iota(jnp.int32, (n_rows, D_GROUP), 1)
        o = jnp.where(c // DV == r // (2 * t_new), o, 0.0)
        y0 = jnp.zeros((t_new, D_GROUP), F32)
        y1 = jnp.zeros((t_new, D_GROUP), F32)
        for h in range(N_HEADS):
            y0 = y0 + o[h * 2 * t_new:h * 2 * t_new + t_new]
            y1 = y1 + o[h * 2 * t_new + t_new:(h + 1) * 2 * t_new]
        lam = _diff_lambda(lq1_ref[0], lk1_ref[0], lq2_ref[0], lk2_ref[0], lam_init)
        od = y0 - lam * y1
        seg_r = lax.broadcasted_iota(jnp.int32, (D_GROUP, D_GROUP), 0) // DV
        seg_c = lax.broadcasted_iota(jnp.int32, (D_GROUP, D_GROUP), 1) // DV
        seg = jnp.where(seg_r == seg_c, 1.0 / DV, 0.0).astype(F32)
        ms = jnp.dot(od * od, seg, preferred_element_type=F32, precision=lax.Precision.HIGHEST)
        y = od * lax.rsqrt(ms + LN_EPS) * subg_ref[0] * (1.0 - lam_init)
        for t in range(t_new):
            o_ref[pl.ds(t * db + b, 1), :] = y[t:t + 1, :]


def _attn_sample(page_table, q, kn, vn, cache_kt, cache_vt, wts, layer, *, db, t_new, lam_init):
    n_pages = page_table.shape[1]
    page = cache_kt.shape[-1]
    g_pages = min(PAGES_PER_STEP, n_pages)
    n_streams = DECODE_STREAMS if g_pages % DECODE_STREAMS == 0 else 1
    m = db * t_new
    n_rows = N_HEADS * 2 * t_new
    full = lambda b, g, pt: (0, 0)
    lvec = _layer_spec((1, DK), layer)

    def page_spec(idx):
        return pl.BlockSpec((1, 1, D_GROUP, page), lambda b, g, pt: (layer, pt[b, g * g_pages + idx], 0, 0))

    in_specs = [pl.BlockSpec((m, D_GROUP), full)] * 3 + [lvec] * 4 + [_layer_spec((1, D_GROUP), layer)]
    in_specs += [page_spec(idx) for idx in range(g_pages)] * 2
    grid_spec = pltpu.PrefetchScalarGridSpec(
        num_scalar_prefetch=1,
        grid=(db, n_pages // g_pages),
        in_specs=in_specs,
        out_specs=pl.BlockSpec((m, D_GROUP), full),
        scratch_shapes=[pltpu.VMEM((n_rows, D_GROUP), BF16), pltpu.VMEM((n_streams, n_rows, 1), F32),
                        pltpu.VMEM((n_streams, n_rows, 1), F32), pltpu.VMEM((n_streams, n_rows, D_GROUP), F32)],
    )
    return pl.pallas_call(
        functools.partial(_attn_sample_kernel, db=db, t_new=t_new, page=page, g_pages=g_pages,
                          past_len=n_pages * page, lam_init=lam_init),
        grid_spec=grid_spec,
        out_shape=jax.ShapeDtypeStruct((m, D_GROUP), F32),
        compiler_params=pltpu.CompilerParams(
            dimension_semantics=("arbitrary", "arbitrary"), vmem_limit_bytes=VMEM_LIMIT_BYTES),
        name="attn_sample",
    )(page_table, q, kn, vn, wts["lambda_q1"], wts["lambda_k1"], wts["lambda_q2"], wts["lambda_k2"],
      wts["attn_sub_g4"], *([cache_kt] * g_pages), *([cache_vt] * g_pages))


def _post_kernel(x_ref, mix_ref, yc_ref, wo_ref, gpost_ref, gfpre_ref, wgu_ref, wd_ref, gfpost_ref, o_ref,
                 hn_s, act_s):
    ab, cd = 2 * D_GROUP, 3 * D_GROUP
    mixo = jnp.dot(mix_ref[:, 0:ab], wo_ref[0, 0:ab, :], preferred_element_type=F32)
    mixo = mixo + jnp.dot(yc_ref[...].astype(BF16), wo_ref[0, ab:cd, :], preferred_element_type=F32)
    mixo = mixo + jnp.dot(mix_ref[:, ab:cd], wo_ref[0, cd:D_MODEL, :], preferred_element_type=F32)
    x1 = x_ref[...] + _rms(mixo, gpost_ref[0], RMS_EPS)
    hn_s[...] = _rms(x1, gfpre_ref[0], RMS_EPS).astype(BF16)
    for c in range(N_FF_CHUNKS):
        lo = c * FF_CHUNK
        gate = jnp.dot(hn_s[...], wgu_ref[0, :, lo:lo + FF_CHUNK], preferred_element_type=F32)
        up = jnp.dot(hn_s[...], wgu_ref[0, :, D_FF + lo:D_FF + lo + FF_CHUNK], preferred_element_type=F32)
        act_s[:, lo:lo + FF_CHUNK] = (gate * _sigmoid(gate) * up).astype(BF16)
    ffn = jnp.dot(act_s[...], wd_ref[0], preferred_element_type=F32)
    o_ref[...] = x1 + _rms(ffn, gfpost_ref[0], RMS_EPS)


def _post(x2d, mix3, yc, wts, layer, *, tm):
    m = x2d.shape[0]
    row = lambda i: (i, 0)
    vec = _layer_spec((1, D_MODEL), layer)
    return pl.pallas_call(
        _post_kernel,
        grid=(m // tm,),
        in_specs=[pl.BlockSpec((tm, D_MODEL), row),
                  pl.BlockSpec((tm, 3 * D_GROUP), row),
                  pl.BlockSpec((tm, D_GROUP), row),
                  _layer_spec_single((D_MODEL, D_MODEL), layer), vec, vec,
                  _layer_spec_single((D_MODEL, 2 * D_FF), layer),
                  _layer_spec_single((D_FF, D_MODEL), layer), vec],
        out_specs=pl.BlockSpec((tm, D_MODEL), row),
        out_shape=jax.ShapeDtypeStruct((m, D_MODEL), F32),
        scratch_shapes=[pltpu.VMEM((tm, D_MODEL), BF16), pltpu.VMEM((tm, D_FF), BF16)],
        compiler_params=pltpu.CompilerParams(
            dimension_semantics=("arbitrary",), vmem_limit_bytes=VMEM_LIMIT_BYTES),
        name="post",
    )(x2d, mix3, yc, wts["w_out"], wts["g_mix_post"], wts["g_ffn_pre"], wts["w_gu"], wts["w_down"],
      wts["g_ffn_post"])


def _prepare_weights(w_in, w_out, conv_w, conv_b, conv_ln_g, conv_ln_b, sgu_w, sgu_b, sgu_ln_g, sgu_ln_b,
                     lambda_q1, lambda_k1, lambda_q2, lambda_k2, attn_sub_g, pool_w, pool_scale, g_mix_pre,
                     g_mix_post, g_ffn_pre, g_ffn_post, w_gu, w_down, t_new):
    depth = w_in.shape[0]
    hd = D_GROUP // SGU_HEADS
    sgu_w4 = jnp.repeat(jnp.transpose(sgu_w[:, :, :t_new, :t_new], (0, 2, 3, 1)), hd, axis=3)
    eye = jnp.eye(len(POOL_WINDOWS), dtype=F32)
    row = lambda a: a[:, None, :]
    return {
        "w_in": w_in.astype(BF16), "w_out": w_out.astype(BF16), "w_gu": w_gu.astype(BF16),
        "w_down": w_down.astype(BF16),
        "conv_w": conv_w, "conv_b": row(conv_b), "conv_ln_g": row(conv_ln_g), "conv_ln_b": row(conv_ln_b),
        "sgu_w": sgu_w,
        "sgu_b_full": jnp.repeat(jnp.swapaxes(sgu_b, 1, 2), hd, axis=2),
        "sgu_w4": sgu_w4.reshape(depth, t_new * t_new, D_GROUP),
        "sgu_ln_g": row(sgu_ln_g), "sgu_ln_b": row(sgu_ln_b),
        "lambda_q1": row(lambda_q1), "lambda_k1": row(lambda_k1), "lambda_q2": row(lambda_q2),
        "lambda_k2": row(lambda_k2),
        "attn_sub_g_col": attn_sub_g[:, :, None], "attn_sub_g4": jnp.tile(row(attn_sub_g), (1, 1, N_HEADS)),
        "pool_w_bd": jnp.einsum("lgce,gh->lgche", pool_w, eye).reshape(depth, D_GROUP, D_GROUP).astype(BF16),
        "pool_scale": row(pool_scale),
        "g_mix_pre": row(g_mix_pre), "g_mix_post": row(g_mix_post), "g_ffn_pre": row(g_ffn_pre),
        "g_ffn_post": row(g_ffn_post),
    }


def kernel(x_prompt, x_sample, cache_k, cache_v, state_conv, state_pool, page_table, w_in, w_out, conv_w, conv_b, conv_ln_g, conv_ln_b, sgu_w, sgu_b, sgu_ln_g, sgu_ln_b, lambda_q1, lambda_k1, lambda_q2, lambda_k2, attn_sub_g, pool_w, pool_scale, g_mix_pre, g_mix_post, g_ffn_pre, g_ffn_post, w_gu, w_down):
    batch, seq, _ = x_prompt.shape
    db, t_new, _ = x_sample.shape
    depth, n_pool, page = cache_k.shape[0], cache_k.shape[1], cache_k.shape[2]
    past_len = page_table.shape[1] * page
    assert seq % ROW_TILE == 0 and seq % SEQ_TILE == 0 and seq % ATT_TILE == 0 and db % SUBLANES == 0
    assert page == LANES and page_table.shape[1] % min(PAGES_PER_STEP, page_table.shape[1]) == 0

    wts = _prepare_weights(w_in, w_out, conv_w, conv_b, conv_ln_g, conv_ln_b, sgu_w, sgu_b, sgu_ln_g, sgu_ln_b,
                           lambda_q1, lambda_k1, lambda_q2, lambda_k2, attn_sub_g, pool_w, pool_scale,
                           g_mix_pre, g_mix_post, g_ffn_pre, g_ffn_post, w_gu, w_down, t_new)

    cache_kt = jnp.transpose(cache_k, (0, 1, 3, 4, 5, 2)).reshape(depth, n_pool, D_GROUP, page)
    cache_vt = jnp.transpose(cache_v, (0, 1, 3, 4, 2)).reshape(depth, n_pool, D_GROUP, page)
    ms = db * t_new
    xs = jnp.swapaxes(x_sample, 0, 1).reshape(ms, D_MODEL)
    sc_tm = jnp.swapaxes(state_conv, 1, 2)
    sp_tm = jnp.swapaxes(state_pool, 1, 2)
    xp = x_prompt.reshape(batch * seq, D_MODEL)

    kv_all = None
    cp_l, pp_l = [], []
    ks_l, vs_l, cs_l, ps_l, gs_l = [], [], [], [], []
    for l in range(depth):
        lam_init = 0.8 - 0.6 * math.exp(-0.3 * l)
        za, zb, k_rows, xd, qt, kt_all, vt_all = _in_proj(
            xp, wts["g_mix_pre"], wts["w_in"], kv_all, layer=l, batch=batch, seq=seq, groups=PROMPT_GROUPS,
            feature_major=True, tm=ROW_TILE)
        kv_all = (kt_all, vt_all)
        mix3, conv_new, pool_new = _mix_prompt(za, zb, xd, wts, l, batch=batch, seq=seq)
        yc = _attn_prompt(qt, k_rows, vt_all, wts, l, batch=batch, seq=seq, lam_init=lam_init)
        xp = _post(xp, mix3, yc, wts, l, tm=ROW_TILE)
        cp_l.append(conv_new); pp_l.append(pool_new)
        za, zb, q, kn, vn, xd = _in_proj(xs, wts["g_mix_pre"], wts["w_in"], None, layer=l, batch=1, seq=ms,
                                         groups=SAMPLE_GROUPS, feature_major=False, tm=ms)
        mix3, conv_new, pool_new, sgu_v = _mix_sample(za, zb, xd, sc_tm, sp_tm, wts, l, db=db, t_new=t_new,
                                                      past_len=past_len)
        yc = _attn_sample(page_table, q, kn, vn, cache_kt, cache_vt, wts, l, db=db, t_new=t_new, lam_init=lam_init)
        xs = _post(xs, mix3, yc, wts, l, tm=ms)
        ks_l.append(kn); vs_l.append(vn); cs_l.append(conv_new); ps_l.append(pool_new); gs_l.append(sgu_v)

    kt_all, vt_all = kv_all
    y_prompt = xp.reshape(batch, seq, D_MODEL)
    y_sample = jnp.swapaxes(xs.reshape(t_new, db, D_MODEL), 0, 1)
    k_prompt = jnp.transpose(kt_all.reshape(depth, batch, N_HEADS, 2, DK, seq), (0, 1, 5, 2, 3, 4))
    v_prompt = jnp.transpose(vt_all.reshape(depth, batch, N_HEADS, DV, seq), (0, 1, 4, 2, 3))
    tm_to_bm = lambda a: jnp.swapaxes(jnp.stack(a).reshape(depth, t_new, db, D_GROUP), 1, 2)
    k_sample = tm_to_bm(ks_l).reshape(depth, db, t_new, N_HEADS, 2, DK)
    v_sample = tm_to_bm(vs_l).reshape(depth, db, t_new, N_HEADS, DV)
    conv_sample = jnp.swapaxes(jnp.stack(cs_l), 1, 2)
    pool_sample = jnp.swapaxes(jnp.stack(ps_l), 1, 2)
    sgu_v_sample = tm_to_bm(gs_l)
    return (y_prompt, y_sample, k_prompt, v_prompt, jnp.stack(cp_l), jnp.stack(pp_l),
            k_sample, v_sample, conv_sample, pool_sample, sgu_v_sample)
```

```python
import functools
import math

import jax
import jax.numpy as jnp
from jax import lax
from jax.experimental import pallas as pl
from jax.experimental.pallas import tpu as pltpu

F32 = jnp.float32
BF16 = jnp.bfloat16

D_MODEL = 1024
D_GROUP = 256
D_IN = 8 * D_GROUP
CONV_W = 31
CONV_HALO = CONV_W - 1
SGU_CHUNK = 128
SGU_HEADS = 4
N_HEADS = 4
DK = 32
DV = 64
POOL_WINDOWS = (2, 4, 8, 16)
POOL_HALO = max(POOL_WINDOWS) - 1
POOL_GD = D_GROUP // len(POOL_WINDOWS)
D_FF = 2816
FF_CHUNK = 256
N_FF_CHUNKS = D_FF // FF_CHUNK
RMS_EPS = 1e-6
LN_EPS = 1e-5
Q_SCALE = DK ** -0.5
LOG2E = math.log2(math.e)

SUBLANES = 8
LANES = 128
VMEM_LIMIT_BYTES = 56 * 1024 * 1024

ROW_TILE = 512
SEQ_TILE = 256
CONV_ROWS = 64
ATT_Q_TILE = 512
ATT_K_TILE = 256
PAGES_PER_STEP = 32
DECODE_STREAMS = 2

NT_DIMS = (((1,), (1,)), ((), ()))


def _rms(x, g, eps):
    return x * lax.rsqrt(jnp.mean(x * x, axis=-1, keepdims=True) + eps) * g


def _layer_norm(x, g, b):
    mu = jnp.mean(x, axis=-1, keepdims=True)
    xc = x - mu
    var = jnp.mean(xc * xc, axis=-1, keepdims=True)
    return xc * lax.rsqrt(var + LN_EPS) * g + b


def _gelu(x):
    return 0.5 * x * (1.0 + lax.erf(x * (2.0 ** -0.5)))


def _sigmoid(x):
    return 1.0 / (1.0 + jnp.exp(-x))


def _lane_group_select(pieces, group_width, shape):
    lane = lax.broadcasted_iota(jnp.int32, shape, len(shape) - 1)
    out = pieces[-1]
    for g in range(len(pieces) - 2, -1, -1):
        out = jnp.where(lane < (g + 1) * group_width, pieces[g], out)
    return out


def _layer_spec(shape, layer):
    zeros = (0,) * len(shape)
    return pl.BlockSpec((1,) + tuple(shape), lambda *_: (layer,) + zeros)


def _layer_spec_single(shape, layer):
    zeros = (0,) * len(shape)
    return pl.BlockSpec((1,) + tuple(shape), lambda *_: (layer,) + zeros, pipeline_mode=pl.Buffered(1))


PROMPT_GROUPS = ((0, 2 * D_GROUP), (2 * D_GROUP, 2 * D_GROUP), (5 * D_GROUP, D_GROUP), (7 * D_GROUP, D_GROUP))
SAMPLE_GROUPS = PROMPT_GROUPS[:2] + ((4 * D_GROUP, D_GROUP), (5 * D_GROUP, D_GROUP), (6 * D_GROUP, D_GROUP),
                                     (7 * D_GROUP, D_GROUP))
Q_COL, K_COL, V_COL = 4 * D_GROUP, 5 * D_GROUP, 6 * D_GROUP


def _in_proj_kernel(x_ref, g_ref, w_ref, *rest, groups, feature_major, n_alias):
    out_refs = rest[n_alias:]
    h = _rms(x_ref[...], g_ref[0], RMS_EPS).astype(BF16)
    z = jnp.dot(h, w_ref[0], preferred_element_type=F32)
    for o_ref, (col, width) in zip(out_refs, groups):
        o_ref[...] = z[:, col:col + width]
    if feature_major:
        qt_ref, kt_ref, vt_ref = out_refs[len(groups):]
        qt_ref[0] = (z[:, Q_COL:Q_COL + D_GROUP] * (Q_SCALE * LOG2E)).T.astype(BF16)
        kt_ref[0, 0] = z[:, K_COL:K_COL + D_GROUP].T
        vt_ref[0, 0] = z[:, V_COL:V_COL + D_GROUP].T


def _in_proj(x2d, g, w_in, kv_prev, *, layer, batch, seq, groups, feature_major, tm):
    m = batch * seq
    ns = seq // tm
    depth = w_in.shape[0]
    row_map = lambda b, i: (b * ns + i, 0)
    in_specs = [pl.BlockSpec((tm, D_MODEL), row_map), _layer_spec((1, D_MODEL), layer),
                _layer_spec((D_MODEL, D_IN), layer)]
    args = [x2d, g, w_in]
    out_shape = [jax.ShapeDtypeStruct((m, w), F32) for _, w in groups]
    out_specs = [pl.BlockSpec((tm, w), row_map) for _, w in groups]
    aliases = {}
    n_alias = 0
    if feature_major:
        out_shape.append(jax.ShapeDtypeStruct((batch, D_GROUP, seq), BF16))
        out_specs.append(pl.BlockSpec((1, D_GROUP, tm), lambda b, i: (b, 0, i)))
        for _ in range(2):
            out_shape.append(jax.ShapeDtypeStruct((depth, batch, D_GROUP, seq), F32))
            out_specs.append(pl.BlockSpec((1, 1, D_GROUP, tm), lambda b, i: (layer, b, 0, i)))
        if kv_prev is not None:
            n_alias = len(kv_prev)
            for a, buf in enumerate(kv_prev):
                aliases[len(args)] = len(groups) + 1 + a
                in_specs.append(pl.BlockSpec(memory_space=pl.ANY))
                args.append(buf)
    return pl.pallas_call(
        functools.partial(_in_proj_kernel, groups=groups, feature_major=feature_major, n_alias=n_alias),
        grid=(batch, ns),
        in_specs=in_specs,
        out_specs=out_specs,
        out_shape=out_shape,
        input_output_aliases=aliases,
        compiler_params=pltpu.CompilerParams(
            dimension_semantics=("arbitrary", "arbitrary"), vmem_limit_bytes=VMEM_LIMIT_BYTES),
        name="in_proj",
    )(*args)


def _conv_post(acc, clg, clb):
    y = _layer_norm(acc, clg, clb)
    return y * _sigmoid(y)


def _pool_select(wins, shape):
    return _lane_group_select(wins, POOL_GD, shape)


HPAD = 32
XPAD = 24


def _mix_prompt_kernel(za_ref, zb_ref, xd_ref, cw_ref, cb_ref, clg_ref, clb_ref, sw_ref, sb_ref, slg_ref,
                       slb_ref, pw_ref, ps_ref, mix_ref, convn_ref, pooln_ref, hext, hshift, xext, s2, s4, s8, *, ts):
    i = pl.program_id(1)

    @pl.when(i == 0)
    def _():
        hext[0:HPAD, :] = jnp.zeros((HPAD, D_GROUP), F32)
        xext[0:XPAD, :] = jnp.zeros((XPAD, D_GROUP), F32)
        for buf in (s2, s4, s8):
            buf[0:SUBLANES, :] = jnp.zeros((SUBLANES, D_GROUP), F32)

    hext[HPAD:HPAD + ts, :] = za_ref[:, 0:D_GROUP] * _sigmoid(za_ref[:, D_GROUP:2 * D_GROUP])
    slab = ts + HPAD - SUBLANES
    for rho in range(1, SUBLANES):
        hshift[rho - 1] = hext[rho:rho + slab, :]
    for c in range(ts // CONV_ROWS):
        r0 = c * CONV_ROWS
        acc = jnp.broadcast_to(cb_ref[0], (CONV_ROWS, D_GROUP))
        for j in range(CONV_W):
            shift = HPAD - CONV_HALO + j
            q8, rho = shift // SUBLANES * SUBLANES, shift % SUBLANES
            if rho == 0:
                rows = hext[r0 + q8:r0 + q8 + CONV_ROWS, :]
            else:
                rows = hshift[rho - 1, r0 + q8:r0 + q8 + CONV_ROWS, :]
            acc = acc + cw_ref[0, j:j + 1, :] * rows
        mix_ref[r0:r0 + CONV_ROWS, 0:D_GROUP] = _conv_post(acc, clg_ref[0], clb_ref[0]).astype(BF16)

    u = _gelu(zb_ref[:, 0:D_GROUP])
    v = _layer_norm(_gelu(zb_ref[:, D_GROUP:2 * D_GROUP]), slg_ref[0], slb_ref[0])
    tri_r = lax.broadcasted_iota(jnp.int32, (SGU_CHUNK, SGU_CHUNK), 0)
    tri_c = lax.broadcasted_iota(jnp.int32, (SGU_CHUNK, SGU_CHUNK), 1)
    w_heads = [jnp.where(tri_c <= tri_r, sw_ref[0, g], 0.0).astype(BF16) for g in range(SGU_HEADS)]
    w_stack = jnp.concatenate(w_heads, axis=0)
    hd = D_GROUP // SGU_HEADS
    for c in range(ts // SGU_CHUNK):
        rows = slice(c * SGU_CHUNK, (c + 1) * SGU_CHUNK)
        r = jnp.dot(w_stack, v[rows].astype(BF16), preferred_element_type=F32)
        mixed = _lane_group_select([r[g * SGU_CHUNK:(g + 1) * SGU_CHUNK] for g in range(SGU_HEADS)],
                                   hd, (SGU_CHUNK, D_GROUP))
        mix_ref[rows, D_GROUP:2 * D_GROUP] = (u[rows] * (mixed + sb_ref[0])).astype(BF16)

    n = XPAD + ts
    xext[XPAD:n, :] = xd_ref[...]
    s2[SUBLANES:n, :] = xext[SUBLANES:n, :] + xext[SUBLANES - 1:n - 1, :]
    s4[SUBLANES:n, :] = s2[SUBLANES:n, :] + s2[SUBLANES - 2:n - 2, :]
    s8[2 * SUBLANES:n, :] = s4[2 * SUBLANES:n, :] + s4[2 * SUBLANES - 4:n - 4, :]
    win16 = s8[XPAD:n, :] + s8[XPAD - 8:n - 8, :]
    x0 = xext[XPAD:n, :]
    wins = _pool_select([s2[XPAD:n, :], s4[XPAD:n, :], s8[XPAD:n, :], win16], (ts, D_GROUP))
    wvec = _pool_select([jnp.full((1, D_GROUP), float(w), F32) for w in POOL_WINDOWS], (1, D_GROUP))
    pos = (i * ts + lax.broadcasted_iota(jnp.int32, (ts, D_GROUP), 0)).astype(F32)
    d = wins / jnp.minimum(pos + 1.0, wvec) - x0
    y = jnp.dot(d.astype(BF16), pw_ref[0], preferred_element_type=F32) * ps_ref[0]
    mix_ref[:, 2 * D_GROUP:3 * D_GROUP] = y.astype(BF16)

    hext[0:HPAD, :] = hext[ts:ts + HPAD, :]
    xext[SUBLANES:XPAD, :] = xext[ts + SUBLANES:ts + XPAD, :]

    @pl.when(i == pl.num_programs(1) - 1)
    def _():
        convn_ref[0] = hext[HPAD - CONV_HALO:HPAD, :]
        pooln_ref[0] = xext[XPAD - POOL_HALO:XPAD, :]


def _mix_prompt(za, zb, xd, wts, layer, *, batch, seq):
    ts = SEQ_TILE
    ns = seq // ts
    m = batch * seq
    row_map = lambda b, i: (b * ns + i, 0)
    vec = _layer_spec((1, D_GROUP), layer)
    ext = lambda pad: pltpu.VMEM((pad + ts, D_GROUP), F32)
    return pl.pallas_call(
        functools.partial(_mix_prompt_kernel, ts=ts),
        grid=(batch, ns),
        in_specs=[pl.BlockSpec((ts, 2 * D_GROUP), row_map),
                  pl.BlockSpec((ts, 2 * D_GROUP), row_map),
                  pl.BlockSpec((ts, D_GROUP), row_map),
                  _layer_spec((CONV_W, D_GROUP), layer), vec, vec, vec,
                  _layer_spec((SGU_HEADS, SGU_CHUNK, SGU_CHUNK), layer),
                  _layer_spec((SGU_CHUNK, D_GROUP), layer), vec, vec,
                  _layer_spec((D_GROUP, D_GROUP), layer), vec],
        out_specs=[pl.BlockSpec((ts, 3 * D_GROUP), row_map),
                   pl.BlockSpec((1, CONV_HALO, D_GROUP), lambda b, i: (b, 0, 0)),
                   pl.BlockSpec((1, POOL_HALO, D_GROUP), lambda b, i: (b, 0, 0))],
        out_shape=[jax.ShapeDtypeStruct((m, 3 * D_GROUP), BF16),
                   jax.ShapeDtypeStruct((batch, CONV_HALO, D_GROUP), F32),
                   jax.ShapeDtypeStruct((batch, POOL_HALO, D_GROUP), F32)],
        scratch_shapes=[ext(HPAD), pltpu.VMEM((SUBLANES - 1, ts + HPAD - SUBLANES, D_GROUP), F32),
                        ext(XPAD), ext(XPAD), ext(XPAD), ext(XPAD)],
        compiler_params=pltpu.CompilerParams(
            dimension_semantics=("arbitrary", "arbitrary"), vmem_limit_bytes=VMEM_LIMIT_BYTES),
        name="mix_prompt",
    )(za, zb, xd, wts["conv_w"], wts["conv_b"], wts["conv_ln_g"], wts["conv_ln_b"], wts["sgu_w"], wts["sgu_b_full"],
      wts["sgu_ln_g"], wts["sgu_ln_b"], wts["pool_w_bd"], wts["pool_scale"])


def _mix_sample_kernel(za_ref, zb_ref, xd_ref, sc_ref, sp_ref, cw_ref, cb_ref, clg_ref, clb_ref, sw4_ref, sb_ref,
                       slg_ref, slb_ref, pw_ref, ps_ref, mix_ref, convn_ref, pooln_ref, sguv_ref, *, db, t_new,
                       past_len):
    rows = lambda t: slice(t * db, (t + 1) * db)

    ext = [sc_ref[0, j] for j in range(CONV_HALO)]
    ext += [za_ref[rows(t), 0:D_GROUP] * _sigmoid(za_ref[rows(t), D_GROUP:2 * D_GROUP]) for t in range(t_new)]
    for t in range(t_new):
        acc = jnp.broadcast_to(cb_ref[0], (db, D_GROUP))
        for j in range(CONV_W):
            acc = acc + cw_ref[0, j:j + 1, :] * ext[t + j]
        mix_ref[rows(t), 0:D_GROUP] = _conv_post(acc, clg_ref[0], clb_ref[0]).astype(BF16)
    for j in range(CONV_HALO):
        convn_ref[j] = ext[t_new + j]

    u = _gelu(zb_ref[:, 0:D_GROUP])
    v = _layer_norm(_gelu(zb_ref[:, D_GROUP:2 * D_GROUP]), slg_ref[0], slb_ref[0])
    sguv_ref[...] = v
    for t in range(t_new):
        mixed = jnp.broadcast_to(sb_ref[0, t:t + 1, :], (db, D_GROUP))
        for s in range(t + 1):
            mixed = mixed + sw4_ref[0, t * t_new + s:t * t_new + s + 1, :] * v[rows(s)]
        mix_ref[rows(t), D_GROUP:2 * D_GROUP] = (u[rows(t)] * mixed).astype(BF16)

    pext = [sp_ref[0, j] for j in range(POOL_HALO)] + [xd_ref[rows(t), :] for t in range(t_new)]
    ds = []
    for t in range(t_new):
        x0 = pext[POOL_HALO + t]
        acc = x0
        wins = []
        for k in range(1, POOL_HALO + 1):
            acc = acc + pext[POOL_HALO + t - k]
            if k + 1 in POOL_WINDOWS:
                wins.append(acc)
        cnts = [jnp.full((1, D_GROUP), float(min(past_len + t + 1, w)), F32) for w in POOL_WINDOWS]
        cnt = _pool_select(cnts, (1, D_GROUP))
        ds.append(_pool_select(wins, (db, D_GROUP)) / cnt - x0)
    d = jnp.concatenate(ds, axis=0)
    y = jnp.dot(d.astype(BF16), pw_ref[0], preferred_element_type=F32) * ps_ref[0]
    mix_ref[:, 2 * D_GROUP:3 * D_GROUP] = y.astype(BF16)
    for j in range(POOL_HALO):
        pooln_ref[j] = pext[t_new + j]


def _mix_sample(za, zb, xd, sc_tm, sp_tm, wts, layer, *, db, t_new, past_len):
    m = db * t_new
    full = lambda shape: pl.BlockSpec(shape, lambda i: (0,) * len(shape))
    vec = _layer_spec((1, D_GROUP), layer)
    return pl.pallas_call(
        functools.partial(_mix_sample_kernel, db=db, t_new=t_new, past_len=past_len),
        grid=(1,),
        in_specs=[full((m, 2 * D_GROUP)), full((m, 2 * D_GROUP)), full((m, D_GROUP)),
                  _layer_spec((CONV_HALO, db, D_GROUP), layer), _layer_spec((POOL_HALO, db, D_GROUP), layer),
                  _layer_spec((CONV_W, D_GROUP), layer), vec, vec, vec,
                  _layer_spec((t_new * t_new, D_GROUP), layer),
                  _layer_spec((SGU_CHUNK, D_GROUP), layer), vec, vec,
                  _layer_spec((D_GROUP, D_GROUP), layer), vec],
        out_specs=[full((m, 3 * D_GROUP)), full((CONV_HALO, db, D_GROUP)), full((POOL_HALO, db, D_GROUP)),
                   full((m, D_GROUP))],
        out_shape=[jax.ShapeDtypeStruct((m, 3 * D_GROUP), BF16),
                   jax.ShapeDtypeStruct((CONV_HALO, db, D_GROUP), F32),
                   jax.ShapeDtypeStruct((POOL_HALO, db, D_GROUP), F32),
                   jax.ShapeDtypeStruct((m, D_GROUP), F32)],
        compiler_params=pltpu.CompilerParams(
            dimension_semantics=("arbitrary",), vmem_limit_bytes=VMEM_LIMIT_BYTES),
        name="mix_sample",
    )(za, zb, xd, sc_tm, sp_tm, wts["conv_w"], wts["conv_b"], wts["conv_ln_g"], wts["conv_ln_b"], wts["sgu_w4"],
      wts["sgu_b_full"], wts["sgu_ln_g"], wts["sgu_ln_b"], wts["pool_w_bd"], wts["pool_scale"])


def _diff_lambda(lq1, lk1, lq2, lk2, lam_init):
    s1 = jnp.sum(lq1 * lk1, axis=-1, keepdims=True)
    s2 = jnp.sum(lq2 * lk2, axis=-1, keepdims=True)
    return jnp.exp(s1) - jnp.exp(s2) + lam_init


def _alibi_slope(h):
    return 2.0 ** (-8.0 * (h + 1) / N_HEADS)


KEY_AUG = LANES
VAL_AUG = 80
N_BIAS = 3


def _attn_prompt_kernel(qt_ref, k_ref, vt_ref, lq1_ref, lk1_ref, lq2_ref, lk2_ref, subg_ref, o_ref, kaug, vaug, *,
                        tq, tk, nblk, lam_init):
    i = pl.program_id(1)
    hd = 2 * DK
    diag_blocks = tq // tk

    @pl.when(i == 0)
    def _():
        key = lax.broadcasted_iota(jnp.int32, (tk, KEY_AUG), 0).astype(F32)
        lane = lax.broadcasted_iota(jnp.int32, (tk, KEY_AUG), 1)
        ones_row = jnp.where(lax.broadcasted_iota(jnp.int32, (VAL_AUG - hd, tk), 0) == 0, 1.0, 0.0).astype(BF16)
        for h in range(N_HEADS):
            bias = (_alibi_slope(h) * LOG2E) * key
            hi = bias.astype(BF16).astype(F32)
            mid = (bias - hi).astype(BF16).astype(F32)
            lo = ((bias - hi) - mid).astype(BF16).astype(F32)
            aug = jnp.where(lane == hd, hi, jnp.where(lane == hd + 1, mid, jnp.where(lane == hd + 2, lo, 0.0)))
            for j in range(nblk):
                kblk = k_ref[j * tk:(j + 1) * tk, h * hd:(h + 1) * hd]
                kpad = jnp.concatenate([kblk, jnp.zeros((tk, KEY_AUG - hd), F32)], axis=1)
                kaug[j, h] = jnp.where(lane < hd, kpad, aug).astype(BF16)
                vaug[j, h, 0:hd, :] = vt_ref[0, 0, h * hd:(h + 1) * hd, j * tk:(j + 1) * tk].astype(BF16)
                vaug[j, h, hd:VAL_AUG, :] = ones_row

    lam = _diff_lambda(lq1_ref[0], lk1_ref[0], lq2_ref[0], lk2_ref[0], lam_init)
    qt = qt_ref[0]
    zero = jnp.zeros((DK, tq), BF16)
    ones_aug = jnp.where(lax.broadcasted_iota(jnp.int32, (KEY_AUG - hd, 2 * tq), 0) < N_BIAS, 1.0, 0.0).astype(BF16)
    qs = []
    for h in range(N_HEADS):
        q1 = qt[h * hd:h * hd + DK]
        q2 = qt[h * hd + DK:(h + 1) * hd]
        body = jnp.concatenate([jnp.concatenate([q1, zero], axis=0), jnp.concatenate([zero, q2], axis=0)], axis=1)
        qs.append(jnp.concatenate([body, ones_aug], axis=0))
    key = lax.broadcasted_iota(jnp.int32, (tk, 2 * tq), 0)
    qry = lax.broadcasted_iota(jnp.int32, (tk, 2 * tq), 1)
    rel = key - jnp.where(qry >= tq, qry - tq, qry)

    def block(j, carry, diag):
        scores = [jnp.dot(kaug[j, h], qs[h], preferred_element_type=F32) for h in range(N_HEADS)]
        probs = []
        for h in range(N_HEADS):
            s = scores[h] if diag is None else jnp.where(rel <= -diag * tk, scores[h], -jnp.inf)
            c = (_alibi_slope(h) * LOG2E) * (j * tk - i * tq).astype(F32)
            m = carry[h][0]
            m_new = jnp.maximum(m, jnp.max(s, axis=0, keepdims=True) + c)
            probs.append((m_new, jnp.exp2(m - m_new), jnp.exp2(s - (m_new - c)).astype(BF16)))
        out = []
        for h in range(N_HEADS):
            m_new, alpha, p = probs[h]
            out.append((m_new, alpha * carry[h][1] + jnp.dot(vaug[j, h], p, preferred_element_type=F32)))
        return tuple(out)

    init = tuple((jnp.full((1, 2 * tq), -jnp.inf, F32), jnp.zeros((VAL_AUG, 2 * tq), F32)) for _ in range(N_HEADS))
    carry = lax.fori_loop(0, i * diag_blocks, functools.partial(block, diag=None), init)
    for d in range(diag_blocks):
        carry = block(i * diag_blocks + d, carry, d)
    ys = []
    for h in range(N_HEADS):
        acc = carry[h][1]
        o = acc[0:hd, :] * (1.0 / acc[hd:hd + 1, :])
        od = o[:, 0:tq] - lam * o[:, tq:2 * tq]
        ms = jnp.mean(od * od, axis=0, keepdims=True)
        ys.append(od * lax.rsqrt(ms + LN_EPS) * subg_ref[0] * (1.0 - lam_init))
    o_ref[...] = jnp.concatenate(ys, axis=0).T.astype(BF16)


def _attn_prompt(qt, k_rows, vt_all, wts, layer, *, batch, seq, lam_init):
    tq, tk = ATT_Q_TILE, ATT_K_TILE
    ns = seq // tq
    nk = seq // tk
    lvec = _layer_spec((1, DK), layer)
    return pl.pallas_call(
        functools.partial(_attn_prompt_kernel, tq=tq, tk=tk, nblk=nk, lam_init=lam_init),
        grid=(batch, ns),
        in_specs=[pl.BlockSpec((1, D_GROUP, tq), lambda b, i: (b, 0, i)),
                  pl.BlockSpec((seq, D_GROUP), lambda b, i: (b, 0)),
                  pl.BlockSpec((1, 1, D_GROUP, seq), lambda b, i: (layer, b, 0, 0)),
                  lvec, lvec, lvec, lvec, _layer_spec((DV, 1), layer)],
        out_specs=pl.BlockSpec((tq, D_GROUP), lambda b, i: (b * ns + i, 0)),
        out_shape=jax.ShapeDtypeStruct((batch * seq, D_GROUP), BF16),
        scratch_shapes=[pltpu.VMEM((nk, N_HEADS, tk, KEY_AUG), BF16), pltpu.VMEM((nk, N_HEADS, VAL_AUG, tk), BF16)],
        compiler_params=pltpu.CompilerParams(
            dimension_semantics=("arbitrary", "arbitrary"), vmem_limit_bytes=VMEM_LIMIT_BYTES),
        name="attn_prompt",
    )(qt, k_rows, vt_all, wts["lambda_q1"], wts["lambda_k1"], wts["lambda_q2"], wts["lambda_k2"],
      wts["attn_sub_g_col"])


def _attn_sample_kernel(pt_ref, q_ref, kn_ref, vn_ref, lq1_ref, lk1_ref, lq2_ref, lk2_ref, subg_ref, kc_hbm, vc_hbm,
                        o_ref, kbuf, vbuf, sems, qbd_s, m_s, l_s, acc_s, *, layer, db, t_new, page, g_pages,
                        past_len, lam_init):
    b = pl.program_id(0)
    g = pl.program_id(1)
    n_g = pl.num_programs(1)
    step = b * n_g + g
    slot = lax.rem(step, 2)

    def page_copies(bb, gg, sl, lookup):
        copies = []
        for idx in range(g_pages):
            pid = pt_ref[bb, gg * g_pages + idx] if lookup else 0
            copies.append(pltpu.make_async_copy(kc_hbm.at[layer, pid], kbuf.at[sl, idx], sems.at[0, sl]))
            copies.append(pltpu.make_async_copy(vc_hbm.at[layer, pid], vbuf.at[sl, idx], sems.at[1, sl]))
        return copies

    @pl.when(step == 0)
    def _():
        for cp in page_copies(0, 0, 0, True):
            cp.start()

    @pl.when(step + 1 < pl.num_programs(0) * n_g)
    def _():
        nxt = step + 1
        for cp in page_copies(nxt // n_g, lax.rem(nxt, n_g), 1 - slot, True):
            cp.start()

    for cp in page_copies(b, g, slot, False):
        cp.wait()
    k_pages = [kbuf.at[slot, idx] for idx in range(g_pages)]
    v_pages = [vbuf.at[slot, idx] for idx in range(g_pages)]
    n_rows = N_HEADS * 2 * t_new
    row1 = lax.broadcasted_iota(jnp.int32, (n_rows, 1), 0)
    t_row = (row1 % t_new).astype(F32)
    slope_row = jnp.zeros((n_rows, 1), F32)
    for h in range(N_HEADS):
        slope_row = jnp.where(row1 // (2 * t_new) == h, _alibi_slope(h), slope_row)

    def gather_rows(ref):
        return jnp.concatenate([ref[pl.ds(t * db + b, 1), :] for t in range(t_new)], axis=0)

    @pl.when(g == 0)
    def _():
        q4 = gather_rows(q_ref) * Q_SCALE
        qt = jnp.concatenate([q4] * (2 * N_HEADS), axis=0)
        r = lax.broadcasted_iota(jnp.int32, (n_rows, D_GROUP), 0)
        c = lax.broadcasted_iota(jnp.int32, (n_rows, D_GROUP), 1)
        qbd_s[...] = jnp.where(c // DK == r // t_new, qt, 0.0).astype(BF16)
        m_s[...] = jnp.full(m_s.shape, -jnp.inf, F32)
        l_s[...] = jnp.zeros(l_s.shape, F32)
        acc_s[...] = jnp.zeros(acc_s.shape, F32)

    qbd = qbd_s[...]
    n_streams = m_s.shape[0]
    per_stream = g_pages // n_streams
    width = per_stream * page
    lane_pos = lax.broadcasted_iota(jnp.int32, (n_rows, width), 1)
    scores = [jnp.concatenate([jnp.dot(qbd, kp[...].astype(BF16), preferred_element_type=F32)
                               for kp in k_pages[st * per_stream:(st + 1) * per_stream]], axis=1)
              for st in range(n_streams)]
    probs = []
    for st in range(n_streams):
        kpos = ((g * g_pages + st * per_stream) * page + lane_pos).astype(F32)
        s = scores[st] - slope_row * ((past_len + t_row) - kpos)
        m_prev = m_s[st]
        m_new = jnp.maximum(m_prev, jnp.max(s, axis=-1, keepdims=True))
        alpha = jnp.exp(m_prev - m_new)
        p = jnp.exp(s - m_new)
        l_s[st] = alpha * l_s[st] + jnp.sum(p, axis=-1, keepdims=True)
        m_s[st] = m_new
        probs.append((alpha, p.astype(BF16)))
    for st in range(n_streams):
        alpha, pb = probs[st]
        pv = jnp.zeros((n_rows, D_GROUP), F32)
        for idx, vp in enumerate(v_pages[st * per_stream:(st + 1) * per_stream]):
            pv = pv + lax.dot_general(pb[:, idx * page:(idx + 1) * page], vp[...].astype(BF16), NT_DIMS,
                                      preferred_element_type=F32)
        acc_s[st] = alpha * acc_s[st] + pv

    @pl.when(g == pl.num_programs(1) - 1)
    def _():
        kn = gather_rows(kn_ref)
        vn = gather_rows(vn_ref)
        kn8 = jnp.concatenate([kn, jnp.zeros((SUBLANES - t_new, D_GROUP), F32)], axis=0).astype(BF16)
        sn = lax.dot_general(qbd, kn8, NT_DIMS, preferred_element_type=F32)
        tk = lax.broadcasted_iota(jnp.int32, (n_rows, SUBLANES), 1).astype(F32)
        dist = t_row - tk
        sn = jnp.where(dist >= 0.0, sn - slope_row * dist, -jnp.inf)
        m_fin = jnp.max(sn, axis=-1, keepdims=True)
        for st in range(n_streams):
            m_fin = jnp.maximum(m_fin, m_s[st])
        pn = jnp.exp(sn - m_fin)
        l_fin = jnp.sum(pn, axis=-1, keepdims=True)
        acc = jnp.zeros((n_rows, D_GROUP), F32)
        for st in range(n_streams):
            alpha = jnp.exp(m_s[st] - m_fin)
            l_fin = l_fin + alpha * l_s[st]
            acc = acc + alpha * acc_s[st]
        for t in range(t_new):
            acc = acc + pn[:, t:t + 1] * vn[t:t + 1, :]
        o = acc / l_fin
        r = lax.broadcasted_iota(jnp.int32, (n_rows, D_GROUP), 0)
        c = lax.broadcasted_iota(jnp.int32, (n_rows, D_GROUP), 1)
        o = jnp.where(c // DV == r // (2 * t_new), o, 0.0)
        y0 = jnp.zeros((t_new, D_GROUP), F32)
        y1 = jnp.zeros((t_new, D_GROUP), F32)
        for h in range(N_HEADS):
            y0 = y0 + o[h * 2 * t_new:h * 2 * t_new + t_new]
            y1 = y1 + o[h * 2 * t_new + t_new:(h + 1) * 2 * t_new]
        lam = _diff_lambda(lq1_ref[0], lk1_ref[0], lq2_ref[0], lk2_ref[0], lam_init)
        od = y0 - lam * y1
        seg_r = lax.broadcasted_iota(jnp.int32, (D_GROUP, D_GROUP), 0) // DV
        seg_c = lax.broadcasted_iota(jnp.int32, (D_GROUP, D_GROUP), 1) // DV
        seg = jnp.where(seg_r == seg_c, 1.0 / DV, 0.0).astype(F32)
        ms = jnp.dot(od * od, seg, preferred_element_type=F32, precision=lax.Precision.HIGHEST)
        y = od * lax.rsqrt(ms + LN_EPS) * subg_ref[0] * (1.0 - lam_init)
        for t in range(t_new):
            o_ref[pl.ds(t * db + b, 1), :] = y[t:t + 1, :]


def _attn_sample(page_table, q, kn, vn, cache_kt, cache_vt, wts, layer, *, db, t_new, lam_init):
    n_pages = page_table.shape[1]
    page = cache_kt.shape[-1]
    g_pages = min(PAGES_PER_STEP, n_pages)
    n_streams = DECODE_STREAMS if g_pages % DECODE_STREAMS == 0 else 1
    m = db * t_new
    n_rows = N_HEADS * 2 * t_new
    full = lambda b, g, pt: (0, 0)
    lvec = _layer_spec((1, DK), layer)

    in_specs = [pl.BlockSpec((m, D_GROUP), full)] * 3 + [lvec] * 4 + [_layer_spec((1, D_GROUP), layer)]
    in_specs += [pl.BlockSpec(memory_space=pl.ANY)] * 2
    page_buf = pltpu.VMEM((2, g_pages, D_GROUP, page), F32)
    grid_spec = pltpu.PrefetchScalarGridSpec(
        num_scalar_prefetch=1,
        grid=(db, n_pages // g_pages),
        in_specs=in_specs,
        out_specs=pl.BlockSpec((m, D_GROUP), full),
        scratch_shapes=[page_buf, page_buf, pltpu.SemaphoreType.DMA((2, 2)),
                        pltpu.VMEM((n_rows, D_GROUP), BF16), pltpu.VMEM((n_streams, n_rows, 1), F32),
                        pltpu.VMEM((n_streams, n_rows, 1), F32), pltpu.VMEM((n_streams, n_rows, D_GROUP), F32)],
    )
    return pl.pallas_call(
        functools.partial(_attn_sample_kernel, layer=layer, db=db, t_new=t_new, page=page, g_pages=g_pages,
                          past_len=n_pages * page, lam_init=lam_init),
        grid_spec=grid_spec,
        out_shape=jax.ShapeDtypeStruct((m, D_GROUP), F32),
        compiler_params=pltpu.CompilerParams(
            dimension_semantics=("arbitrary", "arbitrary"), vmem_limit_bytes=VMEM_LIMIT_BYTES),
        name="attn_sample",
    )(page_table, q, kn, vn, wts["lambda_q1"], wts["lambda_k1"], wts["lambda_q2"], wts["lambda_k2"],
      wts["attn_sub_g4"], cache_kt, cache_vt)


def _post_kernel(x_ref, mix_ref, yc_ref, wo_ref, gpost_ref, gfpre_ref, wgu_ref, wd_ref, gfpost_ref, o_ref,
                 hn_s, act_s):
    ab, cd = 2 * D_GROUP, 3 * D_GROUP
    mixo = jnp.dot(mix_ref[:, 0:ab], wo_ref[0, 0:ab, :], preferred_element_type=F32)
    mixo = mixo + jnp.dot(yc_ref[...].astype(BF16), wo_ref[0, ab:cd, :], preferred_element_type=F32)
    mixo = mixo + jnp.dot(mix_ref[:, ab:cd], wo_ref[0, cd:D_MODEL, :], preferred_element_type=F32)
    x1 = x_ref[...] + _rms(mixo, gpost_ref[0], RMS_EPS)
    hn_s[...] = _rms(x1, gfpre_ref[0], RMS_EPS).astype(BF16)
    for c in range(N_FF_CHUNKS):
        lo = c * FF_CHUNK
        gate = jnp.dot(hn_s[...], wgu_ref[0, :, lo:lo + FF_CHUNK], preferred_element_type=F32)
        up = jnp.dot(hn_s[...], wgu_ref[0, :, D_FF + lo:D_FF + lo + FF_CHUNK], preferred_element_type=F32)
        act_s[:, lo:lo + FF_CHUNK] = (gate * _sigmoid(gate) * up).astype(BF16)
    ffn = jnp.dot(act_s[...], wd_ref[0], preferred_element_type=F32)
    o_ref[...] = x1 + _rms(ffn, gfpost_ref[0], RMS_EPS)


def _post(x2d, mix3, yc, wts, layer, *, tm):
    m = x2d.shape[0]
    row = lambda i: (i, 0)
    vec = _layer_spec((1, D_MODEL), layer)
    return pl.pallas_call(
        _post_kernel,
        grid=(m // tm,),
        in_specs=[pl.BlockSpec((tm, D_MODEL), row),
                  pl.BlockSpec((tm, 3 * D_GROUP), row),
                  pl.BlockSpec((tm, D_GROUP), row),
                  _layer_spec_single((D_MODEL, D_MODEL), layer), vec, vec,
                  _layer_spec_single((D_MODEL, 2 * D_FF), layer),
                  _layer_spec_single((D_FF, D_MODEL), layer), vec],
        out_specs=pl.BlockSpec((tm, D_MODEL), row),
        out_shape=jax.ShapeDtypeStruct((m, D_MODEL), F32),
        scratch_shapes=[pltpu.VMEM((tm, D_MODEL), BF16), pltpu.VMEM((tm, D_FF), BF16)],
        compiler_params=pltpu.CompilerParams(
            dimension_semantics=("arbitrary",), vmem_limit_bytes=VMEM_LIMIT_BYTES),
        name="post",
    )(x2d, mix3, yc, wts["w_out"], wts["g_mix_post"], wts["g_ffn_pre"], wts["w_gu"], wts["w_down"],
      wts["g_ffn_post"])


def _prepare_weights(w_in, w_out, conv_w, conv_b, conv_ln_g, conv_ln_b, sgu_w, sgu_b, sgu_ln_g, sgu_ln_b,
                     lambda_q1, lambda_k1, lambda_q2, lambda_k2, attn_sub_g, pool_w, pool_scale, g_mix_pre,
                     g_mix_post, g_ffn_pre, g_ffn_post, w_gu, w_down, t_new):
    depth = w_in.shape[0]
    hd = D_GROUP // SGU_HEADS
    sgu_w4 = jnp.repeat(jnp.transpose(sgu_w[:, :, :t_new, :t_new], (0, 2, 3, 1)), hd, axis=3)
    eye = jnp.eye(len(POOL_WINDOWS), dtype=F32)
    row = lambda a: a[:, None, :]
    return {
        "w_in": w_in.astype(BF16), "w_out": w_out.astype(BF16), "w_gu": w_gu.astype(BF16),
        "w_down": w_down.astype(BF16),
        "conv_w": conv_w, "conv_b": row(conv_b), "conv_ln_g": row(conv_ln_g), "conv_ln_b": row(conv_ln_b),
        "sgu_w": sgu_w,
        "sgu_b_full": jnp.repeat(jnp.swapaxes(sgu_b, 1, 2), hd, axis=2),
        "sgu_w4": sgu_w4.reshape(depth, t_new * t_new, D_GROUP),
        "sgu_ln_g": row(sgu_ln_g), "sgu_ln_b": row(sgu_ln_b),
        "lambda_q1": row(lambda_q1), "lambda_k1": row(lambda_k1), "lambda_q2": row(lambda_q2),
        "lambda_k2": row(lambda_k2),
        "attn_sub_g_col": attn_sub_g[:, :, None], "attn_sub_g4": jnp.tile(row(attn_sub_g), (1, 1, N_HEADS)),
        "pool_w_bd": jnp.einsum("lgce,gh->lgche", pool_w, eye).reshape(depth, D_GROUP, D_GROUP).astype(BF16),
        "pool_scale": row(pool_scale),
        "g_mix_pre": row(g_mix_pre), "g_mix_post": row(g_mix_post), "g_ffn_pre": row(g_ffn_pre),
        "g_ffn_post": row(g_ffn_post),
    }


def kernel(x_prompt, x_sample, cache_k, cache_v, state_conv, state_pool, page_table, w_in, w_out, conv_w, conv_b, conv_ln_g, conv_ln_b, sgu_w, sgu_b, sgu_ln_g, sgu_ln_b, lambda_q1, lambda_k1, lambda_q2, lambda_k2, attn_sub_g, pool_w, pool_scale, g_mix_pre, g_mix_post, g_ffn_pre, g_ffn_post, w_gu, w_down):
    batch, seq, _ = x_prompt.shape
    db, t_new, _ = x_sample.shape
    depth, n_pool, page = cache_k.shape[0], cache_k.shape[1], cache_k.shape[2]
    past_len = page_table.shape[1] * page
    assert seq % ROW_TILE == 0 and seq % SEQ_TILE == 0 and db % SUBLANES == 0
    assert seq % ATT_Q_TILE == 0 and ATT_Q_TILE % ATT_K_TILE == 0
    assert page == LANES and page_table.shape[1] % min(PAGES_PER_STEP, page_table.shape[1]) == 0

    wts = _prepare_weights(w_in, w_out, conv_w, conv_b, conv_ln_g, conv_ln_b, sgu_w, sgu_b, sgu_ln_g, sgu_ln_b,
                           lambda_q1, lambda_k1, lambda_q2, lambda_k2, attn_sub_g, pool_w, pool_scale,
                           g_mix_pre, g_mix_post, g_ffn_pre, g_ffn_post, w_gu, w_down, t_new)

    cache_kt = jnp.transpose(cache_k, (0, 1, 3, 4, 5, 2)).reshape(depth, n_pool, D_GROUP, page)
    cache_vt = jnp.transpose(cache_v, (0, 1, 3, 4, 2)).reshape(depth, n_pool, D_GROUP, page)
    ms = db * t_new
    xs = jnp.swapaxes(x_sample, 0, 1).reshape(ms, D_MODEL)
    sc_tm = jnp.swapaxes(state_conv, 1, 2)
    sp_tm = jnp.swapaxes(state_pool, 1, 2)
    xp = x_prompt.reshape(batch * seq, D_MODEL)

    kv_all = None
    cp_l, pp_l = [], []
    ks_l, vs_l, cs_l, ps_l, gs_l = [], [], [], [], []
    for l in range(depth):
        lam_init = 0.8 - 0.6 * math.exp(-0.3 * l)
        za, zb, k_rows, xd, qt, kt_all, vt_all = _in_proj(
            xp, wts["g_mix_pre"], wts["w_in"], kv_all, layer=l, batch=batch, seq=seq, groups=PROMPT_GROUPS,
            feature_major=True, tm=ROW_TILE)
        kv_all = (kt_all, vt_all)
        mix3, conv_new, pool_new = _mix_prompt(za, zb, xd, wts, l, batch=batch, seq=seq)
        yc = _attn_prompt(qt, k_rows, vt_all, wts, l, batch=batch, seq=seq, lam_init=lam_init)
        xp = _post(xp, mix3, yc, wts, l, tm=ROW_TILE)
        cp_l.append(conv_new); pp_l.append(pool_new)
        za, zb, q, kn, vn, xd = _in_proj(xs, wts["g_mix_pre"], wts["w_in"], None, layer=l, batch=1, seq=ms,
                                         groups=SAMPLE_GROUPS, feature_major=False, tm=ms)
        mix3, conv_new, pool_new, sgu_v = _mix_sample(za, zb, xd, sc_tm, sp_tm, wts, l, db=db, t_new=t_new,
                                                      past_len=past_len)
        yc = _attn_sample(page_table, q, kn, vn, cache_kt, cache_vt, wts, l, db=db, t_new=t_new, lam_init=lam_init)
        xs = _post(xs, mix3, yc, wts, l, tm=ms)
        ks_l.append(kn); vs_l.append(vn); cs_l.append(conv_new); ps_l.append(pool_new); gs_l.append(sgu_v)

    kt_all, vt_all = kv_all
    y_prompt = xp.reshape(batch, seq, D_MODEL)
    y_sample = jnp.swapaxes(xs.reshape(t_new, db, D_MODEL), 0, 1)
    k_prompt = jnp.transpose(kt_all.reshape(depth, batch, N_HEADS, 2, DK, seq), (0, 1, 5, 2, 3, 4))
    v_prompt = jnp.transpose(vt_all.reshape(depth, batch, N_HEADS, DV, seq), (0, 1, 4, 2, 3))
    tm_to_bm = lambda a: jnp.swapaxes(jnp.stack(a).reshape(depth, t_new, db, D_GROUP), 1, 2)
    k_sample = tm_to_bm(ks_l).reshape(depth, db, t_new, N_HEADS, 2, DK)
    v_sample = tm_to_bm(vs_l).reshape(depth, db, t_new, N_HEADS, DV)
    conv_sample = jnp.swapaxes(jnp.stack(cs_l), 1, 2)
    pool_sample = jnp.swapaxes(jnp.stack(ps_l), 1, 2)
    sgu_v_sample = tm_to_bm(gs_l)
    return (y_prompt, y_sample, k_prompt, v_prompt, jnp.stack(cp_l), jnp.stack(pp_l),
            k_sample, v_sample, conv_sample, pool_sample, sgu_v_sample)
```

```python
import functools
import math

import jax
import jax.numpy as jnp
from jax import lax
from jax.experimental import pallas as pl
from jax.experimental.pallas import tpu as pltpu

F32 = jnp.float32
BF16 = jnp.bfloat16

D_MODEL = 1024
D_GROUP = 256
D_IN = 8 * D_GROUP
CONV_W = 31
CONV_HALO = CONV_W - 1
SGU_CHUNK = 128
SGU_HEADS = 4
N_HEADS = 4
DK = 32
DV = 64
POOL_WINDOWS = (2, 4, 8, 16)
POOL_HALO = max(POOL_WINDOWS) - 1
POOL_GD = D_GROUP // len(POOL_WINDOWS)
D_FF = 2816
FF_CHUNK = 256
N_FF_CHUNKS = D_FF // FF_CHUNK
RMS_EPS = 1e-6
LN_EPS = 1e-5
Q_SCALE = DK ** -0.5
LOG2E = math.log2(math.e)

SUBLANES = 8
LANES = 128
VMEM_LIMIT_BYTES = 56 * 1024 * 1024

ROW_TILE = 512
SEQ_TILE = 512
CONV_ROWS = 64
ATT_Q_TILE = 512
ATT_K_TILE = 256
PAGES_PER_STEP = 32
DECODE_STREAMS = 2

NT_DIMS = (((1,), (1,)), ((), ()))


def _rms(x, g, eps):
    return x * lax.rsqrt(jnp.mean(x * x, axis=-1, keepdims=True) + eps) * g


def _layer_norm(x, g, b):
    mu = jnp.mean(x, axis=-1, keepdims=True)
    xc = x - mu
    var = jnp.mean(xc * xc, axis=-1, keepdims=True)
    return xc * lax.rsqrt(var + LN_EPS) * g + b


def _gelu(x):
    return 0.5 * x * (1.0 + lax.erf(x * (2.0 ** -0.5)))


def _sigmoid(x):
    return 1.0 / (1.0 + jnp.exp(-x))


def _lane_group_select(pieces, group_width, shape):
    lane = lax.broadcasted_iota(jnp.int32, shape, len(shape) - 1)
    out = pieces[-1]
    for g in range(len(pieces) - 2, -1, -1):
        out = jnp.where(lane < (g + 1) * group_width, pieces[g], out)
    return out


def _layer_spec(shape, layer):
    zeros = (0,) * len(shape)
    return pl.BlockSpec((1,) + tuple(shape), lambda *_: (layer,) + zeros)


def _layer_spec_single(shape, layer):
    zeros = (0,) * len(shape)
    return pl.BlockSpec((1,) + tuple(shape), lambda *_: (layer,) + zeros, pipeline_mode=pl.Buffered(1))


AG_COL, UV_COL, Q_COL, K_COL, V_COL, XD_COL = (0, 2 * D_GROUP, 4 * D_GROUP, 5 * D_GROUP, 6 * D_GROUP, 7 * D_GROUP)
SAMPLE_GROUPS = ((AG_COL, 2 * D_GROUP), (UV_COL, 2 * D_GROUP), (Q_COL, D_GROUP), (K_COL, D_GROUP),
                 (V_COL, D_GROUP), (XD_COL, D_GROUP))


def _in_proj_kernel(x_ref, g_ref, w_ref, *out_refs, groups):
    h = _rms(x_ref[...], g_ref[0], RMS_EPS).astype(BF16)
    z = jnp.dot(h, w_ref[0], preferred_element_type=F32)
    for o_ref, (col, width) in zip(out_refs, groups):
        o_ref[...] = z[:, col:col + width]


def _in_proj(x2d, g, w_in, *, layer, groups, tm):
    m = x2d.shape[0]
    row_map = lambda i: (i, 0)
    return pl.pallas_call(
        functools.partial(_in_proj_kernel, groups=groups),
        grid=(m // tm,),
        in_specs=[pl.BlockSpec((tm, D_MODEL), row_map), _layer_spec((1, D_MODEL), layer),
                  _layer_spec((D_MODEL, D_IN), layer)],
        out_specs=[pl.BlockSpec((tm, w), row_map) for _, w in groups],
        out_shape=[jax.ShapeDtypeStruct((m, w), F32) for _, w in groups],
        compiler_params=pltpu.CompilerParams(
            dimension_semantics=("arbitrary",), vmem_limit_bytes=VMEM_LIMIT_BYTES),
        name="in_proj",
    )(x2d, g, w_in)


def _conv_post(acc, clg, clb):
    y = _layer_norm(acc, clg, clb)
    return y * _sigmoid(y)


def _pool_select(wins, shape):
    return _lane_group_select(wins, POOL_GD, shape)


HPAD = 32
XPAD = 24


def _proj_mix_kernel(x_ref, g_ref, w_ref, cw_ref, cb_ref, clg_ref, clb_ref, sw_ref, sb_ref, slg_ref, slb_ref,
                     pw_ref, ps_ref, *rest, ts, n_alias):
    (mix_ref, k_ref, qt_ref, kt_ref, vt_ref, convn_ref, pooln_ref, hext, hshift, xext, s2, s4, s8) = rest[n_alias:]
    i = pl.program_id(1)

    @pl.when(i == 0)
    def _():
        hext[0:HPAD, :] = jnp.zeros((HPAD, D_GROUP), F32)
        xext[0:XPAD, :] = jnp.zeros((XPAD, D_GROUP), F32)
        for buf in (s2, s4, s8):
            buf[0:SUBLANES, :] = jnp.zeros((SUBLANES, D_GROUP), F32)

    hn = _rms(x_ref[...], g_ref[0], RMS_EPS).astype(BF16)
    proj = lambda col, width: jnp.dot(hn, w_ref[0, :, col:col + width], preferred_element_type=F32)
    z_ag = proj(AG_COL, 2 * D_GROUP)
    z_uv = proj(UV_COL, 2 * D_GROUP)
    z_qkv = proj(Q_COL, 3 * D_GROUP)
    z_xd = proj(XD_COL, D_GROUP)

    k_rows = z_qkv[:, D_GROUP:2 * D_GROUP]
    k_ref[...] = k_rows
    qt_ref[0] = (z_qkv[:, 0:D_GROUP] * (Q_SCALE * LOG2E)).T.astype(BF16)
    kt_ref[0, 0] = k_rows.T
    vt_ref[0, 0] = z_qkv[:, 2 * D_GROUP:3 * D_GROUP].T

    hext[HPAD:HPAD + ts, :] = z_ag[:, 0:D_GROUP] * _sigmoid(z_ag[:, D_GROUP:2 * D_GROUP])
    slab = ts + HPAD - SUBLANES
    for rho in range(1, SUBLANES):
        hshift[rho - 1] = hext[rho:rho + slab, :]
    for c in range(ts // CONV_ROWS):
        r0 = c * CONV_ROWS
        acc = jnp.broadcast_to(cb_ref[0], (CONV_ROWS, D_GROUP))
        for j in range(CONV_W):
            shift = HPAD - CONV_HALO + j
            q8, rho = shift // SUBLANES * SUBLANES, shift % SUBLANES
            if rho == 0:
                rows = hext[r0 + q8:r0 + q8 + CONV_ROWS, :]
            else:
                rows = hshift[rho - 1, r0 + q8:r0 + q8 + CONV_ROWS, :]
            acc = acc + cw_ref[0, j:j + 1, :] * rows
        mix_ref[r0:r0 + CONV_ROWS, 0:D_GROUP] = _conv_post(acc, clg_ref[0], clb_ref[0]).astype(BF16)

    u = _gelu(z_uv[:, 0:D_GROUP])
    v = _layer_norm(_gelu(z_uv[:, D_GROUP:2 * D_GROUP]), slg_ref[0], slb_ref[0])
    tri_r = lax.broadcasted_iota(jnp.int32, (SGU_CHUNK, SGU_CHUNK), 0)
    tri_c = lax.broadcasted_iota(jnp.int32, (SGU_CHUNK, SGU_CHUNK), 1)
    w_heads = [jnp.where(tri_c <= tri_r, sw_ref[0, g], 0.0).astype(BF16) for g in range(SGU_HEADS)]
    w_stack = jnp.concatenate(w_heads, axis=0)
    hd = D_GROUP // SGU_HEADS
    for c in range(ts // SGU_CHUNK):
        rows = slice(c * SGU_CHUNK, (c + 1) * SGU_CHUNK)
        r = jnp.dot(w_stack, v[rows].astype(BF16), preferred_element_type=F32)
        mixed = _lane_group_select([r[g * SGU_CHUNK:(g + 1) * SGU_CHUNK] for g in range(SGU_HEADS)],
                                   hd, (SGU_CHUNK, D_GROUP))
        mix_ref[rows, D_GROUP:2 * D_GROUP] = (u[rows] * (mixed + sb_ref[0])).astype(BF16)

    n = XPAD + ts
    xext[XPAD:n, :] = z_xd
    s2[SUBLANES:n, :] = xext[SUBLANES:n, :] + xext[SUBLANES - 1:n - 1, :]
    s4[SUBLANES:n, :] = s2[SUBLANES:n, :] + s2[SUBLANES - 2:n - 2, :]
    s8[2 * SUBLANES:n, :] = s4[2 * SUBLANES:n, :] + s4[2 * SUBLANES - 4:n - 4, :]
    win16 = s8[XPAD:n, :] + s8[XPAD - 8:n - 8, :]
    x0 = xext[XPAD:n, :]
    wins = _pool_select([s2[XPAD:n, :], s4[XPAD:n, :], s8[XPAD:n, :], win16], (ts, D_GROUP))
    wvec = _pool_select([jnp.full((1, D_GROUP), float(w), F32) for w in POOL_WINDOWS], (1, D_GROUP))
    pos = (i * ts + lax.broadcasted_iota(jnp.int32, (ts, D_GROUP), 0)).astype(F32)
    d = wins / jnp.minimum(pos + 1.0, wvec) - x0
    y = jnp.dot(d.astype(BF16), pw_ref[0], preferred_element_type=F32) * ps_ref[0]
    mix_ref[:, 2 * D_GROUP:3 * D_GROUP] = y.astype(BF16)

    hext[0:HPAD, :] = hext[ts:ts + HPAD, :]
    xext[SUBLANES:XPAD, :] = xext[ts + SUBLANES:ts + XPAD, :]

    @pl.when(i == pl.num_programs(1) - 1)
    def _():
        convn_ref[0] = hext[HPAD - CONV_HALO:HPAD, :]
        pooln_ref[0] = xext[XPAD - POOL_HALO:XPAD, :]


def _proj_mix_prompt(x2d, wts, kv_prev, layer, *, batch, seq):
    ts = SEQ_TILE
    ns = seq // ts
    m = batch * seq
    depth = wts["w_in"].shape[0]
    row_map = lambda b, i: (b * ns + i, 0)
    vec = _layer_spec((1, D_GROUP), layer)
    ext = lambda pad: pltpu.VMEM((pad + ts, D_GROUP), F32)
    in_specs = [pl.BlockSpec((ts, D_MODEL), row_map), _layer_spec((1, D_MODEL), layer),
                _layer_spec_single((D_MODEL, D_IN), layer),
                _layer_spec((CONV_W, D_GROUP), layer), vec, vec, vec,
                _layer_spec((SGU_HEADS, SGU_CHUNK, SGU_CHUNK), layer),
                _layer_spec((SGU_CHUNK, D_GROUP), layer), vec, vec,
                _layer_spec((D_GROUP, D_GROUP), layer), vec]
    args = [x2d, wts["g_mix_pre"], wts["w_in"], wts["conv_w"], wts["conv_b"], wts["conv_ln_g"], wts["conv_ln_b"],
            wts["sgu_w"], wts["sgu_b_full"], wts["sgu_ln_g"], wts["sgu_ln_b"], wts["pool_w_bd"], wts["pool_scale"]]
    kv_spec = pl.BlockSpec((1, 1, D_GROUP, ts), lambda b, i: (layer, b, 0, i))
    kv_shape = jax.ShapeDtypeStruct((depth, batch, D_GROUP, seq), F32)
    out_specs = [pl.BlockSpec((ts, 3 * D_GROUP), row_map), pl.BlockSpec((ts, D_GROUP), row_map),
                 pl.BlockSpec((1, D_GROUP, ts), lambda b, i: (b, 0, i)), kv_spec, kv_spec,
                 pl.BlockSpec((1, CONV_HALO, D_GROUP), lambda b, i: (b, 0, 0)),
                 pl.BlockSpec((1, POOL_HALO, D_GROUP), lambda b, i: (b, 0, 0))]
    out_shape = [jax.ShapeDtypeStruct((m, 3 * D_GROUP), BF16), jax.ShapeDtypeStruct((m, D_GROUP), F32),
                 jax.ShapeDtypeStruct((batch, D_GROUP, seq), BF16), kv_shape, kv_shape,
                 jax.ShapeDtypeStruct((batch, CONV_HALO, D_GROUP), F32),
                 jax.ShapeDtypeStruct((batch, POOL_HALO, D_GROUP), F32)]
    kv_out_index = 3
    aliases = {}
    n_alias = 0
    if kv_prev is not None:
        n_alias = len(kv_prev)
        for a, buf in enumerate(kv_prev):
            aliases[len(args)] = kv_out_index + a
            in_specs.append(pl.BlockSpec(memory_space=pl.ANY))
            args.append(buf)
    return pl.pallas_call(
        functools.partial(_proj_mix_kernel, ts=ts, n_alias=n_alias),
        grid=(batch, ns),
        in_specs=in_specs,
        out_specs=out_specs,
        out_shape=out_shape,
        input_output_aliases=aliases,
        scratch_shapes=[ext(HPAD), pltpu.VMEM((SUBLANES - 1, ts + HPAD - SUBLANES, D_GROUP), F32),
                        ext(XPAD), ext(XPAD), ext(XPAD), ext(XPAD)],
        compiler_params=pltpu.CompilerParams(
            dimension_semantics=("arbitrary", "arbitrary"), vmem_limit_bytes=VMEM_LIMIT_BYTES),
        name="proj_mix_prompt",
    )(*args)


def _mix_sample_kernel(za_ref, zb_ref, xd_ref, sc_ref, sp_ref, cw_ref, cb_ref, clg_ref, clb_ref, sw4_ref, sb_ref,
                       slg_ref, slb_ref, pw_ref, ps_ref, mix_ref, convn_ref, pooln_ref, sguv_ref, *, db, t_new,
                       past_len):
    rows = lambda t: slice(t * db, (t + 1) * db)

    ext = [sc_ref[0, j] for j in range(CONV_HALO)]
    ext += [za_ref[rows(t), 0:D_GROUP] * _sigmoid(za_ref[rows(t), D_GROUP:2 * D_GROUP]) for t in range(t_new)]
    for t in range(t_new):
        acc = jnp.broadcast_to(cb_ref[0], (db, D_GROUP))
        for j in range(CONV_W):
            acc = acc + cw_ref[0, j:j + 1, :] * ext[t + j]
        mix_ref[rows(t), 0:D_GROUP] = _conv_post(acc, clg_ref[0], clb_ref[0]).astype(BF16)
    for j in range(CONV_HALO):
        convn_ref[j] = ext[t_new + j]

    u = _gelu(zb_ref[:, 0:D_GROUP])
    v = _layer_norm(_gelu(zb_ref[:, D_GROUP:2 * D_GROUP]), slg_ref[0], slb_ref[0])
    sguv_ref[...] = v
    for t in range(t_new):
        mixed = jnp.broadcast_to(sb_ref[0, t:t + 1, :], (db, D_GROUP))
        for s in range(t + 1):
            mixed = mixed + sw4_ref[0, t * t_new + s:t * t_new + s + 1, :] * v[rows(s)]
        mix_ref[rows(t), D_GROUP:2 * D_GROUP] = (u[rows(t)] * mixed).astype(BF16)

    pext = [sp_ref[0, j] for j in range(POOL_HALO)] + [xd_ref[rows(t), :] for t in range(t_new)]
    ds = []
    for t in range(t_new):
        x0 = pext[POOL_HALO + t]
        acc = x0
        wins = []
        for k in range(1, POOL_HALO + 1):
            acc = acc + pext[POOL_HALO + t - k]
            if k + 1 in POOL_WINDOWS:
                wins.append(acc)
        cnts = [jnp.full((1, D_GROUP), float(min(past_len + t + 1, w)), F32) for w in POOL_WINDOWS]
        cnt = _pool_select(cnts, (1, D_GROUP))
        ds.append(_pool_select(wins, (db, D_GROUP)) / cnt - x0)
    d = jnp.concatenate(ds, axis=0)
    y = jnp.dot(d.astype(BF16), pw_ref[0], preferred_element_type=F32) * ps_ref[0]
    mix_ref[:, 2 * D_GROUP:3 * D_GROUP] = y.astype(BF16)
    for j in range(POOL_HALO):
        pooln_ref[j] = pext[t_new + j]


def _mix_sample(za, zb, xd, sc_tm, sp_tm, wts, layer, *, db, t_new, past_len):
    m = db * t_new
    full = lambda shape: pl.BlockSpec(shape, lambda i: (0,) * len(shape))
    vec = _layer_spec((1, D_GROUP), layer)
    return pl.pallas_call(
        functools.partial(_mix_sample_kernel, db=db, t_new=t_new, past_len=past_len),
        grid=(1,),
        in_specs=[full((m, 2 * D_GROUP)), full((m, 2 * D_GROUP)), full((m, D_GROUP)),
                  _layer_spec((CONV_HALO, db, D_GROUP), layer), _layer_spec((POOL_HALO, db, D_GROUP), layer),
                  _layer_spec((CONV_W, D_GROUP), layer), vec, vec, vec,
                  _layer_spec((t_new * t_new, D_GROUP), layer),
                  _layer_spec((SGU_CHUNK, D_GROUP), layer), vec, vec,
                  _layer_spec((D_GROUP, D_GROUP), layer), vec],
        out_specs=[full((m, 3 * D_GROUP)), full((CONV_HALO, db, D_GROUP)), full((POOL_HALO, db, D_GROUP)),
                   full((m, D_GROUP))],
        out_shape=[jax.ShapeDtypeStruct((m, 3 * D_GROUP), BF16),
                   jax.ShapeDtypeStruct((CONV_HALO, db, D_GROUP), F32),
                   jax.ShapeDtypeStruct((POOL_HALO, db, D_GROUP), F32),
                   jax.ShapeDtypeStruct((m, D_GROUP), F32)],
        compiler_params=pltpu.CompilerParams(
            dimension_semantics=("arbitrary",), vmem_limit_bytes=VMEM_LIMIT_BYTES),
        name="mix_sample",
    )(za, zb, xd, sc_tm, sp_tm, wts["conv_w"], wts["conv_b"], wts["conv_ln_g"], wts["conv_ln_b"], wts["sgu_w4"],
      wts["sgu_b_full"], wts["sgu_ln_g"], wts["sgu_ln_b"], wts["pool_w_bd"], wts["pool_scale"])


def _diff_lambda(lq1, lk1, lq2, lk2, lam_init):
    s1 = jnp.sum(lq1 * lk1, axis=-1, keepdims=True)
    s2 = jnp.sum(lq2 * lk2, axis=-1, keepdims=True)
    return jnp.exp(s1) - jnp.exp(s2) + lam_init


def _alibi_slope(h):
    return 2.0 ** (-8.0 * (h + 1) / N_HEADS)


KEY_AUG = LANES
VAL_AUG = 80
N_BIAS = 3


def _attn_prompt_kernel(qt_ref, k_ref, vt_ref, lq1_ref, lk1_ref, lq2_ref, lk2_ref, subg_ref, o_ref, kaug, vaug, *,
                        tq, tk, nblk, lam_init):
    i = pl.program_id(1)
    hd = 2 * DK
    diag_blocks = tq // tk

    @pl.when(i == 0)
    def _():
        key = lax.broadcasted_iota(jnp.int32, (tk, KEY_AUG), 0).astype(F32)
        lane = lax.broadcasted_iota(jnp.int32, (tk, KEY_AUG), 1)
        ones_row = jnp.where(lax.broadcasted_iota(jnp.int32, (VAL_AUG - hd, tk), 0) == 0, 1.0, 0.0).astype(BF16)
        for h in range(N_HEADS):
            bias = (_alibi_slope(h) * LOG2E) * key
            hi = bias.astype(BF16).astype(F32)
            mid = (bias - hi).astype(BF16).astype(F32)
            lo = ((bias - hi) - mid).astype(BF16).astype(F32)
            aug = jnp.where(lane == hd, hi, jnp.where(lane == hd + 1, mid, jnp.where(lane == hd + 2, lo, 0.0)))
            for j in range(nblk):
                kblk = k_ref[j * tk:(j + 1) * tk, h * hd:(h + 1) * hd]
                kpad = jnp.concatenate([kblk, jnp.zeros((tk, KEY_AUG - hd), F32)], axis=1)
                kaug[j, h] = jnp.where(lane < hd, kpad, aug).astype(BF16)
                vaug[j, h, 0:hd, :] = vt_ref[0, 0, h * hd:(h + 1) * hd, j * tk:(j + 1) * tk].astype(BF16)
                vaug[j, h, hd:VAL_AUG, :] = ones_row

    lam = _diff_lambda(lq1_ref[0], lk1_ref[0], lq2_ref[0], lk2_ref[0], lam_init)
    qt = qt_ref[0]
    zero = jnp.zeros((DK, tq), BF16)
    ones_aug = jnp.where(lax.broadcasted_iota(jnp.int32, (KEY_AUG - hd, 2 * tq), 0) < N_BIAS, 1.0, 0.0).astype(BF16)
    qs = []
    for h in range(N_HEADS):
        q1 = qt[h * hd:h * hd + DK]
        q2 = qt[h * hd + DK:(h + 1) * hd]
        body = jnp.concatenate([jnp.concatenate([q1, zero], axis=0), jnp.concatenate([zero, q2], axis=0)], axis=1)
        qs.append(jnp.concatenate([body, ones_aug], axis=0))
    key = lax.broadcasted_iota(jnp.int32, (tk, 2 * tq), 0)
    qry = lax.broadcasted_iota(jnp.int32, (tk, 2 * tq), 1)
    rel = key - jnp.where(qry >= tq, qry - tq, qry)

    def block(j, carry, diag):
        scores = [jnp.dot(kaug[j, h], qs[h], preferred_element_type=F32) for h in range(N_HEADS)]
        probs = []
        for h in range(N_HEADS):
            s = scores[h] if diag is None else jnp.where(rel <= -diag * tk, scores[h], -jnp.inf)
            c = (_alibi_slope(h) * LOG2E) * (j * tk - i * tq).astype(F32)
            m = carry[h][0]
            m_new = jnp.maximum(m, jnp.max(s, axis=0, keepdims=True) + c)
            probs.append((m_new, jnp.exp2(m - m_new), jnp.exp2(s - (m_new - c)).astype(BF16)))
        out = []
        for h in range(N_HEADS):
            m_new, alpha, p = probs[h]
            out.append((m_new, alpha * carry[h][1] + jnp.dot(vaug[j, h], p, preferred_element_type=F32)))
        return tuple(out)

    init = tuple((jnp.full((1, 2 * tq), -jnp.inf, F32), jnp.zeros((VAL_AUG, 2 * tq), F32)) for _ in range(N_HEADS))
    carry = lax.fori_loop(0, i * diag_blocks, functools.partial(block, diag=None), init)
    for d in range(diag_blocks):
        carry = block(i * diag_blocks + d, carry, d)
    ys = []
    for h in range(N_HEADS):
        acc = carry[h][1]
        o = acc[0:hd, :] * (1.0 / acc[hd:hd + 1, :])
        od = o[:, 0:tq] - lam * o[:, tq:2 * tq]
        ms = jnp.mean(od * od, axis=0, keepdims=True)
        ys.append(od * lax.rsqrt(ms + LN_EPS) * subg_ref[0] * (1.0 - lam_init))
    o_ref[...] = jnp.concatenate(ys, axis=0).T.astype(BF16)


def _attn_prompt(qt, k_rows, vt_all, wts, layer, *, batch, seq, lam_init):
    tq, tk = ATT_Q_TILE, ATT_K_TILE
    ns = seq // tq
    nk = seq // tk
    lvec = _layer_spec((1, DK), layer)
    return pl.pallas_call(
        functools.partial(_attn_prompt_kernel, tq=tq, tk=tk, nblk=nk, lam_init=lam_init),
        grid=(batch, ns),
        in_specs=[pl.BlockSpec((1, D_GROUP, tq), lambda b, i: (b, 0, i)),
                  pl.BlockSpec((seq, D_GROUP), lambda b, i: (b, 0)),
                  pl.BlockSpec((1, 1, D_GROUP, seq), lambda b, i: (layer, b, 0, 0)),
                  lvec, lvec, lvec, lvec, _layer_spec((DV, 1), layer)],
        out_specs=pl.BlockSpec((tq, D_GROUP), lambda b, i: (b * ns + i, 0)),
        out_shape=jax.ShapeDtypeStruct((batch * seq, D_GROUP), BF16),
        scratch_shapes=[pltpu.VMEM((nk, N_HEADS, tk, KEY_AUG), BF16), pltpu.VMEM((nk, N_HEADS, VAL_AUG, tk), BF16)],
        compiler_params=pltpu.CompilerParams(
            dimension_semantics=("arbitrary", "arbitrary"), vmem_limit_bytes=VMEM_LIMIT_BYTES),
        name="attn_prompt",
    )(qt, k_rows, vt_all, wts["lambda_q1"], wts["lambda_k1"], wts["lambda_q2"], wts["lambda_k2"],
      wts["attn_sub_g_col"])


def _attn_sample_kernel(pt_ref, q_ref, kn_ref, vn_ref, lq1_ref, lk1_ref, lq2_ref, lk2_ref, subg_ref, kc_hbm, vc_hbm,
                        o_ref, kbuf, vbuf, sems, qbd_s, m_s, l_s, acc_s, *, layer, db, t_new, page, g_pages,
                        past_len, lam_init):
    b = pl.program_id(0)
    g = pl.program_id(1)
    n_g = pl.num_programs(1)
    step = b * n_g + g
    slot = lax.rem(step, 2)

    def page_copies(bb, gg, sl, lookup):
        copies = []
        for idx in range(g_pages):
            pid = pt_ref[bb, gg * g_pages + idx] if lookup else 0
            copies.append(pltpu.make_async_copy(kc_hbm.at[layer, pid], kbuf.at[sl, idx], sems.at[0, sl]))
            copies.append(pltpu.make_async_copy(vc_hbm.at[layer, pid], vbuf.at[sl, idx], sems.at[1, sl]))
        return copies

    @pl.when(step == 0)
    def _():
        for cp in page_copies(0, 0, 0, True):
            cp.start()

    @pl.when(step + 1 < pl.num_programs(0) * n_g)
    def _():
        nxt = step + 1
        for cp in page_copies(nxt // n_g, lax.rem(nxt, n_g), 1 - slot, True):
            cp.start()

    for cp in page_copies(b, g, slot, False):
        cp.wait()
    k_pages = [kbuf.at[slot, idx] for idx in range(g_pages)]
    v_pages = [vbuf.at[slot, idx] for idx in range(g_pages)]
    n_rows = N_HEADS * 2 * t_new
    row1 = lax.broadcasted_iota(jnp.int32, (n_rows, 1), 0)
    t_row = (row1 % t_new).astype(F32)
    slope_row = jnp.zeros((n_rows, 1), F32)
    for h in range(N_HEADS):
        slope_row = jnp.where(row1 // (2 * t_new) == h, _alibi_slope(h), slope_row)

    def gather_rows(ref):
        return jnp.concatenate([ref[pl.ds(t * db + b, 1), :] for t in range(t_new)], axis=0)

    @pl.when(g == 0)
    def _():
        q4 = gather_rows(q_ref) * Q_SCALE
        qt = jnp.concatenate([q4] * (2 * N_HEADS), axis=0)
        r = lax.broadcasted_iota(jnp.int32, (n_rows, D_GROUP), 0)
        c = lax.broadcasted_iota(jnp.int32, (n_rows, D_GROUP), 1)
        qbd_s[...] = jnp.where(c // DK == r // t_new, qt, 0.0).astype(BF16)
        m_s[...] = jnp.full(m_s.shape, -jnp.inf, F32)
        l_s[...] = jnp.zeros(l_s.shape, F32)
        acc_s[...] = jnp.zeros(acc_s.shape, F32)

    qbd = qbd_s[...]
    n_streams = m_s.shape[0]
    per_stream = g_pages // n_streams
    width = per_stream * page
    lane_pos = lax.broadcasted_iota(jnp.int32, (n_rows, width), 1)
    scores = [jnp.concatenate([jnp.dot(qbd, kp[...].astype(BF16), preferred_element_type=F32)
                               for kp in k_pages[st * per_stream:(st + 1) * per_stream]], axis=1)
              for st in range(n_streams)]
    probs = []
    for st in range(n_streams):
        kpos = ((g * g_pages + st * per_stream) * page + lane_pos).astype(F32)
        s = scores[st] - slope_row * ((past_len + t_row) - kpos)
        m_prev = m_s[st]
        m_new = jnp.maximum(m_prev, jnp.max(s, axis=-1, keepdims=True))
        alpha = jnp.exp(m_prev - m_new)
        p = jnp.exp(s - m_new)
        l_s[st] = alpha * l_s[st] + jnp.sum(p, axis=-1, keepdims=True)
        m_s[st] = m_new
        probs.append((alpha, p.astype(BF16)))
    for st in range(n_streams):
        alpha, pb = probs[st]
        pv = jnp.zeros((n_rows, D_GROUP), F32)
        for idx, vp in enumerate(v_pages[st * per_stream:(st + 1) * per_stream]):
            pv = pv + lax.dot_general(pb[:, idx * page:(idx + 1) * page], vp[...].astype(BF16), NT_DIMS,
                                      preferred_element_type=F32)
        acc_s[st] = alpha * acc_s[st] + pv

    @pl.when(g == pl.num_programs(1) - 1)
    def _():
        kn = gather_rows(kn_ref)
        vn = gather_rows(vn_ref)
        kn8 = jnp.concatenate([kn, jnp.zeros((SUBLANES - t_new, D_GROUP), F32)], axis=0).astype(BF16)
        sn = lax.dot_general(qbd, kn8, NT_DIMS, preferred_element_type=F32)
        tk = lax.broadcasted_iota(jnp.int32, (n_rows, SUBLANES), 1).astype(F32)
        dist = t_row - tk
        sn = jnp.where(dist >= 0.0, sn - slope_row * dist, -jnp.inf)
        m_fin = jnp.max(sn, axis=-1, keepdims=True)
        for st in range(n_streams):
            m_fin = jnp.maximum(m_fin, m_s[st])
        pn = jnp.exp(sn - m_fin)
        l_fin = jnp.sum(pn, axis=-1, keepdims=True)
        acc = jnp.zeros((n_rows, D_GROUP), F32)
        for st in range(n_streams):
            alpha = jnp.exp(m_s[st] - m_fin)
            l_fin = l_fin + alpha * l_s[st]
            acc = acc + alpha * acc_s[st]
        for t in range(t_new):
            acc = acc + pn[:, t:t + 1] * vn[t:t + 1, :]
        o = acc / l_fin
        r = lax.broadcasted_iota(jnp.int32, (n_rows, D_GROUP), 0)
        c = lax.broadcasted_iota(jnp.int32, (n_rows, D_GROUP), 1)
        o = jnp.where(c // DV == r // (2 * t_new), o, 0.0)
        y0 = jnp.zeros((t_new, D_GROUP), F32)
        y1 = jnp.zeros((t_new, D_GROUP), F32)
        for h in range(N_HEADS):
            y0 = y0 + o[h * 2 * t_new:h * 2 * t_new + t_new]
            y1 = y1 + o[h * 2 * t_new + t_new:(h + 1) * 2 * t_new]
        lam = _diff_lambda(lq1_ref[0], lk1_ref[0], lq2_ref[0], lk2_ref[0], lam_init)
        od = y0 - lam * y1
        seg_r = lax.broadcasted_iota(jnp.int32, (D_GROUP, D_GROUP), 0) // DV
        seg_c = lax.broadcasted_iota(jnp.int32, (D_GROUP, D_GROUP), 1) // DV
        seg = jnp.where(seg_r == seg_c, 1.0 / DV, 0.0).astype(F32)
        ms = jnp.dot(od * od, seg, preferred_element_type=F32, precision=lax.Precision.HIGHEST)
        y = od * lax.rsqrt(ms + LN_EPS) * subg_ref[0] * (1.0 - lam_init)
        for t in range(t_new):
            o_ref[pl.ds(t * db + b, 1), :] = y[t:t + 1, :]


def _attn_sample(page_table, q, kn, vn, cache_kt, cache_vt, wts, layer, *, db, t_new, lam_init):
    n_pages = page_table.shape[1]
    page = cache_kt.shape[-1]
    g_pages = min(PAGES_PER_STEP, n_pages)
    n_streams = DECODE_STREAMS if g_pages % DECODE_STREAMS == 0 else 1
    m = db * t_new
    n_rows = N_HEADS * 2 * t_new
    full = lambda b, g, pt: (0, 0)
    lvec = _layer_spec((1, DK), layer)

    in_specs = [pl.BlockSpec((m, D_GROUP), full)] * 3 + [lvec] * 4 + [_layer_spec((1, D_GROUP), layer)]
    in_specs += [pl.BlockSpec(memory_space=pl.ANY)] * 2
    page_buf = pltpu.VMEM((2, g_pages, D_GROUP, page), F32)
    grid_spec = pltpu.PrefetchScalarGridSpec(
        num_scalar_prefetch=1,
        grid=(db, n_pages // g_pages),
        in_specs=in_specs,
        out_specs=pl.BlockSpec((m, D_GROUP), full),
        scratch_shapes=[page_buf, page_buf, pltpu.SemaphoreType.DMA((2, 2)),
                        pltpu.VMEM((n_rows, D_GROUP), BF16), pltpu.VMEM((n_streams, n_rows, 1), F32),
                        pltpu.VMEM((n_streams, n_rows, 1), F32), pltpu.VMEM((n_streams, n_rows, D_GROUP), F32)],
    )
    return pl.pallas_call(
        functools.partial(_attn_sample_kernel, layer=layer, db=db, t_new=t_new, page=page, g_pages=g_pages,
                          past_len=n_pages * page, lam_init=lam_init),
        grid_spec=grid_spec,
        out_shape=jax.ShapeDtypeStruct((m, D_GROUP), F32),
        compiler_params=pltpu.CompilerParams(
            dimension_semantics=("arbitrary", "arbitrary"), vmem_limit_bytes=VMEM_LIMIT_BYTES),
        name="attn_sample",
    )(page_table, q, kn, vn, wts["lambda_q1"], wts["lambda_k1"], wts["lambda_q2"], wts["lambda_k2"],
      wts["attn_sub_g4"], cache_kt, cache_vt)


def _post_kernel(x_ref, mix_ref, yc_ref, wo_ref, gpost_ref, gfpre_ref, wgu_ref, wd_ref, gfpost_ref, o_ref,
                 hn_s, act_s):
    ab, cd = 2 * D_GROUP, 3 * D_GROUP
    mixo = jnp.dot(mix_ref[:, 0:ab], wo_ref[0, 0:ab, :], preferred_element_type=F32)
    mixo = mixo + jnp.dot(yc_ref[...].astype(BF16), wo_ref[0, ab:cd, :], preferred_element_type=F32)
    mixo = mixo + jnp.dot(mix_ref[:, ab:cd], wo_ref[0, cd:D_MODEL, :], preferred_element_type=F32)
    x1 = x_ref[...] + _rms(mixo, gpost_ref[0], RMS_EPS)
    hn_s[...] = _rms(x1, gfpre_ref[0], RMS_EPS).astype(BF16)
    for c in range(N_FF_CHUNKS):
        lo = c * FF_CHUNK
        gate = jnp.dot(hn_s[...], wgu_ref[0, :, lo:lo + FF_CHUNK], preferred_element_type=F32)
        up = jnp.dot(hn_s[...], wgu_ref[0, :, D_FF + lo:D_FF + lo + FF_CHUNK], preferred_element_type=F32)
        act_s[:, lo:lo + FF_CHUNK] = (gate * _sigmoid(gate) * up).astype(BF16)
    ffn = jnp.dot(act_s[...], wd_ref[0], preferred_element_type=F32)
    o_ref[...] = x1 + _rms(ffn, gfpost_ref[0], RMS_EPS)


def _post(x2d, mix3, yc, wts, layer, *, tm):
    m = x2d.shape[0]
    row = lambda i: (i, 0)
    vec = _layer_spec((1, D_MODEL), layer)
    return pl.pallas_call(
        _post_kernel,
        grid=(m // tm,),
        in_specs=[pl.BlockSpec((tm, D_MODEL), row),
                  pl.BlockSpec((tm, 3 * D_GROUP), row),
                  pl.BlockSpec((tm, D_GROUP), row),
                  _layer_spec_single((D_MODEL, D_MODEL), layer), vec, vec,
                  _layer_spec_single((D_MODEL, 2 * D_FF), layer),
                  _layer_spec_single((D_FF, D_MODEL), layer), vec],
        out_specs=pl.BlockSpec((tm, D_MODEL), row),
        out_shape=jax.ShapeDtypeStruct((m, D_MODEL), F32),
        scratch_shapes=[pltpu.VMEM((tm, D_MODEL), BF16), pltpu.VMEM((tm, D_FF), BF16)],
        compiler_params=pltpu.CompilerParams(
            dimension_semantics=("arbitrary",), vmem_limit_bytes=VMEM_LIMIT_BYTES),
        name="post",
    )(x2d, mix3, yc, wts["w_out"], wts["g_mix_post"], wts["g_ffn_pre"], wts["w_gu"], wts["w_down"],
      wts["g_ffn_post"])


def _prepare_weights(w_in, w_out, conv_w, conv_b, conv_ln_g, conv_ln_b, sgu_w, sgu_b, sgu_ln_g, sgu_ln_b,
                     lambda_q1, lambda_k1, lambda_q2, lambda_k2, attn_sub_g, pool_w, pool_scale, g_mix_pre,
                     g_mix_post, g_ffn_pre, g_ffn_post, w_gu, w_down, t_new):
    depth = w_in.shape[0]
    hd = D_GROUP // SGU_HEADS
    sgu_w4 = jnp.repeat(jnp.transpose(sgu_w[:, :, :t_new, :t_new], (0, 2, 3, 1)), hd, axis=3)
    eye = jnp.eye(len(POOL_WINDOWS), dtype=F32)
    row = lambda a: a[:, None, :]
    return {
        "w_in": w_in.astype(BF16), "w_out": w_out.astype(BF16), "w_gu": w_gu.astype(BF16),
        "w_down": w_down.astype(BF16),
        "conv_w": conv_w, "conv_b": row(conv_b), "conv_ln_g": row(conv_ln_g), "conv_ln_b": row(conv_ln_b),
        "sgu_w": sgu_w,
        "sgu_b_full": jnp.repeat(jnp.swapaxes(sgu_b, 1, 2), hd, axis=2),
        "sgu_w4": sgu_w4.reshape(depth, t_new * t_new, D_GROUP),
        "sgu_ln_g": row(sgu_ln_g), "sgu_ln_b": row(sgu_ln_b),
        "lambda_q1": row(lambda_q1), "lambda_k1": row(lambda_k1), "lambda_q2": row(lambda_q2),
        "lambda_k2": row(lambda_k2),
        "attn_sub_g_col": attn_sub_g[:, :, None], "attn_sub_g4": jnp.tile(row(attn_sub_g), (1, 1, N_HEADS)),
        "pool_w_bd": jnp.einsum("lgce,gh->lgche", pool_w, eye).reshape(depth, D_GROUP, D_GROUP).astype(BF16),
        "pool_scale": row(pool_scale),
        "g_mix_pre": row(g_mix_pre), "g_mix_post": row(g_mix_post), "g_ffn_pre": row(g_ffn_pre),
        "g_ffn_post": row(g_ffn_post),
    }


def kernel(x_prompt, x_sample, cache_k, cache_v, state_conv, state_pool, page_table, w_in, w_out, conv_w, conv_b, conv_ln_g, conv_ln_b, sgu_w, sgu_b, sgu_ln_g, sgu_ln_b, lambda_q1, lambda_k1, lambda_q2, lambda_k2, attn_sub_g, pool_w, pool_scale, g_mix_pre, g_mix_post, g_ffn_pre, g_ffn_post, w_gu, w_down):
    batch, seq, _ = x_prompt.shape
    db, t_new, _ = x_sample.shape
    depth, n_pool, page = cache_k.shape[0], cache_k.shape[1], cache_k.shape[2]
    past_len = page_table.shape[1] * page
    assert seq % ROW_TILE == 0 and seq % SEQ_TILE == 0 and db % SUBLANES == 0
    assert seq % ATT_Q_TILE == 0 and ATT_Q_TILE % ATT_K_TILE == 0
    assert page == LANES and page_table.shape[1] % min(PAGES_PER_STEP, page_table.shape[1]) == 0

    wts = _prepare_weights(w_in, w_out, conv_w, conv_b, conv_ln_g, conv_ln_b, sgu_w, sgu_b, sgu_ln_g, sgu_ln_b,
                           lambda_q1, lambda_k1, lambda_q2, lambda_k2, attn_sub_g, pool_w, pool_scale,
                           g_mix_pre, g_mix_post, g_ffn_pre, g_ffn_post, w_gu, w_down, t_new)

    cache_kt = jnp.transpose(cache_k, (0, 1, 3, 4, 5, 2)).reshape(depth, n_pool, D_GROUP, page)
    cache_vt = jnp.transpose(cache_v, (0, 1, 3, 4, 2)).reshape(depth, n_pool, D_GROUP, page)
    ms = db * t_new
    xs = jnp.swapaxes(x_sample, 0, 1).reshape(ms, D_MODEL)
    sc_tm = jnp.swapaxes(state_conv, 1, 2)
    sp_tm = jnp.swapaxes(state_pool, 1, 2)
    xp = x_prompt.reshape(batch * seq, D_MODEL)

    kv_all = None
    cp_l, pp_l = [], []
    ks_l, vs_l, cs_l, ps_l, gs_l = [], [], [], [], []
    for l in range(depth):
        lam_init = 0.8 - 0.6 * math.exp(-0.3 * l)
        mix3, k_rows, qt, kt_all, vt_all, conv_new, pool_new = _proj_mix_prompt(xp, wts, kv_all, l, batch=batch,
                                                                                seq=seq)
        kv_all = (kt_all, vt_all)
        yc = _attn_prompt(qt, k_rows, vt_all, wts, l, batch=batch, seq=seq, lam_init=lam_init)
        xp = _post(xp, mix3, yc, wts, l, tm=ROW_TILE)
        cp_l.append(conv_new); pp_l.append(pool_new)
        za, zb, q, kn, vn, xd = _in_proj(xs, wts["g_mix_pre"], wts["w_in"], layer=l, groups=SAMPLE_GROUPS, tm=ms)
        mix3, conv_new, pool_new, sgu_v = _mix_sample(za, zb, xd, sc_tm, sp_tm, wts, l, db=db, t_new=t_new,
                                                      past_len=past_len)
        yc = _attn_sample(page_table, q, kn, vn, cache_kt, cache_vt, wts, l, db=db, t_new=t_new, lam_init=lam_init)
        xs = _post(xs, mix3, yc, wts, l, tm=ms)
        ks_l.append(kn); vs_l.append(vn); cs_l.append(conv_new); ps_l.append(pool_new); gs_l.append(sgu_v)

    kt_all, vt_all = kv_all
    y_prompt = xp.reshape(batch, seq, D_MODEL)
    y_sample = jnp.swapaxes(xs.reshape(t_new, db, D_MODEL), 0, 1)
    k_prompt = jnp.transpose(kt_all.reshape(depth, batch, N_HEADS, 2, DK, seq), (0, 1, 5, 2, 3, 4))
    v_prompt = jnp.transpose(vt_all.reshape(depth, batch, N_HEADS, DV, seq), (0, 1, 4, 2, 3))
    tm_to_bm = lambda a: jnp.swapaxes(jnp.stack(a).reshape(depth, t_new, db, D_GROUP), 1, 2)
    k_sample = tm_to_bm(ks_l).reshape(depth, db, t_new, N_HEADS, 2, DK)
    v_sample = tm_to_bm(vs_l).reshape(depth, db, t_new, N_HEADS, DV)
    conv_sample = jnp.swapaxes(jnp.stack(cs_l), 1, 2)
    pool_sample = jnp.swapaxes(jnp.stack(ps_l), 1, 2)
    sgu_v_sample = tm_to_bm(gs_l)
    return (y_prompt, y_sample, k_prompt, v_prompt, jnp.stack(cp_l), jnp.stack(pp_l),
            k_sample, v_sample, conv_sample, pool_sample, sgu_v_sample)
```

```python
import functools
import math

import jax
import jax.numpy as jnp
from jax import lax
from jax.experimental import pallas as pl
from jax.experimental.pallas import tpu as pltpu

F32 = jnp.float32
BF16 = jnp.bfloat16

D_MODEL = 1024
D_GROUP = 256
D_IN = 8 * D_GROUP
CONV_W = 31
CONV_HALO = CONV_W - 1
SGU_CHUNK = 128
SGU_HEADS = 4
N_HEADS = 4
DK = 32
DV = 64
POOL_WINDOWS = (2, 4, 8, 16)
POOL_HALO = max(POOL_WINDOWS) - 1
POOL_GD = D_GROUP // len(POOL_WINDOWS)
D_FF = 2816
FF_CHUNK = 256
N_FF_CHUNKS = D_FF // FF_CHUNK
RMS_EPS = 1e-6
LN_EPS = 1e-5
Q_SCALE = DK ** -0.5
LOG2E = math.log2(math.e)

SUBLANES = 8
LANES = 128
VMEM_LIMIT_BYTES = 56 * 1024 * 1024

ROW_TILE = 512
SEQ_TILE = 512
CONV_ROWS = 64
ATT_Q_TILE = 512
ATT_K_TILE = 256
DECODE_CHUNK_PAGES = 32

NT_DIMS = (((1,), (1,)), ((), ()))


def _rms(x, g, eps):
    return x * lax.rsqrt(jnp.mean(x * x, axis=-1, keepdims=True) + eps) * g


def _layer_norm(x, g, b):
    mu = jnp.mean(x, axis=-1, keepdims=True)
    xc = x - mu
    var = jnp.mean(xc * xc, axis=-1, keepdims=True)
    return xc * lax.rsqrt(var + LN_EPS) * g + b


def _gelu(x):
    return 0.5 * x * (1.0 + lax.erf(x * (2.0 ** -0.5)))


def _sigmoid(x):
    return 1.0 / (1.0 + jnp.exp(-x))


def _lane_group_select(pieces, group_width, shape):
    lane = lax.broadcasted_iota(jnp.int32, shape, len(shape) - 1)
    out = pieces[-1]
    for g in range(len(pieces) - 2, -1, -1):
        out = jnp.where(lane < (g + 1) * group_width, pieces[g], out)
    return out


def _layer_spec(shape, layer):
    zeros = (0,) * len(shape)
    return pl.BlockSpec((1,) + tuple(shape), lambda *_: (layer,) + zeros)


def _layer_spec_single(shape, layer):
    zeros = (0,) * len(shape)
    return pl.BlockSpec((1,) + tuple(shape), lambda *_: (layer,) + zeros, pipeline_mode=pl.Buffered(1))


AG_COL, UV_COL, Q_COL, K_COL, V_COL, XD_COL = (0, 2 * D_GROUP, 4 * D_GROUP, 5 * D_GROUP, 6 * D_GROUP, 7 * D_GROUP)
SAMPLE_GROUPS = ((AG_COL, 2 * D_GROUP), (UV_COL, 2 * D_GROUP), (Q_COL, D_GROUP), (K_COL, D_GROUP),
                 (V_COL, D_GROUP), (XD_COL, D_GROUP))


def _in_proj_kernel(x_ref, g_ref, w_ref, *out_refs, groups):
    h = _rms(x_ref[...], g_ref[0], RMS_EPS).astype(BF16)
    z = jnp.dot(h, w_ref[0], preferred_element_type=F32)
    for o_ref, (col, width) in zip(out_refs, groups):
        o_ref[...] = z[:, col:col + width]


def _in_proj(x2d, g, w_in, *, layer, groups, tm):
    m = x2d.shape[0]
    row_map = lambda i: (i, 0)
    return pl.pallas_call(
        functools.partial(_in_proj_kernel, groups=groups),
        grid=(m // tm,),
        in_specs=[pl.BlockSpec((tm, D_MODEL), row_map), _layer_spec((1, D_MODEL), layer),
                  _layer_spec((D_MODEL, D_IN), layer)],
        out_specs=[pl.BlockSpec((tm, w), row_map) for _, w in groups],
        out_shape=[jax.ShapeDtypeStruct((m, w), F32) for _, w in groups],
        compiler_params=pltpu.CompilerParams(
            dimension_semantics=("arbitrary",), vmem_limit_bytes=VMEM_LIMIT_BYTES),
        name="in_proj",
    )(x2d, g, w_in)


def _conv_post(acc, clg, clb):
    y = _layer_norm(acc, clg, clb)
    return y * _sigmoid(y)


def _pool_select(wins, shape):
    return _lane_group_select(wins, POOL_GD, shape)


HPAD = 32
XPAD = 24


def _proj_mix_kernel(x_ref, g_ref, w_ref, cw_ref, cb_ref, clg_ref, clb_ref, sw_ref, sb_ref, slg_ref, slb_ref,
                     pw_ref, ps_ref, *rest, ts, n_alias):
    (mix_ref, k_ref, qt_ref, kt_ref, vt_ref, convn_ref, pooln_ref, hext, hshift, xext, s2, s4, s8) = rest[n_alias:]
    i = pl.program_id(1)

    @pl.when(i == 0)
    def _():
        hext[0:HPAD, :] = jnp.zeros((HPAD, D_GROUP), F32)
        xext[0:XPAD, :] = jnp.zeros((XPAD, D_GROUP), F32)
        for buf in (s2, s4, s8):
            buf[0:SUBLANES, :] = jnp.zeros((SUBLANES, D_GROUP), F32)

    hn = _rms(x_ref[...], g_ref[0], RMS_EPS).astype(BF16)
    proj = lambda col, width: jnp.dot(hn, w_ref[0, :, col:col + width], preferred_element_type=F32)
    z_ag = proj(AG_COL, 2 * D_GROUP)
    z_uv = proj(UV_COL, 2 * D_GROUP)
    z_qkv = proj(Q_COL, 3 * D_GROUP)
    z_xd = proj(XD_COL, D_GROUP)

    k_rows = z_qkv[:, D_GROUP:2 * D_GROUP]
    k_ref[...] = k_rows
    qt_ref[0] = (z_qkv[:, 0:D_GROUP] * (Q_SCALE * LOG2E)).T.astype(BF16)
    kt_ref[0, 0] = k_rows.T
    vt_ref[0, 0] = z_qkv[:, 2 * D_GROUP:3 * D_GROUP].T

    hext[HPAD:HPAD + ts, :] = z_ag[:, 0:D_GROUP] * _sigmoid(z_ag[:, D_GROUP:2 * D_GROUP])
    slab = ts + HPAD - SUBLANES
    for rho in range(1, SUBLANES):
        hshift[rho - 1] = hext[rho:rho + slab, :]
    for c in range(ts // CONV_ROWS):
        r0 = c * CONV_ROWS
        acc = jnp.broadcast_to(cb_ref[0], (CONV_ROWS, D_GROUP))
        for j in range(CONV_W):
            shift = HPAD - CONV_HALO + j
            q8, rho = shift // SUBLANES * SUBLANES, shift % SUBLANES
            if rho == 0:
                rows = hext[r0 + q8:r0 + q8 + CONV_ROWS, :]
            else:
                rows = hshift[rho - 1, r0 + q8:r0 + q8 + CONV_ROWS, :]
            acc = acc + cw_ref[0, j:j + 1, :] * rows
        mix_ref[r0:r0 + CONV_ROWS, 0:D_GROUP] = _conv_post(acc, clg_ref[0], clb_ref[0]).astype(BF16)

    u = _gelu(z_uv[:, 0:D_GROUP])
    v = _layer_norm(_gelu(z_uv[:, D_GROUP:2 * D_GROUP]), slg_ref[0], slb_ref[0])
    tri_r = lax.broadcasted_iota(jnp.int32, (SGU_CHUNK, SGU_CHUNK), 0)
    tri_c = lax.broadcasted_iota(jnp.int32, (SGU_CHUNK, SGU_CHUNK), 1)
    w_heads = [jnp.where(tri_c <= tri_r, sw_ref[0, g], 0.0).astype(BF16) for g in range(SGU_HEADS)]
    w_stack = jnp.concatenate(w_heads, axis=0)
    hd = D_GROUP // SGU_HEADS
    for c in range(ts // SGU_CHUNK):
        rows = slice(c * SGU_CHUNK, (c + 1) * SGU_CHUNK)
        r = jnp.dot(w_stack, v[rows].astype(BF16), preferred_element_type=F32)
        mixed = _lane_group_select([r[g * SGU_CHUNK:(g + 1) * SGU_CHUNK] for g in range(SGU_HEADS)],
                                   hd, (SGU_CHUNK, D_GROUP))
        mix_ref[rows, D_GROUP:2 * D_GROUP] = (u[rows] * (mixed + sb_ref[0])).astype(BF16)

    n = XPAD + ts
    xext[XPAD:n, :] = z_xd
    s2[SUBLANES:n, :] = xext[SUBLANES:n, :] + xext[SUBLANES - 1:n - 1, :]
    s4[SUBLANES:n, :] = s2[SUBLANES:n, :] + s2[SUBLANES - 2:n - 2, :]
    s8[2 * SUBLANES:n, :] = s4[2 * SUBLANES:n, :] + s4[2 * SUBLANES - 4:n - 4, :]
    win16 = s8[XPAD:n, :] + s8[XPAD - 8:n - 8, :]
    x0 = xext[XPAD:n, :]
    wins = _pool_select([s2[XPAD:n, :], s4[XPAD:n, :], s8[XPAD:n, :], win16], (ts, D_GROUP))
    wvec = _pool_select([jnp.full((1, D_GROUP), float(w), F32) for w in POOL_WINDOWS], (1, D_GROUP))
    pos = (i * ts + lax.broadcasted_iota(jnp.int32, (ts, D_GROUP), 0)).astype(F32)
    d = wins / jnp.minimum(pos + 1.0, wvec) - x0
    y = jnp.dot(d.astype(BF16), pw_ref[0], preferred_element_type=F32) * ps_ref[0]
    mix_ref[:, 2 * D_GROUP:3 * D_GROUP] = y.astype(BF16)

    hext[0:HPAD, :] = hext[ts:ts + HPAD, :]
    xext[SUBLANES:XPAD, :] = xext[ts + SUBLANES:ts + XPAD, :]

    @pl.when(i == pl.num_programs(1) - 1)
    def _():
        convn_ref[0] = hext[HPAD - CONV_HALO:HPAD, :]
        pooln_ref[0] = xext[XPAD - POOL_HALO:XPAD, :]


def _proj_mix_prompt(x2d, wts, kv_prev, layer, *, batch, seq):
    ts = SEQ_TILE
    ns = seq // ts
    m = batch * seq
    depth = wts["w_in"].shape[0]
    row_map = lambda b, i: (b * ns + i, 0)
    vec = _layer_spec((1, D_GROUP), layer)
    ext = lambda pad: pltpu.VMEM((pad + ts, D_GROUP), F32)
    in_specs = [pl.BlockSpec((ts, D_MODEL), row_map), _layer_spec((1, D_MODEL), layer),
                _layer_spec_single((D_MODEL, D_IN), layer),
                _layer_spec((CONV_W, D_GROUP), layer), vec, vec, vec,
                _layer_spec((SGU_HEADS, SGU_CHUNK, SGU_CHUNK), layer),
                _layer_spec((SGU_CHUNK, D_GROUP), layer), vec, vec,
                _layer_spec((D_GROUP, D_GROUP), layer), vec]
    args = [x2d, wts["g_mix_pre"], wts["w_in"], wts["conv_w"], wts["conv_b"], wts["conv_ln_g"], wts["conv_ln_b"],
            wts["sgu_w"], wts["sgu_b_full"], wts["sgu_ln_g"], wts["sgu_ln_b"], wts["pool_w_bd"], wts["pool_scale"]]
    kv_spec = pl.BlockSpec((1, 1, D_GROUP, ts), lambda b, i: (layer, b, 0, i))
    kv_shape = jax.ShapeDtypeStruct((depth, batch, D_GROUP, seq), F32)
    out_specs = [pl.BlockSpec((ts, 3 * D_GROUP), row_map), pl.BlockSpec((ts, D_GROUP), row_map),
                 pl.BlockSpec((1, D_GROUP, ts), lambda b, i: (b, 0, i)), kv_spec, kv_spec,
                 pl.BlockSpec((1, CONV_HALO, D_GROUP), lambda b, i: (b, 0, 0)),
                 pl.BlockSpec((1, POOL_HALO, D_GROUP), lambda b, i: (b, 0, 0))]
    out_shape = [jax.ShapeDtypeStruct((m, 3 * D_GROUP), BF16), jax.ShapeDtypeStruct((m, D_GROUP), F32),
                 jax.ShapeDtypeStruct((batch, D_GROUP, seq), BF16), kv_shape, kv_shape,
                 jax.ShapeDtypeStruct((batch, CONV_HALO, D_GROUP), F32),
                 jax.ShapeDtypeStruct((batch, POOL_HALO, D_GROUP), F32)]
    kv_out_index = 3
    aliases = {}
    n_alias = 0
    if kv_prev is not None:
        n_alias = len(kv_prev)
        for a, buf in enumerate(kv_prev):
            aliases[len(args)] = kv_out_index + a
            in_specs.append(pl.BlockSpec(memory_space=pl.ANY))
            args.append(buf)
    return pl.pallas_call(
        functools.partial(_proj_mix_kernel, ts=ts, n_alias=n_alias),
        grid=(batch, ns),
        in_specs=in_specs,
        out_specs=out_specs,
        out_shape=out_shape,
        input_output_aliases=aliases,
        scratch_shapes=[ext(HPAD), pltpu.VMEM((SUBLANES - 1, ts + HPAD - SUBLANES, D_GROUP), F32),
                        ext(XPAD), ext(XPAD), ext(XPAD), ext(XPAD)],
        compiler_params=pltpu.CompilerParams(
            dimension_semantics=("arbitrary", "arbitrary"), vmem_limit_bytes=VMEM_LIMIT_BYTES),
        name="proj_mix_prompt",
    )(*args)


def _mix_sample_kernel(za_ref, zb_ref, xd_ref, sc_ref, sp_ref, cw_ref, cb_ref, clg_ref, clb_ref, sw4_ref, sb_ref,
                       slg_ref, slb_ref, pw_ref, ps_ref, mix_ref, convn_ref, pooln_ref, sguv_ref, *, db, t_new,
                       past_len):
    rows = lambda t: slice(t * db, (t + 1) * db)

    ext = [sc_ref[0, j] for j in range(CONV_HALO)]
    ext += [za_ref[rows(t), 0:D_GROUP] * _sigmoid(za_ref[rows(t), D_GROUP:2 * D_GROUP]) for t in range(t_new)]
    for t in range(t_new):
        acc = jnp.broadcast_to(cb_ref[0], (db, D_GROUP))
        for j in range(CONV_W):
            acc = acc + cw_ref[0, j:j + 1, :] * ext[t + j]
        mix_ref[rows(t), 0:D_GROUP] = _conv_post(acc, clg_ref[0], clb_ref[0]).astype(BF16)
    for j in range(CONV_HALO):
        convn_ref[j] = ext[t_new + j]

    u = _gelu(zb_ref[:, 0:D_GROUP])
    v = _layer_norm(_gelu(zb_ref[:, D_GROUP:2 * D_GROUP]), slg_ref[0], slb_ref[0])
    sguv_ref[...] = v
    for t in range(t_new):
        mixed = jnp.broadcast_to(sb_ref[0, t:t + 1, :], (db, D_GROUP))
        for s in range(t + 1):
            mixed = mixed + sw4_ref[0, t * t_new + s:t * t_new + s + 1, :] * v[rows(s)]
        mix_ref[rows(t), D_GROUP:2 * D_GROUP] = (u[rows(t)] * mixed).astype(BF16)

    pext = [sp_ref[0, j] for j in range(POOL_HALO)] + [xd_ref[rows(t), :] for t in range(t_new)]
    ds = []
    for t in range(t_new):
        x0 = pext[POOL_HALO + t]
        acc = x0
        wins = []
        for k in range(1, POOL_HALO + 1):
            acc = acc + pext[POOL_HALO + t - k]
            if k + 1 in POOL_WINDOWS:
                wins.append(acc)
        cnts = [jnp.full((1, D_GROUP), float(min(past_len + t + 1, w)), F32) for w in POOL_WINDOWS]
        cnt = _pool_select(cnts, (1, D_GROUP))
        ds.append(_pool_select(wins, (db, D_GROUP)) / cnt - x0)
    d = jnp.concatenate(ds, axis=0)
    y = jnp.dot(d.astype(BF16), pw_ref[0], preferred_element_type=F32) * ps_ref[0]
    mix_ref[:, 2 * D_GROUP:3 * D_GROUP] = y.astype(BF16)
    for j in range(POOL_HALO):
        pooln_ref[j] = pext[t_new + j]


def _mix_sample(za, zb, xd, sc_tm, sp_tm, wts, layer, *, db, t_new, past_len):
    m = db * t_new
    full = lambda shape: pl.BlockSpec(shape, lambda i: (0,) * len(shape))
    vec = _layer_spec((1, D_GROUP), layer)
    return pl.pallas_call(
        functools.partial(_mix_sample_kernel, db=db, t_new=t_new, past_len=past_len),
        grid=(1,),
        in_specs=[full((m, 2 * D_GROUP)), full((m, 2 * D_GROUP)), full((m, D_GROUP)),
                  _layer_spec((CONV_HALO, db, D_GROUP), layer), _layer_spec((POOL_HALO, db, D_GROUP), layer),
                  _layer_spec((CONV_W, D_GROUP), layer), vec, vec, vec,
                  _layer_spec((t_new * t_new, D_GROUP), layer),
                  _layer_spec((SGU_CHUNK, D_GROUP), layer), vec, vec,
                  _layer_spec((D_GROUP, D_GROUP), layer), vec],
        out_specs=[full((m, 3 * D_GROUP)), full((CONV_HALO, db, D_GROUP)), full((POOL_HALO, db, D_GROUP)),
                   full((m, D_GROUP))],
        out_shape=[jax.ShapeDtypeStruct((m, 3 * D_GROUP), BF16),
                   jax.ShapeDtypeStruct((CONV_HALO, db, D_GROUP), F32),
                   jax.ShapeDtypeStruct((POOL_HALO, db, D_GROUP), F32),
                   jax.ShapeDtypeStruct((m, D_GROUP), F32)],
        compiler_params=pltpu.CompilerParams(
            dimension_semantics=("arbitrary",), vmem_limit_bytes=VMEM_LIMIT_BYTES),
        name="mix_sample",
    )(za, zb, xd, sc_tm, sp_tm, wts["conv_w"], wts["conv_b"], wts["conv_ln_g"], wts["conv_ln_b"], wts["sgu_w4"],
      wts["sgu_b_full"], wts["sgu_ln_g"], wts["sgu_ln_b"], wts["pool_w_bd"], wts["pool_scale"])


def _diff_lambda(lq1, lk1, lq2, lk2, lam_init):
    s1 = jnp.sum(lq1 * lk1, axis=-1, keepdims=True)
    s2 = jnp.sum(lq2 * lk2, axis=-1, keepdims=True)
    return jnp.exp(s1) - jnp.exp(s2) + lam_init


def _alibi_slope(h):
    return 2.0 ** (-8.0 * (h + 1) / N_HEADS)


KEY_AUG = LANES
VAL_AUG = 80
N_BIAS = 3


def _attn_prompt_kernel(qt_ref, k_ref, vt_ref, lq1_ref, lk1_ref, lq2_ref, lk2_ref, subg_ref, o_ref, kaug, vaug, *,
                        tq, tk, nblk, lam_init):
    i = pl.program_id(1)
    hd = 2 * DK
    diag_blocks = tq // tk

    @pl.when(i == 0)
    def _():
        key = lax.broadcasted_iota(jnp.int32, (tk, KEY_AUG), 0).astype(F32)
        lane = lax.broadcasted_iota(jnp.int32, (tk, KEY_AUG), 1)
        ones_row = jnp.where(lax.broadcasted_iota(jnp.int32, (VAL_AUG - hd, tk), 0) == 0, 1.0, 0.0).astype(BF16)
        for h in range(N_HEADS):
            bias = (_alibi_slope(h) * LOG2E) * key
            hi = bias.astype(BF16).astype(F32)
            mid = (bias - hi).astype(BF16).astype(F32)
            lo = ((bias - hi) - mid).astype(BF16).astype(F32)
            aug = jnp.where(lane == hd, hi, jnp.where(lane == hd + 1, mid, jnp.where(lane == hd + 2, lo, 0.0)))
            for j in range(nblk):
                kblk = k_ref[j * tk:(j + 1) * tk, h * hd:(h + 1) * hd]
                kpad = jnp.concatenate([kblk, jnp.zeros((tk, KEY_AUG - hd), F32)], axis=1)
                kaug[j, h] = jnp.where(lane < hd, kpad, aug).astype(BF16)
                vaug[j, h, 0:hd, :] = vt_ref[0, 0, h * hd:(h + 1) * hd, j * tk:(j + 1) * tk].astype(BF16)
                vaug[j, h, hd:VAL_AUG, :] = ones_row

    lam = _diff_lambda(lq1_ref[0], lk1_ref[0], lq2_ref[0], lk2_ref[0], lam_init)
    qt = qt_ref[0]
    zero = jnp.zeros((DK, tq), BF16)
    ones_aug = jnp.where(lax.broadcasted_iota(jnp.int32, (KEY_AUG - hd, 2 * tq), 0) < N_BIAS, 1.0, 0.0).astype(BF16)
    qs = []
    for h in range(N_HEADS):
        q1 = qt[h * hd:h * hd + DK]
        q2 = qt[h * hd + DK:(h + 1) * hd]
        body = jnp.concatenate([jnp.concatenate([q1, zero], axis=0), jnp.concatenate([zero, q2], axis=0)], axis=1)
        qs.append(jnp.concatenate([body, ones_aug], axis=0))
    key = lax.broadcasted_iota(jnp.int32, (tk, 2 * tq), 0)
    qry = lax.broadcasted_iota(jnp.int32, (tk, 2 * tq), 1)
    rel = key - jnp.where(qry >= tq, qry - tq, qry)

    def block(j, carry, diag):
        scores = [jnp.dot(kaug[j, h], qs[h], preferred_element_type=F32) for h in range(N_HEADS)]
        probs = []
        for h in range(N_HEADS):
            s = scores[h] if diag is None else jnp.where(rel <= -diag * tk, scores[h], -jnp.inf)
            c = (_alibi_slope(h) * LOG2E) * (j * tk - i * tq).astype(F32)
            m = carry[h][0]
            m_new = jnp.maximum(m, jnp.max(s, axis=0, keepdims=True) + c)
            probs.append((m_new, jnp.exp2(m - m_new), jnp.exp2(s - (m_new - c)).astype(BF16)))
        out = []
        for h in range(N_HEADS):
            m_new, alpha, p = probs[h]
            out.append((m_new, alpha * carry[h][1] + jnp.dot(vaug[j, h], p, preferred_element_type=F32)))
        return tuple(out)

    init = tuple((jnp.full((1, 2 * tq), -jnp.inf, F32), jnp.zeros((VAL_AUG, 2 * tq), F32)) for _ in range(N_HEADS))
    carry = lax.fori_loop(0, i * diag_blocks, functools.partial(block, diag=None), init)
    for d in range(diag_blocks):
        carry = block(i * diag_blocks + d, carry, d)
    ys = []
    for h in range(N_HEADS):
        acc = carry[h][1]
        o = acc[0:hd, :] * (1.0 / acc[hd:hd + 1, :])
        od = o[:, 0:tq] - lam * o[:, tq:2 * tq]
        ms = jnp.mean(od * od, axis=0, keepdims=True)
        ys.append(od * lax.rsqrt(ms + LN_EPS) * subg_ref[0] * (1.0 - lam_init))
    o_ref[...] = jnp.concatenate(ys, axis=0).T.astype(BF16)


def _attn_prompt(qt, k_rows, vt_all, wts, layer, *, batch, seq, lam_init):
    tq, tk = ATT_Q_TILE, ATT_K_TILE
    ns = seq // tq
    nk = seq // tk
    lvec = _layer_spec((1, DK), layer)
    return pl.pallas_call(
        functools.partial(_attn_prompt_kernel, tq=tq, tk=tk, nblk=nk, lam_init=lam_init),
        grid=(batch, ns),
        in_specs=[pl.BlockSpec((1, D_GROUP, tq), lambda b, i: (b, 0, i)),
                  pl.BlockSpec((seq, D_GROUP), lambda b, i: (b, 0)),
                  pl.BlockSpec((1, 1, D_GROUP, seq), lambda b, i: (layer, b, 0, 0)),
                  lvec, lvec, lvec, lvec, _layer_spec((DV, 1), layer)],
        out_specs=pl.BlockSpec((tq, D_GROUP), lambda b, i: (b * ns + i, 0)),
        out_shape=jax.ShapeDtypeStruct((batch * seq, D_GROUP), BF16),
        scratch_shapes=[pltpu.VMEM((nk, N_HEADS, tk, KEY_AUG), BF16), pltpu.VMEM((nk, N_HEADS, VAL_AUG, tk), BF16)],
        compiler_params=pltpu.CompilerParams(
            dimension_semantics=("arbitrary", "arbitrary"), vmem_limit_bytes=VMEM_LIMIT_BYTES),
        name="attn_prompt",
    )(qt, k_rows, vt_all, wts["lambda_q1"], wts["lambda_k1"], wts["lambda_q2"], wts["lambda_k2"],
      wts["attn_sub_g_col"])


def _decode_rows(t_new):
    n_rows = N_HEADS * 2 * t_new
    row1 = lax.broadcasted_iota(jnp.int32, (n_rows, 1), 0)
    t_row = (row1 % t_new).astype(F32)
    slope_row = jnp.zeros((n_rows, 1), F32)
    for h in range(N_HEADS):
        slope_row = jnp.where(row1 // (2 * t_new) == h, _alibi_slope(h), slope_row)
    return n_rows, t_row, slope_row


def _decode_queries(q4, t_new):
    n_rows = N_HEADS * 2 * t_new
    qt = jnp.concatenate([q4 * Q_SCALE] * (2 * N_HEADS), axis=0)
    r = lax.broadcasted_iota(jnp.int32, (n_rows, D_GROUP), 0)
    c = lax.broadcasted_iota(jnp.int32, (n_rows, D_GROUP), 1)
    return jnp.where(c // DK == r // t_new, qt, 0.0).astype(BF16)


def _decode_chunk(qbd, k_pages, v_pages, state, first_pos, q_pos, slope_row, page):
    m_prev, l_prev, acc_prev = state
    n_rows = qbd.shape[0]
    s = jnp.concatenate([jnp.dot(qbd, kp[...].astype(BF16), preferred_element_type=F32) for kp in k_pages], axis=1)
    kpos = (first_pos + lax.broadcasted_iota(jnp.int32, s.shape, 1)).astype(F32)
    s = s - slope_row * (q_pos - kpos)
    m_new = jnp.maximum(m_prev, jnp.max(s, axis=-1, keepdims=True))
    alpha = jnp.exp(m_prev - m_new)
    p = jnp.exp(s - m_new)
    l_new = alpha * l_prev + jnp.sum(p, axis=-1, keepdims=True)
    pb = p.astype(BF16)
    pv = jnp.zeros((n_rows, D_GROUP), F32)
    for idx, vp in enumerate(v_pages):
        pv = pv + lax.dot_general(pb[:, idx * page:(idx + 1) * page], vp[...].astype(BF16), NT_DIMS,
                                  preferred_element_type=F32)
    return m_new, l_new, alpha * acc_prev + pv


def _decode_finish(qbd, state, kn, vn, lam, subg, t_row, slope_row, t_new, lam_init):
    m_prev, l_prev, acc_prev = state
    n_rows = qbd.shape[0]
    kn8 = jnp.concatenate([kn, jnp.zeros((SUBLANES - t_new, D_GROUP), F32)], axis=0).astype(BF16)
    sn = lax.dot_general(qbd, kn8, NT_DIMS, preferred_element_type=F32)
    dist = t_row - lax.broadcasted_iota(jnp.int32, (n_rows, SUBLANES), 1).astype(F32)
    sn = jnp.where(dist >= 0.0, sn - slope_row * dist, -jnp.inf)
    m_fin = jnp.maximum(m_prev, jnp.max(sn, axis=-1, keepdims=True))
    alpha = jnp.exp(m_prev - m_fin)
    pn = jnp.exp(sn - m_fin)
    l_fin = alpha * l_prev + jnp.sum(pn, axis=-1, keepdims=True)
    acc = alpha * acc_prev
    for t in range(t_new):
        acc = acc + pn[:, t:t + 1] * vn[t:t + 1, :]
    o = acc / l_fin
    r = lax.broadcasted_iota(jnp.int32, (n_rows, D_GROUP), 0)
    c = lax.broadcasted_iota(jnp.int32, (n_rows, D_GROUP), 1)
    o = jnp.where(c // DV == r // (2 * t_new), o, 0.0)
    y0 = jnp.zeros((t_new, D_GROUP), F32)
    y1 = jnp.zeros((t_new, D_GROUP), F32)
    for h in range(N_HEADS):
        y0 = y0 + o[h * 2 * t_new:h * 2 * t_new + t_new]
        y1 = y1 + o[h * 2 * t_new + t_new:(h + 1) * 2 * t_new]
    od = y0 - lam * y1
    seg_r = lax.broadcasted_iota(jnp.int32, (D_GROUP, D_GROUP), 0) // DV
    seg_c = lax.broadcasted_iota(jnp.int32, (D_GROUP, D_GROUP), 1) // DV
    seg = jnp.where(seg_r == seg_c, 1.0 / DV, 0.0).astype(F32)
    ms = jnp.dot(od * od, seg, preferred_element_type=F32, precision=lax.Precision.HIGHEST)
    return od * lax.rsqrt(ms + LN_EPS) * subg * (1.0 - lam_init)


def _post_head(x_ref, mix_ref, yc_ref, wo_ref, gpost_ref, gfpre_ref, hn_s):
    ab, cd = 2 * D_GROUP, 3 * D_GROUP
    mixo = jnp.dot(mix_ref[:, 0:ab], wo_ref[0, 0:ab, :], preferred_element_type=F32)
    mixo = mixo + jnp.dot(yc_ref[...].astype(BF16), wo_ref[0, ab:cd, :], preferred_element_type=F32)
    mixo = mixo + jnp.dot(mix_ref[:, ab:cd], wo_ref[0, cd:D_MODEL, :], preferred_element_type=F32)
    x1 = x_ref[...] + _rms(mixo, gpost_ref[0], RMS_EPS)
    hn_s[...] = _rms(x1, gfpre_ref[0], RMS_EPS).astype(BF16)
    return x1


def _ffn_chunk(c, hn_s, wgu_ref, act_s):
    lo = c * FF_CHUNK
    gate = jnp.dot(hn_s[...], wgu_ref[0, :, lo:lo + FF_CHUNK], preferred_element_type=F32)
    up = jnp.dot(hn_s[...], wgu_ref[0, :, D_FF + lo:D_FF + lo + FF_CHUNK], preferred_element_type=F32)
    act_s[:, lo:lo + FF_CHUNK] = (gate * _sigmoid(gate) * up).astype(BF16)


def _post_tail(x1, act_s, wd_ref, gfpost_ref, o_ref):
    ffn = jnp.dot(act_s[...], wd_ref[0], preferred_element_type=F32)
    o_ref[...] = x1 + _rms(ffn, gfpost_ref[0], RMS_EPS)


def _post_kernel(x_ref, mix_ref, yc_ref, wo_ref, gpost_ref, gfpre_ref, wgu_ref, wd_ref, gfpost_ref, o_ref,
                 hn_s, act_s):
    x1 = _post_head(x_ref, mix_ref, yc_ref, wo_ref, gpost_ref, gfpre_ref, hn_s)
    for c in range(N_FF_CHUNKS):
        _ffn_chunk(c, hn_s, wgu_ref, act_s)
    _post_tail(x1, act_s, wd_ref, gfpost_ref, o_ref)


def _post_specs(tm, layer, row):
    vec = _layer_spec((1, D_MODEL), layer)
    return [pl.BlockSpec((tm, D_MODEL), row),
            pl.BlockSpec((tm, 3 * D_GROUP), row),
            pl.BlockSpec((tm, D_GROUP), row),
            _layer_spec_single((D_MODEL, D_MODEL), layer), vec, vec,
            _layer_spec_single((D_MODEL, 2 * D_FF), layer),
            _layer_spec_single((D_FF, D_MODEL), layer), vec]


def _post_args(x2d, mix3, yc, wts):
    return (x2d, mix3, yc, wts["w_out"], wts["g_mix_post"], wts["g_ffn_pre"], wts["w_gu"], wts["w_down"],
            wts["g_ffn_post"])


def _post(x2d, mix3, yc, wts, layer, *, tm):
    m = x2d.shape[0]
    row = lambda i: (i, 0)
    return pl.pallas_call(
        _post_kernel,
        grid=(m // tm,),
        in_specs=_post_specs(tm, layer, row),
        out_specs=pl.BlockSpec((tm, D_MODEL), row),
        out_shape=jax.ShapeDtypeStruct((m, D_MODEL), F32),
        scratch_shapes=[pltpu.VMEM((tm, D_MODEL), BF16), pltpu.VMEM((tm, D_FF), BF16)],
        compiler_params=pltpu.CompilerParams(
            dimension_semantics=("arbitrary",), vmem_limit_bytes=VMEM_LIMIT_BYTES),
        name="post",
    )(*_post_args(x2d, mix3, yc, wts))


def _post_attn_kernel(pt_ref, x_ref, mix_ref, yc_ref, wo_ref, gpost_ref, gfpre_ref, wgu_ref, wd_ref, gfpost_ref,
                      q_ref, kn_ref, vn_ref, lq1_ref, lk1_ref, lq2_ref, lk2_ref, subg_ref, kc_hbm, vc_hbm,
                      o_ref, oa_ref, hn_s, act_s, kbuf, vbuf, sems, *, layer, db, seqs_per_step, t_new, page,
                      n_chunks, chunk_pages, past_len, lam_init):
    i = pl.program_id(0)
    regions = seqs_per_step * n_chunks
    n_rows, t_row, slope_row = _decode_rows(t_new)

    def page_copies(bb, cc, sl, lookup):
        copies = []
        for idx in range(chunk_pages):
            pid = pt_ref[bb, cc * chunk_pages + idx] if lookup else 0
            copies.append(pltpu.make_async_copy(kc_hbm.at[layer, pid], kbuf.at[sl, idx], sems.at[0, sl]))
            copies.append(pltpu.make_async_copy(vc_hbm.at[layer, pid], vbuf.at[sl, idx], sems.at[1, sl]))
        return copies

    def gather_rows(ref, bb):
        return jnp.concatenate([ref[pl.ds(t * db + bb, 1), :] for t in range(t_new)], axis=0)

    @pl.when(i == 0)
    def _():
        for cp in page_copies(0, 0, 0, True):
            cp.start()

    x1 = _post_head(x_ref, mix_ref, yc_ref, wo_ref, gpost_ref, gfpre_ref, hn_s)
    lam = _diff_lambda(lq1_ref[0], lk1_ref[0], lq2_ref[0], lk2_ref[0], lam_init)
    ffn_order = [list(range(N_FF_CHUNKS))[r::regions] for r in range(regions)]

    for r in range(seqs_per_step):
        b = i * seqs_per_step + r
        qbd = _decode_queries(gather_rows(q_ref, b), t_new)
        state = (jnp.full((n_rows, 1), -jnp.inf, F32), jnp.zeros((n_rows, 1), F32),
                 jnp.zeros((n_rows, D_GROUP), F32))
        for c in range(n_chunks):
            region = r * n_chunks + c
            slot = lax.rem(i * regions + region, 2)
            if c + 1 < n_chunks:
                for cp in page_copies(b, c + 1, 1 - slot, True):
                    cp.start()
            else:
                @pl.when(b + 1 < db)
                def _():
                    for cp in page_copies(b + 1, 0, 1 - slot, True):
                        cp.start()
            for cp in page_copies(b, c, slot, False):
                cp.wait()
            k_pages = [kbuf.at[slot, idx] for idx in range(chunk_pages)]
            v_pages = [vbuf.at[slot, idx] for idx in range(chunk_pages)]
            state = _decode_chunk(qbd, k_pages, v_pages, state, c * chunk_pages * page, past_len + t_row,
                                  slope_row, page)
            for f in ffn_order[region]:
                _ffn_chunk(f, hn_s, wgu_ref, act_s)
        y = _decode_finish(qbd, state, gather_rows(kn_ref, b), gather_rows(vn_ref, b), lam, subg_ref[0], t_row,
                           slope_row, t_new, lam_init)
        for t in range(t_new):
            oa_ref[pl.ds(t * db + b, 1), :] = y[t:t + 1, :]

    _post_tail(x1, act_s, wd_ref, gfpost_ref, o_ref)


def _post_attn(x2d, mix3, yc, page_table, q, kn, vn, cache_kt, cache_vt, wts, layer, *, tm, db, t_new, lam_init):
    m = x2d.shape[0]
    n_steps = m // tm
    n_pages = page_table.shape[1]
    page = cache_kt.shape[-1]
    chunk_pages = min(DECODE_CHUNK_PAGES, n_pages)
    assert db % n_steps == 0 and n_pages % chunk_pages == 0
    ms = db * t_new
    row = lambda i, pt: (i, 0)
    full = lambda i, pt: (0, 0)
    lvec = _layer_spec((1, DK), layer)
    in_specs = _post_specs(tm, layer, row)
    in_specs += [pl.BlockSpec((ms, D_GROUP), full)] * 3 + [lvec] * 4 + [_layer_spec((1, D_GROUP), layer)]
    in_specs += [pl.BlockSpec(memory_space=pl.ANY)] * 2
    page_buf = pltpu.VMEM((2, chunk_pages, D_GROUP, page), F32)
    grid_spec = pltpu.PrefetchScalarGridSpec(
        num_scalar_prefetch=1,
        grid=(n_steps,),
        in_specs=in_specs,
        out_specs=[pl.BlockSpec((tm, D_MODEL), row), pl.BlockSpec((ms, D_GROUP), full)],
        scratch_shapes=[pltpu.VMEM((tm, D_MODEL), BF16), pltpu.VMEM((tm, D_FF), BF16), page_buf, page_buf,
                        pltpu.SemaphoreType.DMA((2, 2))],
    )
    return pl.pallas_call(
        functools.partial(_post_attn_kernel, layer=layer, db=db, seqs_per_step=db // n_steps, t_new=t_new, page=page,
                          n_chunks=n_pages // chunk_pages, chunk_pages=chunk_pages, past_len=n_pages * page,
                          lam_init=lam_init),
        grid_spec=grid_spec,
        out_shape=[jax.ShapeDtypeStruct((m, D_MODEL), F32), jax.ShapeDtypeStruct((ms, D_GROUP), F32)],
        compiler_params=pltpu.CompilerParams(
            dimension_semantics=("arbitrary",), vmem_limit_bytes=VMEM_LIMIT_BYTES),
        name="post_attn",
    )(page_table, *_post_args(x2d, mix3, yc, wts), q, kn, vn, wts["lambda_q1"], wts["lambda_k1"], wts["lambda_q2"],
      wts["lambda_k2"], wts["attn_sub_g4"], cache_kt, cache_vt)


def _prepare_weights(w_in, w_out, conv_w, conv_b, conv_ln_g, conv_ln_b, sgu_w, sgu_b, sgu_ln_g, sgu_ln_b,
                     lambda_q1, lambda_k1, lambda_q2, lambda_k2, attn_sub_g, pool_w, pool_scale, g_mix_pre,
                     g_mix_post, g_ffn_pre, g_ffn_post, w_gu, w_down, t_new):
    depth = w_in.shape[0]
    hd = D_GROUP // SGU_HEADS
    sgu_w4 = jnp.repeat(jnp.transpose(sgu_w[:, :, :t_new, :t_new], (0, 2, 3, 1)), hd, axis=3)
    eye = jnp.eye(len(POOL_WINDOWS), dtype=F32)
    row = lambda a: a[:, None, :]
    return {
        "w_in": w_in.astype(BF16), "w_out": w_out.astype(BF16), "w_gu": w_gu.astype(BF16),
        "w_down": w_down.astype(BF16),
        "conv_w": conv_w, "conv_b": row(conv_b), "conv_ln_g": row(conv_ln_g), "conv_ln_b": row(conv_ln_b),
        "sgu_w": sgu_w,
        "sgu_b_full": jnp.repeat(jnp.swapaxes(sgu_b, 1, 2), hd, axis=2),
        "sgu_w4": sgu_w4.reshape(depth, t_new * t_new, D_GROUP),
        "sgu_ln_g": row(sgu_ln_g), "sgu_ln_b": row(sgu_ln_b),
        "lambda_q1": row(lambda_q1), "lambda_k1": row(lambda_k1), "lambda_q2": row(lambda_q2),
        "lambda_k2": row(lambda_k2),
        "attn_sub_g_col": attn_sub_g[:, :, None], "attn_sub_g4": jnp.tile(row(attn_sub_g), (1, 1, N_HEADS)),
        "pool_w_bd": jnp.einsum("lgce,gh->lgche", pool_w, eye).reshape(depth, D_GROUP, D_GROUP).astype(BF16),
        "pool_scale": row(pool_scale),
        "g_mix_pre": row(g_mix_pre), "g_mix_post": row(g_mix_post), "g_ffn_pre": row(g_ffn_pre),
        "g_ffn_post": row(g_ffn_post),
    }


def kernel(x_prompt, x_sample, cache_k, cache_v, state_conv, state_pool, page_table, w_in, w_out, conv_w, conv_b, conv_ln_g, conv_ln_b, sgu_w, sgu_b, sgu_ln_g, sgu_ln_b, lambda_q1, lambda_k1, lambda_q2, lambda_k2, attn_sub_g, pool_w, pool_scale, g_mix_pre, g_mix_post, g_ffn_pre, g_ffn_post, w_gu, w_down):
    batch, seq, _ = x_prompt.shape
    db, t_new, _ = x_sample.shape
    depth, n_pool, page = cache_k.shape[0], cache_k.shape[1], cache_k.shape[2]
    past_len = page_table.shape[1] * page
    assert seq % ROW_TILE == 0 and seq % SEQ_TILE == 0 and db % SUBLANES == 0
    assert seq % ATT_Q_TILE == 0 and ATT_Q_TILE % ATT_K_TILE == 0
    assert page == LANES

    wts = _prepare_weights(w_in, w_out, conv_w, conv_b, conv_ln_g, conv_ln_b, sgu_w, sgu_b, sgu_ln_g, sgu_ln_b,
                           lambda_q1, lambda_k1, lambda_q2, lambda_k2, attn_sub_g, pool_w, pool_scale,
                           g_mix_pre, g_mix_post, g_ffn_pre, g_ffn_post, w_gu, w_down, t_new)

    cache_kt = jnp.transpose(cache_k, (0, 1, 3, 4, 5, 2)).reshape(depth, n_pool, D_GROUP, page)
    cache_vt = jnp.transpose(cache_v, (0, 1, 3, 4, 2)).reshape(depth, n_pool, D_GROUP, page)
    ms = db * t_new
    xs = jnp.swapaxes(x_sample, 0, 1).reshape(ms, D_MODEL)
    sc_tm = jnp.swapaxes(state_conv, 1, 2)
    sp_tm = jnp.swapaxes(state_pool, 1, 2)
    xp = x_prompt.reshape(batch * seq, D_MODEL)

    kv_all = None
    cp_l, pp_l = [], []
    ks_l, vs_l, cs_l, ps_l, gs_l = [], [], [], [], []
    for l in range(depth):
        lam_init = 0.8 - 0.6 * math.exp(-0.3 * l)
        mix3, k_rows, qt, kt_all, vt_all, conv_new, pool_new = _proj_mix_prompt(xp, wts, kv_all, l, batch=batch,
                                                                                seq=seq)
        kv_all = (kt_all, vt_all)
        yc = _attn_prompt(qt, k_rows, vt_all, wts, l, batch=batch, seq=seq, lam_init=lam_init)
        cp_l.append(conv_new); pp_l.append(pool_new)
        za, zb, q, kn, vn, xd = _in_proj(xs, wts["g_mix_pre"], wts["w_in"], layer=l, groups=SAMPLE_GROUPS, tm=ms)
        mix3_s, conv_new, pool_new, sgu_v = _mix_sample(za, zb, xd, sc_tm, sp_tm, wts, l, db=db, t_new=t_new,
                                                        past_len=past_len)
        xp, yc_s = _post_attn(xp, mix3, yc, page_table, q, kn, vn, cache_kt, cache_vt, wts, l, tm=ROW_TILE, db=db,
                              t_new=t_new, lam_init=lam_init)
        xs = _post(xs, mix3_s, yc_s, wts, l, tm=ms)
        ks_l.append(kn); vs_l.append(vn); cs_l.append(conv_new); ps_l.append(pool_new); gs_l.append(sgu_v)

    kt_all, vt_all = kv_all
    y_prompt = xp.reshape(batch, seq, D_MODEL)
    y_sample = jnp.swapaxes(xs.reshape(t_new, db, D_MODEL), 0, 1)
    k_prompt = jnp.transpose(kt_all.reshape(depth, batch, N_HEADS, 2, DK, seq), (0, 1, 5, 2, 3, 4))
    v_prompt = jnp.transpose(vt_all.reshape(depth, batch, N_HEADS, DV, seq), (0, 1, 4, 2, 3))
    tm_to_bm = lambda a: jnp.swapaxes(jnp.stack(a).reshape(depth, t_new, db, D_GROUP), 1, 2)
    k_sample = tm_to_bm(ks_l).reshape(depth, db, t_new, N_HEADS, 2, DK)
    v_sample = tm_to_bm(vs_l).reshape(depth, db, t_new, N_HEADS, DV)
    conv_sample = jnp.swapaxes(jnp.stack(cs_l), 1, 2)
    pool_sample = jnp.swapaxes(jnp.stack(ps_l), 1, 2)
    sgu_v_sample = tm_to_bm(gs_l)
    return (y_prompt, y_sample, k_prompt, v_prompt, jnp.stack(cp_l), jnp.stack(pp_l),
            k_sample, v_sample, conv_sample, pool_sample, sgu_v_sample)
```

```python
import functools
import math

import jax
import jax.numpy as jnp
from jax import lax
from jax.experimental import pallas as pl
from jax.experimental.pallas import tpu as pltpu

F32 = jnp.float32
BF16 = jnp.bfloat16

D_MODEL = 1024
D_GROUP = 256
D_IN = 8 * D_GROUP
CONV_W = 31
CONV_HALO = CONV_W - 1
SGU_CHUNK = 128
SGU_HEADS = 4
N_HEADS = 4
DK = 32
DV = 64
POOL_WINDOWS = (2, 4, 8, 16)
POOL_HALO = max(POOL_WINDOWS) - 1
POOL_GD = D_GROUP // len(POOL_WINDOWS)
D_FF = 2816
FF_CHUNK = 256
N_FF_CHUNKS = D_FF // FF_CHUNK
RMS_EPS = 1e-6
LN_EPS = 1e-5
Q_SCALE = DK ** -0.5
LOG2E = math.log2(math.e)

SUBLANES = 8
LANES = 128
VMEM_LIMIT_BYTES = 56 * 1024 * 1024

ROW_TILE = 512
SEQ_TILE = 512
CONV_ROWS = 64
ATT_Q_TILE = 512
ATT_K_TILE = 256
DECODE_CHUNK_PAGES = 32

NT_DIMS = (((1,), (1,)), ((), ()))


def _rms(x, g, eps):
    return x * lax.rsqrt(jnp.mean(x * x, axis=-1, keepdims=True) + eps) * g


def _layer_norm(x, g, b):
    mu = jnp.mean(x, axis=-1, keepdims=True)
    xc = x - mu
    var = jnp.mean(xc * xc, axis=-1, keepdims=True)
    return xc * lax.rsqrt(var + LN_EPS) * g + b


def _gelu(x):
    return 0.5 * x * (1.0 + lax.erf(x * (2.0 ** -0.5)))


def _sigmoid(x):
    return 1.0 / (1.0 + jnp.exp(-x))


def _lane_group_select(pieces, group_width, shape):
    lane = lax.broadcasted_iota(jnp.int32, shape, len(shape) - 1)
    out = pieces[-1]
    for g in range(len(pieces) - 2, -1, -1):
        out = jnp.where(lane < (g + 1) * group_width, pieces[g], out)
    return out


def _layer_spec(shape, layer):
    zeros = (0,) * len(shape)
    return pl.BlockSpec((1,) + tuple(shape), lambda *_: (layer,) + zeros)


def _layer_spec_single(shape, layer):
    zeros = (0,) * len(shape)
    return pl.BlockSpec((1,) + tuple(shape), lambda *_: (layer,) + zeros, pipeline_mode=pl.Buffered(1))


AG_COL, UV_COL, Q_COL, K_COL, V_COL, XD_COL = (0, 2 * D_GROUP, 4 * D_GROUP, 5 * D_GROUP, 6 * D_GROUP, 7 * D_GROUP)


def _conv_post(acc, clg, clb):
    y = _layer_norm(acc, clg, clb)
    return y * _sigmoid(y)


def _pool_select(wins, shape):
    return _lane_group_select(wins, POOL_GD, shape)


HPAD = 32
XPAD = 24


def _proj_mix_kernel(x_ref, g_ref, w_ref, cw_ref, cb_ref, clg_ref, clb_ref, sw_ref, sb_ref, slg_ref, slb_ref,
                     pw_ref, ps_ref, *rest, ts, n_alias):
    (mix_ref, k_ref, qt_ref, kt_ref, vt_ref, convn_ref, pooln_ref, hext, hshift, xext, s2, s4, s8) = rest[n_alias:]
    i = pl.program_id(1)

    @pl.when(i == 0)
    def _():
        hext[0:HPAD, :] = jnp.zeros((HPAD, D_GROUP), F32)
        xext[0:XPAD, :] = jnp.zeros((XPAD, D_GROUP), F32)
        for buf in (s2, s4, s8):
            buf[0:SUBLANES, :] = jnp.zeros((SUBLANES, D_GROUP), F32)

    hn = _rms(x_ref[...], g_ref[0], RMS_EPS).astype(BF16)
    proj = lambda col, width: jnp.dot(hn, w_ref[0, :, col:col + width], preferred_element_type=F32)
    z_ag = proj(AG_COL, 2 * D_GROUP)
    z_uv = proj(UV_COL, 2 * D_GROUP)
    z_qkv = proj(Q_COL, 3 * D_GROUP)
    z_xd = proj(XD_COL, D_GROUP)

    k_rows = z_qkv[:, D_GROUP:2 * D_GROUP]
    k_ref[...] = k_rows
    qt_ref[0] = (z_qkv[:, 0:D_GROUP] * (Q_SCALE * LOG2E)).T.astype(BF16)
    kt_ref[0, 0] = k_rows.T
    vt_ref[0, 0] = z_qkv[:, 2 * D_GROUP:3 * D_GROUP].T

    hext[HPAD:HPAD + ts, :] = z_ag[:, 0:D_GROUP] * _sigmoid(z_ag[:, D_GROUP:2 * D_GROUP])
    slab = ts + HPAD - SUBLANES
    for rho in range(1, SUBLANES):
        hshift[rho - 1] = hext[rho:rho + slab, :]
    for c in range(ts // CONV_ROWS):
        r0 = c * CONV_ROWS
        acc = jnp.broadcast_to(cb_ref[0], (CONV_ROWS, D_GROUP))
        for j in range(CONV_W):
            shift = HPAD - CONV_HALO + j
            q8, rho = shift // SUBLANES * SUBLANES, shift % SUBLANES
            if rho == 0:
                rows = hext[r0 + q8:r0 + q8 + CONV_ROWS, :]
            else:
                rows = hshift[rho - 1, r0 + q8:r0 + q8 + CONV_ROWS, :]
            acc = acc + cw_ref[0, j:j + 1, :] * rows
        mix_ref[r0:r0 + CONV_ROWS, 0:D_GROUP] = _conv_post(acc, clg_ref[0], clb_ref[0]).astype(BF16)

    u = _gelu(z_uv[:, 0:D_GROUP])
    v = _layer_norm(_gelu(z_uv[:, D_GROUP:2 * D_GROUP]), slg_ref[0], slb_ref[0])
    tri_r = lax.broadcasted_iota(jnp.int32, (SGU_CHUNK, SGU_CHUNK), 0)
    tri_c = lax.broadcasted_iota(jnp.int32, (SGU_CHUNK, SGU_CHUNK), 1)
    w_heads = [jnp.where(tri_c <= tri_r, sw_ref[0, g], 0.0).astype(BF16) for g in range(SGU_HEADS)]
    w_stack = jnp.concatenate(w_heads, axis=0)
    hd = D_GROUP // SGU_HEADS
    for c in range(ts // SGU_CHUNK):
        rows = slice(c * SGU_CHUNK, (c + 1) * SGU_CHUNK)
        r = jnp.dot(w_stack, v[rows].astype(BF16), preferred_element_type=F32)
        mixed = _lane_group_select([r[g * SGU_CHUNK:(g + 1) * SGU_CHUNK] for g in range(SGU_HEADS)],
                                   hd, (SGU_CHUNK, D_GROUP))
        mix_ref[rows, D_GROUP:2 * D_GROUP] = (u[rows] * (mixed + sb_ref[0])).astype(BF16)

    n = XPAD + ts
    xext[XPAD:n, :] = z_xd
    s2[SUBLANES:n, :] = xext[SUBLANES:n, :] + xext[SUBLANES - 1:n - 1, :]
    s4[SUBLANES:n, :] = s2[SUBLANES:n, :] + s2[SUBLANES - 2:n - 2, :]
    s8[2 * SUBLANES:n, :] = s4[2 * SUBLANES:n, :] + s4[2 * SUBLANES - 4:n - 4, :]
    win16 = s8[XPAD:n, :] + s8[XPAD - 8:n - 8, :]
    x0 = xext[XPAD:n, :]
    wins = _pool_select([s2[XPAD:n, :], s4[XPAD:n, :], s8[XPAD:n, :], win16], (ts, D_GROUP))
    wvec = _pool_select([jnp.full((1, D_GROUP), float(w), F32) for w in POOL_WINDOWS], (1, D_GROUP))
    pos = (i * ts + lax.broadcasted_iota(jnp.int32, (ts, D_GROUP), 0)).astype(F32)
    d = wins / jnp.minimum(pos + 1.0, wvec) - x0
    y = jnp.dot(d.astype(BF16), pw_ref[0], preferred_element_type=F32) * ps_ref[0]
    mix_ref[:, 2 * D_GROUP:3 * D_GROUP] = y.astype(BF16)

    hext[0:HPAD, :] = hext[ts:ts + HPAD, :]
    xext[SUBLANES:XPAD, :] = xext[ts + SUBLANES:ts + XPAD, :]

    @pl.when(i == pl.num_programs(1) - 1)
    def _():
        convn_ref[0] = hext[HPAD - CONV_HALO:HPAD, :]
        pooln_ref[0] = xext[XPAD - POOL_HALO:XPAD, :]


def _proj_mix_prompt(x2d, wts, kv_prev, layer, *, batch, seq):
    ts = SEQ_TILE
    ns = seq // ts
    m = batch * seq
    depth = wts["w_in"].shape[0]
    row_map = lambda b, i: (b * ns + i, 0)
    vec = _layer_spec((1, D_GROUP), layer)
    ext = lambda pad: pltpu.VMEM((pad + ts, D_GROUP), F32)
    in_specs = [pl.BlockSpec((ts, D_MODEL), row_map), _layer_spec((1, D_MODEL), layer),
                _layer_spec_single((D_MODEL, D_IN), layer),
                _layer_spec((CONV_W, D_GROUP), layer), vec, vec, vec,
                _layer_spec((SGU_HEADS, SGU_CHUNK, SGU_CHUNK), layer),
                _layer_spec((SGU_CHUNK, D_GROUP), layer), vec, vec,
                _layer_spec((D_GROUP, D_GROUP), layer), vec]
    args = [x2d, wts["g_mix_pre"], wts["w_in"], wts["conv_w"], wts["conv_b"], wts["conv_ln_g"], wts["conv_ln_b"],
            wts["sgu_w"], wts["sgu_b_full"], wts["sgu_ln_g"], wts["sgu_ln_b"], wts["pool_w_bd"], wts["pool_scale"]]
    kv_spec = pl.BlockSpec((1, 1, D_GROUP, ts), lambda b, i: (layer, b, 0, i))
    kv_shape = jax.ShapeDtypeStruct((depth, batch, D_GROUP, seq), F32)
    out_specs = [pl.BlockSpec((ts, 3 * D_GROUP), row_map), pl.BlockSpec((ts, D_GROUP), row_map),
                 pl.BlockSpec((1, D_GROUP, ts), lambda b, i: (b, 0, i)), kv_spec, kv_spec,
                 pl.BlockSpec((1, CONV_HALO, D_GROUP), lambda b, i: (b, 0, 0)),
                 pl.BlockSpec((1, POOL_HALO, D_GROUP), lambda b, i: (b, 0, 0))]
    out_shape = [jax.ShapeDtypeStruct((m, 3 * D_GROUP), BF16), jax.ShapeDtypeStruct((m, D_GROUP), F32),
                 jax.ShapeDtypeStruct((batch, D_GROUP, seq), BF16), kv_shape, kv_shape,
                 jax.ShapeDtypeStruct((batch, CONV_HALO, D_GROUP), F32),
                 jax.ShapeDtypeStruct((batch, POOL_HALO, D_GROUP), F32)]
    kv_out_index = 3
    aliases = {}
    n_alias = 0
    if kv_prev is not None:
        n_alias = len(kv_prev)
        for a, buf in enumerate(kv_prev):
            aliases[len(args)] = kv_out_index + a
            in_specs.append(pl.BlockSpec(memory_space=pl.ANY))
            args.append(buf)
    return pl.pallas_call(
        functools.partial(_proj_mix_kernel, ts=ts, n_alias=n_alias),
        grid=(batch, ns),
        in_specs=in_specs,
        out_specs=out_specs,
        out_shape=out_shape,
        input_output_aliases=aliases,
        scratch_shapes=[ext(HPAD), pltpu.VMEM((SUBLANES - 1, ts + HPAD - SUBLANES, D_GROUP), F32),
                        ext(XPAD), ext(XPAD), ext(XPAD), ext(XPAD)],
        compiler_params=pltpu.CompilerParams(
            dimension_semantics=("arbitrary", "arbitrary"), vmem_limit_bytes=VMEM_LIMIT_BYTES),
        name="proj_mix_prompt",
    )(*args)


def _proj_mix_sample_kernel(x_ref, g_ref, w_ref, sc_ref, sp_ref, cw_ref, cb_ref, clg_ref, clb_ref, sw4_ref, sb_ref,
                            slg_ref, slb_ref, pw_ref, ps_ref, mix_ref, q_ref, k_ref, v_ref, convn_ref, pooln_ref,
                            sguv_ref, *, db, t_new, past_len):
    rows = lambda t: slice(t * db, (t + 1) * db)
    h = _rms(x_ref[...], g_ref[0], RMS_EPS).astype(BF16)
    z = jnp.dot(h, w_ref[0], preferred_element_type=F32)
    q_ref[...] = z[:, Q_COL:Q_COL + D_GROUP]
    k_ref[...] = z[:, K_COL:K_COL + D_GROUP]
    v_ref[...] = z[:, V_COL:V_COL + D_GROUP]

    ext = [sc_ref[0, j] for j in range(CONV_HALO)]
    ext += [z[rows(t), AG_COL:AG_COL + D_GROUP] * _sigmoid(z[rows(t), AG_COL + D_GROUP:AG_COL + 2 * D_GROUP])
            for t in range(t_new)]
    for t in range(t_new):
        acc = jnp.broadcast_to(cb_ref[0], (db, D_GROUP))
        for j in range(CONV_W):
            acc = acc + cw_ref[0, j:j + 1, :] * ext[t + j]
        mix_ref[rows(t), 0:D_GROUP] = _conv_post(acc, clg_ref[0], clb_ref[0]).astype(BF16)
    for j in range(CONV_HALO):
        convn_ref[j] = ext[t_new + j]

    u = _gelu(z[:, UV_COL:UV_COL + D_GROUP])
    v = _layer_norm(_gelu(z[:, UV_COL + D_GROUP:UV_COL + 2 * D_GROUP]), slg_ref[0], slb_ref[0])
    sguv_ref[...] = v
    for t in range(t_new):
        mixed = jnp.broadcast_to(sb_ref[0, t:t + 1, :], (db, D_GROUP))
        for s in range(t + 1):
            mixed = mixed + sw4_ref[0, t * t_new + s:t * t_new + s + 1, :] * v[rows(s)]
        mix_ref[rows(t), D_GROUP:2 * D_GROUP] = (u[rows(t)] * mixed).astype(BF16)

    pext = [sp_ref[0, j] for j in range(POOL_HALO)] + [z[rows(t), XD_COL:XD_COL + D_GROUP] for t in range(t_new)]
    ds = []
    for t in range(t_new):
        x0 = pext[POOL_HALO + t]
        acc = x0
        wins = []
        for k in range(1, POOL_HALO + 1):
            acc = acc + pext[POOL_HALO + t - k]
            if k + 1 in POOL_WINDOWS:
                wins.append(acc)
        cnts = [jnp.full((1, D_GROUP), float(min(past_len + t + 1, w)), F32) for w in POOL_WINDOWS]
        cnt = _pool_select(cnts, (1, D_GROUP))
        ds.append(_pool_select(wins, (db, D_GROUP)) / cnt - x0)
    d = jnp.concatenate(ds, axis=0)
    y = jnp.dot(d.astype(BF16), pw_ref[0], preferred_element_type=F32) * ps_ref[0]
    mix_ref[:, 2 * D_GROUP:3 * D_GROUP] = y.astype(BF16)
    for j in range(POOL_HALO):
        pooln_ref[j] = pext[t_new + j]


def _proj_mix_sample(xs, sc_tm, sp_tm, wts, layer, *, db, t_new, past_len):
    m = db * t_new
    full = lambda shape: pl.BlockSpec(shape, lambda i: (0,) * len(shape))
    vec = _layer_spec((1, D_GROUP), layer)
    act = jax.ShapeDtypeStruct((m, D_GROUP), F32)
    return pl.pallas_call(
        functools.partial(_proj_mix_sample_kernel, db=db, t_new=t_new, past_len=past_len),
        grid=(1,),
        in_specs=[full((m, D_MODEL)), _layer_spec((1, D_MODEL), layer), _layer_spec((D_MODEL, D_IN), layer),
                  _layer_spec((CONV_HALO, db, D_GROUP), layer), _layer_spec((POOL_HALO, db, D_GROUP), layer),
                  _layer_spec((CONV_W, D_GROUP), layer), vec, vec, vec,
                  _layer_spec((t_new * t_new, D_GROUP), layer),
                  _layer_spec((SGU_CHUNK, D_GROUP), layer), vec, vec,
                  _layer_spec((D_GROUP, D_GROUP), layer), vec],
        out_specs=[full((m, 3 * D_GROUP)), full((m, D_GROUP)), full((m, D_GROUP)), full((m, D_GROUP)),
                   full((CONV_HALO, db, D_GROUP)), full((POOL_HALO, db, D_GROUP)), full((m, D_GROUP))],
        out_shape=[jax.ShapeDtypeStruct((m, 3 * D_GROUP), BF16), act, act, act,
                   jax.ShapeDtypeStruct((CONV_HALO, db, D_GROUP), F32),
                   jax.ShapeDtypeStruct((POOL_HALO, db, D_GROUP), F32), act],
        compiler_params=pltpu.CompilerParams(
            dimension_semantics=("arbitrary",), vmem_limit_bytes=VMEM_LIMIT_BYTES),
        name="proj_mix_sample",
    )(xs, wts["g_mix_pre"], wts["w_in"], sc_tm, sp_tm, wts["conv_w"], wts["conv_b"], wts["conv_ln_g"],
      wts["conv_ln_b"], wts["sgu_w4"], wts["sgu_b_full"], wts["sgu_ln_g"], wts["sgu_ln_b"], wts["pool_w_bd"],
      wts["pool_scale"])


def _diff_lambda(lq1, lk1, lq2, lk2, lam_init):
    s1 = jnp.sum(lq1 * lk1, axis=-1, keepdims=True)
    s2 = jnp.sum(lq2 * lk2, axis=-1, keepdims=True)
    return jnp.exp(s1) - jnp.exp(s2) + lam_init


def _alibi_slope(h):
    return 2.0 ** (-8.0 * (h + 1) / N_HEADS)


KEY_AUG = LANES
VAL_AUG = 80
N_BIAS = 3


def _attn_prompt_kernel(qt_ref, k_ref, vt_ref, lq1_ref, lk1_ref, lq2_ref, lk2_ref, subg_ref, o_ref, kaug, vaug, *,
                        tq, tk, nblk, lam_init):
    i = pl.program_id(1)
    hd = 2 * DK
    diag_blocks = tq // tk

    @pl.when(i == 0)
    def _():
        key = lax.broadcasted_iota(jnp.int32, (tk, KEY_AUG), 0).astype(F32)
        lane = lax.broadcasted_iota(jnp.int32, (tk, KEY_AUG), 1)
        ones_row = jnp.where(lax.broadcasted_iota(jnp.int32, (VAL_AUG - hd, tk), 0) == 0, 1.0, 0.0).astype(BF16)
        for h in range(N_HEADS):
            bias = (_alibi_slope(h) * LOG2E) * key
            hi = bias.astype(BF16).astype(F32)
            mid = (bias - hi).astype(BF16).astype(F32)
            lo = ((bias - hi) - mid).astype(BF16).astype(F32)
            aug = jnp.where(lane == hd, hi, jnp.where(lane == hd + 1, mid, jnp.where(lane == hd + 2, lo, 0.0)))
            for j in range(nblk):
                kblk = k_ref[j * tk:(j + 1) * tk, h * hd:(h + 1) * hd]
                kpad = jnp.concatenate([kblk, jnp.zeros((tk, KEY_AUG - hd), F32)], axis=1)
                kaug[j, h] = jnp.where(lane < hd, kpad, aug).astype(BF16)
                vaug[j, h, 0:hd, :] = vt_ref[0, 0, h * hd:(h + 1) * hd, j * tk:(j + 1) * tk].astype(BF16)
                vaug[j, h, hd:VAL_AUG, :] = ones_row

    lam = _diff_lambda(lq1_ref[0], lk1_ref[0], lq2_ref[0], lk2_ref[0], lam_init)
    qt = qt_ref[0]
    zero = jnp.zeros((DK, tq), BF16)
    ones_aug = jnp.where(lax.broadcasted_iota(jnp.int32, (KEY_AUG - hd, 2 * tq), 0) < N_BIAS, 1.0, 0.0).astype(BF16)
    qs = []
    for h in range(N_HEADS):
        q1 = qt[h * hd:h * hd + DK]
        q2 = qt[h * hd + DK:(h + 1) * hd]
        body = jnp.concatenate([jnp.concatenate([q1, zero], axis=0), jnp.concatenate([zero, q2], axis=0)], axis=1)
        qs.append(jnp.concatenate([body, ones_aug], axis=0))
    key = lax.broadcasted_iota(jnp.int32, (tk, 2 * tq), 0)
    qry = lax.broadcasted_iota(jnp.int32, (tk, 2 * tq), 1)
    rel = key - jnp.where(qry >= tq, qry - tq, qry)

    def blocks(js, carry, diags):
        scores = [[jnp.dot(kaug[j, h], qs[h], preferred_element_type=F32) for h in range(N_HEADS)] for j in js]
        stats = [carry[h][0] for h in range(N_HEADS)]
        probs = []
        for n, j in enumerate(js):
            row = []
            for h in range(N_HEADS):
                s = scores[n][h] if diags[n] is None else jnp.where(rel <= -diags[n] * tk, scores[n][h], -jnp.inf)
                c = (_alibi_slope(h) * LOG2E) * (j * tk - i * tq).astype(F32)
                m = stats[h]
                m_new = jnp.maximum(m, jnp.max(s, axis=0, keepdims=True) + c)
                row.append((jnp.exp2(m - m_new), jnp.exp2(s - (m_new - c)).astype(BF16)))
                stats[h] = m_new
            probs.append(row)
        accs = [carry[h][1] for h in range(N_HEADS)]
        for n, j in enumerate(js):
            for h in range(N_HEADS):
                alpha, p = probs[n][h]
                accs[h] = alpha * accs[h] + jnp.dot(vaug[j, h], p, preferred_element_type=F32)
        return tuple((stats[h], accs[h]) for h in range(N_HEADS))

    init = tuple((jnp.full((1, 2 * tq), -jnp.inf, F32), jnp.zeros((VAL_AUG, 2 * tq), F32)) for _ in range(N_HEADS))
    pair = lambda jj, carry: blocks([jj * diag_blocks + d for d in range(diag_blocks)], carry, [None] * diag_blocks)
    carry = lax.fori_loop(0, i, pair, init)
    carry = blocks([i * diag_blocks + d for d in range(diag_blocks)], carry, list(range(diag_blocks)))
    ys = []
    for h in range(N_HEADS):
        acc = carry[h][1]
        o = acc[0:hd, :] * (1.0 / acc[hd:hd + 1, :])
        od = o[:, 0:tq] - lam * o[:, tq:2 * tq]
        ms = jnp.mean(od * od, axis=0, keepdims=True)
        ys.append(od * lax.rsqrt(ms + LN_EPS) * subg_ref[0] * (1.0 - lam_init))
    o_ref[...] = jnp.concatenate(ys, axis=0).T.astype(BF16)


def _attn_prompt(qt, k_rows, vt_all, wts, layer, *, batch, seq, lam_init):
    tq, tk = ATT_Q_TILE, ATT_K_TILE
    ns = seq // tq
    nk = seq // tk
    lvec = _layer_spec((1, DK), layer)
    return pl.pallas_call(
        functools.partial(_attn_prompt_kernel, tq=tq, tk=tk, nblk=nk, lam_init=lam_init),
        grid=(batch, ns),
        in_specs=[pl.BlockSpec((1, D_GROUP, tq), lambda b, i: (b, 0, i)),
                  pl.BlockSpec((seq, D_GROUP), lambda b, i: (b, 0)),
                  pl.BlockSpec((1, 1, D_GROUP, seq), lambda b, i: (layer, b, 0, 0)),
                  lvec, lvec, lvec, lvec, _layer_spec((DV, 1), layer)],
        out_specs=pl.BlockSpec((tq, D_GROUP), lambda b, i: (b * ns + i, 0)),
        out_shape=jax.ShapeDtypeStruct((batch * seq, D_GROUP), BF16),
        scratch_shapes=[pltpu.VMEM((nk, N_HEADS, tk, KEY_AUG), BF16), pltpu.VMEM((nk, N_HEADS, VAL_AUG, tk), BF16)],
        compiler_params=pltpu.CompilerParams(
            dimension_semantics=("arbitrary", "arbitrary"), vmem_limit_bytes=VMEM_LIMIT_BYTES),
        name="attn_prompt",
    )(qt, k_rows, vt_all, wts["lambda_q1"], wts["lambda_k1"], wts["lambda_q2"], wts["lambda_k2"],
      wts["attn_sub_g_col"])


def _decode_rows(t_new):
    n_rows = N_HEADS * 2 * t_new
    row1 = lax.broadcasted_iota(jnp.int32, (n_rows, 1), 0)
    t_row = (row1 % t_new).astype(F32)
    slope_row = jnp.zeros((n_rows, 1), F32)
    for h in range(N_HEADS):
        slope_row = jnp.where(row1 // (2 * t_new) == h, _alibi_slope(h), slope_row)
    return n_rows, t_row, slope_row


def _decode_queries(q4, t_new):
    n_rows = N_HEADS * 2 * t_new
    qt = jnp.concatenate([q4 * Q_SCALE] * (2 * N_HEADS), axis=0)
    r = lax.broadcasted_iota(jnp.int32, (n_rows, D_GROUP), 0)
    c = lax.broadcasted_iota(jnp.int32, (n_rows, D_GROUP), 1)
    return jnp.where(c // DK == r // t_new, qt, 0.0).astype(BF16)


def _decode_scores(qbd, k_pages, state, first_pos, q_pos, slope_row):
    m_prev, l_prev, _ = state
    s = jnp.concatenate([jnp.dot(qbd, kp[...].astype(BF16), preferred_element_type=F32) for kp in k_pages], axis=1)
    kpos = (first_pos + lax.broadcasted_iota(jnp.int32, s.shape, 1)).astype(F32)
    s = s - slope_row * (q_pos - kpos)
    m_new = jnp.maximum(m_prev, jnp.max(s, axis=-1, keepdims=True))
    alpha = jnp.exp(m_prev - m_new)
    p = jnp.exp(s - m_new)
    l_new = alpha * l_prev + jnp.sum(p, axis=-1, keepdims=True)
    return m_new, l_new, alpha, p.astype(BF16)


def _decode_values(pb, alpha, v_pages, state, page):
    acc_prev = state[2]
    pv = jnp.zeros(acc_prev.shape, F32)
    for idx, vp in enumerate(v_pages):
        pv = pv + lax.dot_general(pb[:, idx * page:(idx + 1) * page], vp[...].astype(BF16), NT_DIMS,
                                  preferred_element_type=F32)
    return alpha * acc_prev + pv


def _decode_finish(qbd, state, kn, vn, lam, subg, t_row, slope_row, t_new, lam_init):
    m_prev, l_prev, acc_prev = state
    n_rows = qbd.shape[0]
    kn8 = jnp.concatenate([kn, jnp.zeros((SUBLANES - t_new, D_GROUP), F32)], axis=0).astype(BF16)
    sn = lax.dot_general(qbd, kn8, NT_DIMS, preferred_element_type=F32)
    dist = t_row - lax.broadcasted_iota(jnp.int32, (n_rows, SUBLANES), 1).astype(F32)
    sn = jnp.where(dist >= 0.0, sn - slope_row * dist, -jnp.inf)
    m_fin = jnp.maximum(m_prev, jnp.max(sn, axis=-1, keepdims=True))
    alpha = jnp.exp(m_prev - m_fin)
    pn = jnp.exp(sn - m_fin)
    l_fin = alpha * l_prev + jnp.sum(pn, axis=-1, keepdims=True)
    acc = alpha * acc_prev
    for t in range(t_new):
        acc = acc + pn[:, t:t + 1] * vn[t:t + 1, :]
    o = acc / l_fin
    r = lax.broadcasted_iota(jnp.int32, (n_rows, D_GROUP), 0)
    c = lax.broadcasted_iota(jnp.int32, (n_rows, D_GROUP), 1)
    o = jnp.where(c // DV == r // (2 * t_new), o, 0.0)
    y0 = jnp.zeros((t_new, D_GROUP), F32)
    y1 = jnp.zeros((t_new, D_GROUP), F32)
    for h in range(N_HEADS):
        y0 = y0 + o[h * 2 * t_new:h * 2 * t_new + t_new]
        y1 = y1 + o[h * 2 * t_new + t_new:(h + 1) * 2 * t_new]
    od = y0 - lam * y1
    seg_r = lax.broadcasted_iota(jnp.int32, (D_GROUP, D_GROUP), 0) // DV
    seg_c = lax.broadcasted_iota(jnp.int32, (D_GROUP, D_GROUP), 1) // DV
    seg = jnp.where(seg_r == seg_c, 1.0 / DV, 0.0).astype(F32)
    ms = jnp.dot(od * od, seg, preferred_element_type=F32, precision=lax.Precision.HIGHEST)
    return od * lax.rsqrt(ms + LN_EPS) * subg * (1.0 - lam_init)


def _post_head(x_ref, mix_ref, yc_ref, wo_ref, gpost_ref, gfpre_ref, hn_s):
    ab, cd = 2 * D_GROUP, 3 * D_GROUP
    mixo = jnp.dot(mix_ref[:, 0:ab], wo_ref[0, 0:ab, :], preferred_element_type=F32)
    mixo = mixo + jnp.dot(yc_ref[...].astype(BF16), wo_ref[0, ab:cd, :], preferred_element_type=F32)
    mixo = mixo + jnp.dot(mix_ref[:, ab:cd], wo_ref[0, cd:D_MODEL, :], preferred_element_type=F32)
    x1 = x_ref[...] + _rms(mixo, gpost_ref[0], RMS_EPS)
    hn_s[...] = _rms(x1, gfpre_ref[0], RMS_EPS).astype(BF16)
    return x1


def _ffn_chunk(c, hn_s, wgu_ref, act_s):
    lo = c * FF_CHUNK
    gate = jnp.dot(hn_s[...], wgu_ref[0, :, lo:lo + FF_CHUNK], preferred_element_type=F32)
    up = jnp.dot(hn_s[...], wgu_ref[0, :, D_FF + lo:D_FF + lo + FF_CHUNK], preferred_element_type=F32)
    act_s[:, lo:lo + FF_CHUNK] = (gate * _sigmoid(gate) * up).astype(BF16)


def _post_tail(x1, act_s, wd_ref, gfpost_ref, o_ref):
    ffn = jnp.dot(act_s[...], wd_ref[0], preferred_element_type=F32)
    o_ref[...] = x1 + _rms(ffn, gfpost_ref[0], RMS_EPS)


def _post_specs(tm, layer, row):
    vec = _layer_spec((1, D_MODEL), layer)
    return [pl.BlockSpec((tm, D_MODEL), row),
            pl.BlockSpec((tm, 3 * D_GROUP), row),
            pl.BlockSpec((tm, D_GROUP), row),
            _layer_spec_single((D_MODEL, D_MODEL), layer), vec, vec,
            _layer_spec_single((D_MODEL, 2 * D_FF), layer),
            _layer_spec_single((D_FF, D_MODEL), layer), vec]


def _post_args(x2d, mix3, yc, wts):
    return (x2d, mix3, yc, wts["w_out"], wts["g_mix_post"], wts["g_ffn_pre"], wts["w_gu"], wts["w_down"],
            wts["g_ffn_post"])


def _post_attn_kernel(pt_ref, x_ref, mix_ref, yc_ref, wo_ref, gpost_ref, gfpre_ref, wgu_ref, wd_ref, gfpost_ref,
                      q_ref, kn_ref, vn_ref, lq1_ref, lk1_ref, lq2_ref, lk2_ref, subg_ref, xs_ref, mixs_ref, kc_hbm,
                      vc_hbm, o_ref, oa_ref, xso_ref, hn_s, act_s, kbuf, vbuf, sems, *, layer, db, seqs_per_step,
                      t_new, page, n_chunks, chunk_pages, past_len, lam_init):
    i = pl.program_id(0)
    regions = seqs_per_step * n_chunks
    n_rows, t_row, slope_row = _decode_rows(t_new)

    def page_copies(bb, cc, sl, lookup):
        copies = []
        for idx in range(chunk_pages):
            pid = pt_ref[bb, cc * chunk_pages + idx] if lookup else 0
            copies.append(pltpu.make_async_copy(kc_hbm.at[layer, pid], kbuf.at[sl, idx], sems.at[0, sl]))
            copies.append(pltpu.make_async_copy(vc_hbm.at[layer, pid], vbuf.at[sl, idx], sems.at[1, sl]))
        return copies

    def gather_rows(ref, bb):
        return jnp.concatenate([ref[pl.ds(t * db + bb, 1), :] for t in range(t_new)], axis=0)

    @pl.when(i == 0)
    def _():
        for cp in page_copies(0, 0, 0, True):
            cp.start()

    lam = _diff_lambda(lq1_ref[0], lk1_ref[0], lq2_ref[0], lk2_ref[0], lam_init)
    ffn_order = [list(range(N_FF_CHUNKS))[r::regions] for r in range(regions)]
    x1 = None

    for r in range(seqs_per_step):
        b = i * seqs_per_step + r
        qbd = _decode_queries(gather_rows(q_ref, b), t_new)
        state = (jnp.full((n_rows, 1), -jnp.inf, F32), jnp.zeros((n_rows, 1), F32),
                 jnp.zeros((n_rows, D_GROUP), F32))
        for c in range(n_chunks):
            region = r * n_chunks + c
            slot = lax.rem(i * regions + region, 2)
            if c + 1 < n_chunks:
                for cp in page_copies(b, c + 1, 1 - slot, True):
                    cp.start()
            else:
                @pl.when(b + 1 < db)
                def _():
                    for cp in page_copies(b + 1, 0, 1 - slot, True):
                        cp.start()
            for cp in page_copies(b, c, slot, False):
                cp.wait()
            k_pages = [kbuf.at[slot, idx] for idx in range(chunk_pages)]
            v_pages = [vbuf.at[slot, idx] for idx in range(chunk_pages)]
            m_new, l_new, alpha, pb = _decode_scores(qbd, k_pages, state, c * chunk_pages * page, past_len + t_row,
                                                     slope_row)
            if region == 0:
                x1 = _post_head(x_ref, mix_ref, yc_ref, wo_ref, gpost_ref, gfpre_ref, hn_s)
            state = (m_new, l_new, _decode_values(pb, alpha, v_pages, state, page))
            for f in ffn_order[region]:
                _ffn_chunk(f, hn_s, wgu_ref, act_s)
        y = _decode_finish(qbd, state, gather_rows(kn_ref, b), gather_rows(vn_ref, b), lam, subg_ref[0], t_row,
                           slope_row, t_new, lam_init)
        for t in range(t_new):
            oa_ref[pl.ds(t * db + b, 1), :] = y[t:t + 1, :]

    _post_tail(x1, act_s, wd_ref, gfpost_ref, o_ref)

    @pl.when(i == pl.num_programs(0) - 1)
    def _():
        n_s = xs_ref.shape[0]
        hn_v, act_v = hn_s.at[pl.ds(0, n_s)], act_s.at[pl.ds(0, n_s)]
        x1s = _post_head(xs_ref, mixs_ref, oa_ref, wo_ref, gpost_ref, gfpre_ref, hn_v)
        for f in range(N_FF_CHUNKS):
            _ffn_chunk(f, hn_v, wgu_ref, act_v)
        _post_tail(x1s, act_v, wd_ref, gfpost_ref, xso_ref)


def _post_attn(x2d, mix3, yc, xs, mix3_s, page_table, q, kn, vn, cache_kt, cache_vt, wts, layer, *, tm, db, t_new,
               lam_init):
    m = x2d.shape[0]
    n_steps = m // tm
    n_pages = page_table.shape[1]
    page = cache_kt.shape[-1]
    chunk_pages = min(DECODE_CHUNK_PAGES, n_pages)
    ms = db * t_new
    assert db % n_steps == 0 and n_pages % chunk_pages == 0 and ms <= tm
    row = lambda i, pt: (i, 0)
    full = lambda i, pt: (0, 0)
    lvec = _layer_spec((1, DK), layer)
    in_specs = _post_specs(tm, layer, row)
    in_specs += [pl.BlockSpec((ms, D_GROUP), full)] * 3 + [lvec] * 4 + [_layer_spec((1, D_GROUP), layer)]
    in_specs += [pl.BlockSpec((ms, D_MODEL), full), pl.BlockSpec((ms, 3 * D_GROUP), full)]
    in_specs += [pl.BlockSpec(memory_space=pl.ANY)] * 2
    page_buf = pltpu.VMEM((2, chunk_pages, D_GROUP, page), F32)
    grid_spec = pltpu.PrefetchScalarGridSpec(
        num_scalar_prefetch=1,
        grid=(n_steps,),
        in_specs=in_specs,
        out_specs=[pl.BlockSpec((tm, D_MODEL), row), pl.BlockSpec((ms, D_GROUP), full),
                   pl.BlockSpec((ms, D_MODEL), full)],
        scratch_shapes=[pltpu.VMEM((tm, D_MODEL), BF16), pltpu.VMEM((tm, D_FF), BF16), page_buf, page_buf,
                        pltpu.SemaphoreType.DMA((2, 2))],
    )
    x_new, _, xs_new = pl.pallas_call(
        functools.partial(_post_attn_kernel, layer=layer, db=db, seqs_per_step=db // n_steps, t_new=t_new, page=page,
                          n_chunks=n_pages // chunk_pages, chunk_pages=chunk_pages, past_len=n_pages * page,
                          lam_init=lam_init),
        grid_spec=grid_spec,
        out_shape=[jax.ShapeDtypeStruct((m, D_MODEL), F32), jax.ShapeDtypeStruct((ms, D_GROUP), F32),
                   jax.ShapeDtypeStruct((ms, D_MODEL), F32)],
        compiler_params=pltpu.CompilerParams(
            dimension_semantics=("arbitrary",), vmem_limit_bytes=VMEM_LIMIT_BYTES),
        name="post_attn",
    )(page_table, *_post_args(x2d, mix3, yc, wts), q, kn, vn, wts["lambda_q1"], wts["lambda_k1"], wts["lambda_q2"],
      wts["lambda_k2"], wts["attn_sub_g4"], xs, mix3_s, cache_kt, cache_vt)
    return x_new, xs_new


def _prepare_weights(w_in, w_out, conv_w, conv_b, conv_ln_g, conv_ln_b, sgu_w, sgu_b, sgu_ln_g, sgu_ln_b,
                     lambda_q1, lambda_k1, lambda_q2, lambda_k2, attn_sub_g, pool_w, pool_scale, g_mix_pre,
                     g_mix_post, g_ffn_pre, g_ffn_post, w_gu, w_down, t_new):
    depth = w_in.shape[0]
    hd = D_GROUP // SGU_HEADS
    sgu_w4 = jnp.repeat(jnp.transpose(sgu_w[:, :, :t_new, :t_new], (0, 2, 3, 1)), hd, axis=3)
    eye = jnp.eye(len(POOL_WINDOWS), dtype=F32)
    row = lambda a: a[:, None, :]
    return {
        "w_in": w_in.astype(BF16), "w_out": w_out.astype(BF16), "w_gu": w_gu.astype(BF16),
        "w_down": w_down.astype(BF16),
        "conv_w": conv_w, "conv_b": row(conv_b), "conv_ln_g": row(conv_ln_g), "conv_ln_b": row(conv_ln_b),
        "sgu_w": sgu_w,
        "sgu_b_full": jnp.repeat(jnp.swapaxes(sgu_b, 1, 2), hd, axis=2),
        "sgu_w4": sgu_w4.reshape(depth, t_new * t_new, D_GROUP),
        "sgu_ln_g": row(sgu_ln_g), "sgu_ln_b": row(sgu_ln_b),
        "lambda_q1": row(lambda_q1), "lambda_k1": row(lambda_k1), "lambda_q2": row(lambda_q2),
        "lambda_k2": row(lambda_k2),
        "attn_sub_g_col": attn_sub_g[:, :, None], "attn_sub_g4": jnp.tile(row(attn_sub_g), (1, 1, N_HEADS)),
        "pool_w_bd": jnp.einsum("lgce,gh->lgche", pool_w, eye).reshape(depth, D_GROUP, D_GROUP).astype(BF16),
        "pool_scale": row(pool_scale),
        "g_mix_pre": row(g_mix_pre), "g_mix_post": row(g_mix_post), "g_ffn_pre": row(g_ffn_pre),
        "g_ffn_post": row(g_ffn_post),
    }


def kernel(x_prompt, x_sample, cache_k, cache_v, state_conv, state_pool, page_table, w_in, w_out, conv_w, conv_b, conv_ln_g, conv_ln_b, sgu_w, sgu_b, sgu_ln_g, sgu_ln_b, lambda_q1, lambda_k1, lambda_q2, lambda_k2, attn_sub_g, pool_w, pool_scale, g_mix_pre, g_mix_post, g_ffn_pre, g_ffn_post, w_gu, w_down):
    batch, seq, _ = x_prompt.shape
    db, t_new, _ = x_sample.shape
    depth, n_pool, page = cache_k.shape[0], cache_k.shape[1], cache_k.shape[2]
    past_len = page_table.shape[1] * page
    assert seq % ROW_TILE == 0 and seq % SEQ_TILE == 0 and db % SUBLANES == 0
    assert seq % ATT_Q_TILE == 0 and ATT_Q_TILE % ATT_K_TILE == 0
    assert page == LANES

    wts = _prepare_weights(w_in, w_out, conv_w, conv_b, conv_ln_g, conv_ln_b, sgu_w, sgu_b, sgu_ln_g, sgu_ln_b,
                           lambda_q1, lambda_k1, lambda_q2, lambda_k2, attn_sub_g, pool_w, pool_scale,
                           g_mix_pre, g_mix_post, g_ffn_pre, g_ffn_post, w_gu, w_down, t_new)

    cache_kt = jnp.transpose(cache_k, (0, 1, 3, 4, 5, 2)).reshape(depth, n_pool, D_GROUP, page)
    cache_vt = jnp.transpose(cache_v, (0, 1, 3, 4, 2)).reshape(depth, n_pool, D_GROUP, page)
    ms = db * t_new
    xs = jnp.swapaxes(x_sample, 0, 1).reshape(ms, D_MODEL)
    sc_tm = jnp.swapaxes(state_conv, 1, 2)
    sp_tm = jnp.swapaxes(state_pool, 1, 2)
    xp = x_prompt.reshape(batch * seq, D_MODEL)

    kv_all = None
    cp_l, pp_l = [], []
    ks_l, vs_l, cs_l, ps_l, gs_l = [], [], [], [], []
    for l in range(depth):
        lam_init = 0.8 - 0.6 * math.exp(-0.3 * l)
        mix3, k_rows, qt, kt_all, vt_all, conv_new, pool_new = _proj_mix_prompt(xp, wts, kv_all, l, batch=batch,
                                                                                seq=seq)
        kv_all = (kt_all, vt_all)
        yc = _attn_prompt(qt, k_rows, vt_all, wts, l, batch=batch, seq=seq, lam_init=lam_init)
        cp_l.append(conv_new); pp_l.append(pool_new)
        mix3_s, q, kn, vn, conv_new, pool_new, sgu_v = _proj_mix_sample(xs, sc_tm, sp_tm, wts, l, db=db, t_new=t_new,
                                                                        past_len=past_len)
        xp, xs = _post_attn(xp, mix3, yc, xs, mix3_s, page_table, q, kn, vn, cache_kt, cache_vt, wts, l, tm=ROW_TILE,
                            db=db, t_new=t_new, lam_init=lam_init)
        ks_l.append(kn); vs_l.append(vn); cs_l.append(conv_new); ps_l.append(pool_new); gs_l.append(sgu_v)

    kt_all, vt_all = kv_all
    y_prompt = xp.reshape(batch, seq, D_MODEL)
    y_sample = jnp.swapaxes(xs.reshape(t_new, db, D_MODEL), 0, 1)
    k_prompt = jnp.transpose(kt_all.reshape(depth, batch, N_HEADS, 2, DK, seq), (0, 1, 5, 2, 3, 4))
    v_prompt = jnp.transpose(vt_all.reshape(depth, batch, N_HEADS, DV, seq), (0, 1, 4, 2, 3))
    tm_to_bm = lambda a: jnp.swapaxes(jnp.stack(a).reshape(depth, t_new, db, D_GROUP), 1, 2)
    k_sample = tm_to_bm(ks_l).reshape(depth, db, t_new, N_HEADS, 2, DK)
    v_sample = tm_to_bm(vs_l).reshape(depth, db, t_new, N_HEADS, DV)
    conv_sample = jnp.swapaxes(jnp.stack(cs_l), 1, 2)
    pool_sample = jnp.swapaxes(jnp.stack(ps_l), 1, 2)
    sgu_v_sample = tm_to_bm(gs_l)
    return (y_prompt, y_sample, k_prompt, v_prompt, jnp.stack(cp_l), jnp.stack(pp_l),
            k_sample, v_sample, conv_sample, pool_sample, sgu_v_sample)
```

```python
import functools
import math

import jax
import jax.numpy as jnp
from jax import lax
from jax.experimental import pallas as pl
from jax.experimental.pallas import tpu as pltpu

F32 = jnp.float32
BF16 = jnp.bfloat16

D_MODEL = 1024
D_GROUP = 256
D_IN = 8 * D_GROUP
CONV_W = 31
CONV_HALO = CONV_W - 1
SGU_CHUNK = 128
SGU_HEADS = 4
N_HEADS = 4
DK = 32
DV = 64
POOL_WINDOWS = (2, 4, 8, 16)
POOL_HALO = max(POOL_WINDOWS) - 1
POOL_GD = D_GROUP // len(POOL_WINDOWS)
D_FF = 2816
FF_CHUNK = 256
N_FF_CHUNKS = D_FF // FF_CHUNK
RMS_EPS = 1e-6
LN_EPS = 1e-5
Q_SCALE = DK ** -0.5
LOG2E = math.log2(math.e)

SUBLANES = 8
LANES = 128
VMEM_LIMIT_BYTES = 56 * 1024 * 1024

ROW_TILE = 512
SEQ_TILE = 512
CONV_ROWS = 64
ATT_Q_TILE = 512
ATT_K_TILE = 256
DECODE_CHUNK_PAGES = 32

NT_DIMS = (((1,), (1,)), ((), ()))


def _rms(x, g, eps):
    return x * lax.rsqrt(jnp.mean(x * x, axis=-1, keepdims=True) + eps) * g


def _layer_norm(x, g, b):
    mu = jnp.mean(x, axis=-1, keepdims=True)
    xc = x - mu
    var = jnp.mean(xc * xc, axis=-1, keepdims=True)
    return xc * lax.rsqrt(var + LN_EPS) * g + b


def _gelu(x):
    return 0.5 * x * (1.0 + lax.erf(x * (2.0 ** -0.5)))


def _sigmoid(x):
    return 1.0 / (1.0 + jnp.exp(-x))


def _lane_group_select(pieces, group_width, shape):
    lane = lax.broadcasted_iota(jnp.int32, shape, len(shape) - 1)
    out = pieces[-1]
    for g in range(len(pieces) - 2, -1, -1):
        out = jnp.where(lane < (g + 1) * group_width, pieces[g], out)
    return out


def _layer_spec(shape, layer):
    zeros = (0,) * len(shape)
    return pl.BlockSpec((1,) + tuple(shape), lambda *_: (layer,) + zeros)


def _layer_spec_single(shape, layer):
    zeros = (0,) * len(shape)
    return pl.BlockSpec((1,) + tuple(shape), lambda *_: (layer,) + zeros, pipeline_mode=pl.Buffered(1))


AG_COL, UV_COL, Q_COL, K_COL, V_COL, XD_COL = (0, 2 * D_GROUP, 4 * D_GROUP, 5 * D_GROUP, 6 * D_GROUP, 7 * D_GROUP)


def _conv_post(acc, clg, clb):
    y = _layer_norm(acc, clg, clb)
    return y * _sigmoid(y)


def _pool_select(wins, shape):
    return _lane_group_select(wins, POOL_GD, shape)


HPAD = 32
XPAD = 24


def _proj_mix_kernel(x_ref, g_ref, w_ref, cw_ref, cb_ref, clg_ref, clb_ref, sw_ref, sb_ref, slg_ref, slb_ref,
                     pw_ref, ps_ref, *rest, ts, n_alias):
    (mix_ref, k_ref, qt_ref, kt_ref, vt_ref, convn_ref, pooln_ref, hext, hshift, xext, s2, s4, s8) = rest[n_alias:]
    i = pl.program_id(1)

    @pl.when(i == 0)
    def _():
        hext[0:HPAD, :] = jnp.zeros((HPAD, D_GROUP), F32)
        xext[0:XPAD, :] = jnp.zeros((XPAD, D_GROUP), F32)
        for buf in (s2, s4, s8):
            buf[0:SUBLANES, :] = jnp.zeros((SUBLANES, D_GROUP), F32)

    hn = _rms(x_ref[...], g_ref[0], RMS_EPS).astype(BF16)
    proj = lambda col, width: jnp.dot(hn, w_ref[0, :, col:col + width], preferred_element_type=F32)
    z_ag = proj(AG_COL, 2 * D_GROUP)
    z_uv = proj(UV_COL, 2 * D_GROUP)
    z_qkv = proj(Q_COL, 3 * D_GROUP)
    z_xd = proj(XD_COL, D_GROUP)

    k_rows = z_qkv[:, D_GROUP:2 * D_GROUP]
    k_ref[...] = k_rows
    qt_ref[0] = (z_qkv[:, 0:D_GROUP] * (Q_SCALE * LOG2E)).T.astype(BF16)
    kt_ref[0, 0] = k_rows.T
    vt_ref[0, 0] = z_qkv[:, 2 * D_GROUP:3 * D_GROUP].T

    hext[HPAD:HPAD + ts, :] = z_ag[:, 0:D_GROUP] * _sigmoid(z_ag[:, D_GROUP:2 * D_GROUP])
    slab = ts + HPAD - SUBLANES
    for rho in range(1, SUBLANES):
        hshift[rho - 1] = hext[rho:rho + slab, :]
    for c in range(ts // CONV_ROWS):
        r0 = c * CONV_ROWS
        acc = jnp.broadcast_to(cb_ref[0], (CONV_ROWS, D_GROUP))
        for j in range(CONV_W):
            shift = HPAD - CONV_HALO + j
            q8, rho = shift // SUBLANES * SUBLANES, shift % SUBLANES
            if rho == 0:
                rows = hext[r0 + q8:r0 + q8 + CONV_ROWS, :]
            else:
                rows = hshift[rho - 1, r0 + q8:r0 + q8 + CONV_ROWS, :]
            acc = acc + cw_ref[0, j:j + 1, :] * rows
        mix_ref[r0:r0 + CONV_ROWS, 0:D_GROUP] = _conv_post(acc, clg_ref[0], clb_ref[0]).astype(BF16)

    u = _gelu(z_uv[:, 0:D_GROUP])
    v = _layer_norm(_gelu(z_uv[:, D_GROUP:2 * D_GROUP]), slg_ref[0], slb_ref[0])
    tri_r = lax.broadcasted_iota(jnp.int32, (SGU_CHUNK, SGU_CHUNK), 0)
    tri_c = lax.broadcasted_iota(jnp.int32, (SGU_CHUNK, SGU_CHUNK), 1)
    w_heads = [jnp.where(tri_c <= tri_r, sw_ref[0, g], 0.0).astype(BF16) for g in range(SGU_HEADS)]
    w_stack = jnp.concatenate(w_heads, axis=0)
    hd = D_GROUP // SGU_HEADS
    for c in range(ts // SGU_CHUNK):
        rows = slice(c * SGU_CHUNK, (c + 1) * SGU_CHUNK)
        r = jnp.dot(w_stack, v[rows].astype(BF16), preferred_element_type=F32)
        mixed = _lane_group_select([r[g * SGU_CHUNK:(g + 1) * SGU_CHUNK] for g in range(SGU_HEADS)],
                                   hd, (SGU_CHUNK, D_GROUP))
        mix_ref[rows, D_GROUP:2 * D_GROUP] = (u[rows] * (mixed + sb_ref[0])).astype(BF16)

    n = XPAD + ts
    xext[XPAD:n, :] = z_xd
    s2[SUBLANES:n, :] = xext[SUBLANES:n, :] + xext[SUBLANES - 1:n - 1, :]
    s4[SUBLANES:n, :] = s2[SUBLANES:n, :] + s2[SUBLANES - 2:n - 2, :]
    s8[2 * SUBLANES:n, :] = s4[2 * SUBLANES:n, :] + s4[2 * SUBLANES - 4:n - 4, :]
    win16 = s8[XPAD:n, :] + s8[XPAD - 8:n - 8, :]
    x0 = xext[XPAD:n, :]
    wins = _pool_select([s2[XPAD:n, :], s4[XPAD:n, :], s8[XPAD:n, :], win16], (ts, D_GROUP))
    wvec = _pool_select([jnp.full((1, D_GROUP), float(w), F32) for w in POOL_WINDOWS], (1, D_GROUP))
    pos = (i * ts + lax.broadcasted_iota(jnp.int32, (ts, D_GROUP), 0)).astype(F32)
    d = wins / jnp.minimum(pos + 1.0, wvec) - x0
    y = jnp.dot(d.astype(BF16), pw_ref[0], preferred_element_type=F32) * ps_ref[0]
    mix_ref[:, 2 * D_GROUP:3 * D_GROUP] = y.astype(BF16)

    hext[0:HPAD, :] = hext[ts:ts + HPAD, :]
    xext[SUBLANES:XPAD, :] = xext[ts + SUBLANES:ts + XPAD, :]

    @pl.when(i == pl.num_programs(1) - 1)
    def _():
        convn_ref[0] = hext[HPAD - CONV_HALO:HPAD, :]
        pooln_ref[0] = xext[XPAD - POOL_HALO:XPAD, :]


def _proj_mix_prompt(x2d, wts, kv_prev, layer, *, batch, seq):
    ts = SEQ_TILE
    ns = seq // ts
    m = batch * seq
    depth = wts["w_in"].shape[0]
    row_map = lambda b, i: (b * ns + i, 0)
    vec = _layer_spec((1, D_GROUP), layer)
    ext = lambda pad: pltpu.VMEM((pad + ts, D_GROUP), F32)
    in_specs = [pl.BlockSpec((ts, D_MODEL), row_map), _layer_spec((1, D_MODEL), layer),
                _layer_spec_single((D_MODEL, D_IN), layer),
                _layer_spec((CONV_W, D_GROUP), layer), vec, vec, vec,
                _layer_spec((SGU_HEADS, SGU_CHUNK, SGU_CHUNK), layer),
                _layer_spec((SGU_CHUNK, D_GROUP), layer), vec, vec,
                _layer_spec((D_GROUP, D_GROUP), layer), vec]
    args = [x2d, wts["g_mix_pre"], wts["w_in"], wts["conv_w"], wts["conv_b"], wts["conv_ln_g"], wts["conv_ln_b"],
            wts["sgu_w"], wts["sgu_b_full"], wts["sgu_ln_g"], wts["sgu_ln_b"], wts["pool_w_bd"], wts["pool_scale"]]
    kv_spec = pl.BlockSpec((1, 1, D_GROUP, ts), lambda b, i: (layer, b, 0, i))
    kv_shape = jax.ShapeDtypeStruct((depth, batch, D_GROUP, seq), F32)
    out_specs = [pl.BlockSpec((ts, 3 * D_GROUP), row_map), pl.BlockSpec((ts, D_GROUP), row_map),
                 pl.BlockSpec((1, D_GROUP, ts), lambda b, i: (b, 0, i)), kv_spec, kv_spec,
                 pl.BlockSpec((1, CONV_HALO, D_GROUP), lambda b, i: (b, 0, 0)),
                 pl.BlockSpec((1, POOL_HALO, D_GROUP), lambda b, i: (b, 0, 0))]
    out_shape = [jax.ShapeDtypeStruct((m, 3 * D_GROUP), BF16), jax.ShapeDtypeStruct((m, D_GROUP), F32),
                 jax.ShapeDtypeStruct((batch, D_GROUP, seq), BF16), kv_shape, kv_shape,
                 jax.ShapeDtypeStruct((batch, CONV_HALO, D_GROUP), F32),
                 jax.ShapeDtypeStruct((batch, POOL_HALO, D_GROUP), F32)]
    kv_out_index = 3
    aliases = {}
    n_alias = 0
    if kv_prev is not None:
        n_alias = len(kv_prev)
        for a, buf in enumerate(kv_prev):
            aliases[len(args)] = kv_out_index + a
            in_specs.append(pl.BlockSpec(memory_space=pl.ANY))
            args.append(buf)
    return pl.pallas_call(
        functools.partial(_proj_mix_kernel, ts=ts, n_alias=n_alias),
        grid=(batch, ns),
        in_specs=in_specs,
        out_specs=out_specs,
        out_shape=out_shape,
        input_output_aliases=aliases,
        scratch_shapes=[ext(HPAD), pltpu.VMEM((SUBLANES - 1, ts + HPAD - SUBLANES, D_GROUP), F32),
                        ext(XPAD), ext(XPAD), ext(XPAD), ext(XPAD)],
        compiler_params=pltpu.CompilerParams(
            dimension_semantics=("arbitrary", "arbitrary"), vmem_limit_bytes=VMEM_LIMIT_BYTES),
        name="proj_mix_prompt",
    )(*args)


def _proj_mix_sample_kernel(x_ref, g_ref, w_ref, sc_ref, sp_ref, cw_ref, cb_ref, clg_ref, clb_ref, sw4_ref, sb_ref,
                            slg_ref, slb_ref, pw_ref, ps_ref, mix_ref, q_ref, k_ref, v_ref, convn_ref, pooln_ref,
                            sguv_ref, *, db, t_new, past_len):
    rows = lambda t: slice(t * db, (t + 1) * db)
    h = _rms(x_ref[...], g_ref[0], RMS_EPS).astype(BF16)
    z = jnp.dot(h, w_ref[0], preferred_element_type=F32)
    q_ref[...] = z[:, Q_COL:Q_COL + D_GROUP]
    k_ref[...] = z[:, K_COL:K_COL + D_GROUP]
    v_ref[...] = z[:, V_COL:V_COL + D_GROUP]

    ext = [sc_ref[0, j] for j in range(CONV_HALO)]
    ext += [z[rows(t), AG_COL:AG_COL + D_GROUP] * _sigmoid(z[rows(t), AG_COL + D_GROUP:AG_COL + 2 * D_GROUP])
            for t in range(t_new)]
    for t in range(t_new):
        acc = jnp.broadcast_to(cb_ref[0], (db, D_GROUP))
        for j in range(CONV_W):
            acc = acc + cw_ref[0, j:j + 1, :] * ext[t + j]
        mix_ref[rows(t), 0:D_GROUP] = _conv_post(acc, clg_ref[0], clb_ref[0]).astype(BF16)
    for j in range(CONV_HALO):
        convn_ref[j] = ext[t_new + j]

    u = _gelu(z[:, UV_COL:UV_COL + D_GROUP])
    v = _layer_norm(_gelu(z[:, UV_COL + D_GROUP:UV_COL + 2 * D_GROUP]), slg_ref[0], slb_ref[0])
    sguv_ref[...] = v
    for t in range(t_new):
        mixed = jnp.broadcast_to(sb_ref[0, t:t + 1, :], (db, D_GROUP))
        for s in range(t + 1):
            mixed = mixed + sw4_ref[0, t * t_new + s:t * t_new + s + 1, :] * v[rows(s)]
        mix_ref[rows(t), D_GROUP:2 * D_GROUP] = (u[rows(t)] * mixed).astype(BF16)

    pext = [sp_ref[0, j] for j in range(POOL_HALO)] + [z[rows(t), XD_COL:XD_COL + D_GROUP] for t in range(t_new)]
    ds = []
    for t in range(t_new):
        x0 = pext[POOL_HALO + t]
        acc = x0
        wins = []
        for k in range(1, POOL_HALO + 1):
            acc = acc + pext[POOL_HALO + t - k]
            if k + 1 in POOL_WINDOWS:
                wins.append(acc)
        cnts = [jnp.full((1, D_GROUP), float(min(past_len + t + 1, w)), F32) for w in POOL_WINDOWS]
        cnt = _pool_select(cnts, (1, D_GROUP))
        ds.append(_pool_select(wins, (db, D_GROUP)) / cnt - x0)
    d = jnp.concatenate(ds, axis=0)
    y = jnp.dot(d.astype(BF16), pw_ref[0], preferred_element_type=F32) * ps_ref[0]
    mix_ref[:, 2 * D_GROUP:3 * D_GROUP] = y.astype(BF16)
    for j in range(POOL_HALO):
        pooln_ref[j] = pext[t_new + j]


def _proj_mix_sample(xs, sc_tm, sp_tm, wts, layer, *, db, t_new, past_len):
    m = db * t_new
    full = lambda shape: pl.BlockSpec(shape, lambda i: (0,) * len(shape))
    vec = _layer_spec((1, D_GROUP), layer)
    act = jax.ShapeDtypeStruct((m, D_GROUP), F32)
    return pl.pallas_call(
        functools.partial(_proj_mix_sample_kernel, db=db, t_new=t_new, past_len=past_len),
        grid=(1,),
        in_specs=[full((m, D_MODEL)), _layer_spec((1, D_MODEL), layer), _layer_spec((D_MODEL, D_IN), layer),
                  _layer_spec((CONV_HALO, db, D_GROUP), layer), _layer_spec((POOL_HALO, db, D_GROUP), layer),
                  _layer_spec((CONV_W, D_GROUP), layer), vec, vec, vec,
                  _layer_spec((t_new * t_new, D_GROUP), layer),
                  _layer_spec((SGU_CHUNK, D_GROUP), layer), vec, vec,
                  _layer_spec((D_GROUP, D_GROUP), layer), vec],
        out_specs=[full((m, 3 * D_GROUP)), full((m, D_GROUP)), full((m, D_GROUP)), full((m, D_GROUP)),
                   full((CONV_HALO, db, D_GROUP)), full((POOL_HALO, db, D_GROUP)), full((m, D_GROUP))],
        out_shape=[jax.ShapeDtypeStruct((m, 3 * D_GROUP), BF16), act, act, act,
                   jax.ShapeDtypeStruct((CONV_HALO, db, D_GROUP), F32),
                   jax.ShapeDtypeStruct((POOL_HALO, db, D_GROUP), F32), act],
        compiler_params=pltpu.CompilerParams(
            dimension_semantics=("arbitrary",), vmem_limit_bytes=VMEM_LIMIT_BYTES),
        name="proj_mix_sample",
    )(xs, wts["g_mix_pre"], wts["w_in"], sc_tm, sp_tm, wts["conv_w"], wts["conv_b"], wts["conv_ln_g"],
      wts["conv_ln_b"], wts["sgu_w4"], wts["sgu_b_full"], wts["sgu_ln_g"], wts["sgu_ln_b"], wts["pool_w_bd"],
      wts["pool_scale"])


def _diff_lambda(lq1, lk1, lq2, lk2, lam_init):
    s1 = jnp.sum(lq1 * lk1, axis=-1, keepdims=True)
    s2 = jnp.sum(lq2 * lk2, axis=-1, keepdims=True)
    return jnp.exp(s1) - jnp.exp(s2) + lam_init


def _alibi_slope(h):
    return 2.0 ** (-8.0 * (h + 1) / N_HEADS)


KEY_AUG = LANES
VAL_AUG = 80
N_BIAS = 3


def _attn_prompt_kernel(qt_ref, k_ref, vt_ref, lq1_ref, lk1_ref, lq2_ref, lk2_ref, subg_ref, o_ref, kaug, vaug, *,
                        tq, tk, nblk, lam_init):
    i = pl.program_id(1)
    hd = 2 * DK
    diag_blocks = tq // tk

    @pl.when(i == 0)
    def _():
        key = lax.broadcasted_iota(jnp.int32, (tk, KEY_AUG), 0).astype(F32)
        lane = lax.broadcasted_iota(jnp.int32, (tk, KEY_AUG), 1)
        ones_row = jnp.where(lax.broadcasted_iota(jnp.int32, (VAL_AUG - hd, tk), 0) == 0, 1.0, 0.0).astype(BF16)
        for h in range(N_HEADS):
            bias = (_alibi_slope(h) * LOG2E) * key
            hi = bias.astype(BF16).astype(F32)
            mid = (bias - hi).astype(BF16).astype(F32)
            lo = ((bias - hi) - mid).astype(BF16).astype(F32)
            aug = jnp.where(lane == hd, hi, jnp.where(lane == hd + 1, mid, jnp.where(lane == hd + 2, lo, 0.0)))
            for j in range(nblk):
                kblk = k_ref[j * tk:(j + 1) * tk, h * hd:(h + 1) * hd]
                kpad = jnp.concatenate([kblk, jnp.zeros((tk, KEY_AUG - hd), F32)], axis=1)
                kaug[j, h] = jnp.where(lane < hd, kpad, aug).astype(BF16)
                vaug[j, h, 0:hd, :] = vt_ref[0, 0, h * hd:(h + 1) * hd, j * tk:(j + 1) * tk].astype(BF16)
                vaug[j, h, hd:VAL_AUG, :] = ones_row

    lam = _diff_lambda(lq1_ref[0], lk1_ref[0], lq2_ref[0], lk2_ref[0], lam_init)
    qt = qt_ref[0]
    zero = jnp.zeros((DK, tq), BF16)
    ones_aug = jnp.where(lax.broadcasted_iota(jnp.int32, (KEY_AUG - hd, 2 * tq), 0) < N_BIAS, 1.0, 0.0).astype(BF16)
    qs = []
    for h in range(N_HEADS):
        q1 = qt[h * hd:h * hd + DK]
        q2 = qt[h * hd + DK:(h + 1) * hd]
        body = jnp.concatenate([jnp.concatenate([q1, zero], axis=0), jnp.concatenate([zero, q2], axis=0)], axis=1)
        qs.append(jnp.concatenate([body, ones_aug], axis=0))
    lower = (lax.broadcasted_iota(jnp.int32, (tk, tk), 0) <= lax.broadcasted_iota(jnp.int32, (tk, tk), 1))

    def causal_mask(s, d):
        cols = []
        for cb in range(2 * diag_blocks):
            blk = s[:, cb * tk:(cb + 1) * tk]
            sub = cb % diag_blocks
            if sub == d:
                blk = jnp.where(lower, blk, -jnp.inf)
            elif sub < d:
                blk = jnp.full((tk, tk), -jnp.inf, F32)
            cols.append(blk)
        return jnp.concatenate(cols, axis=1)

    def blocks(js, carry, diags):
        scores = [[jnp.dot(kaug[j, h], qs[h], preferred_element_type=F32) for h in range(N_HEADS)] for j in js]
        stats = [carry[h][0] for h in range(N_HEADS)]
        probs = []
        for n, j in enumerate(js):
            row = []
            for h in range(N_HEADS):
                s = scores[n][h] if diags[n] is None else causal_mask(scores[n][h], diags[n])
                c = (_alibi_slope(h) * LOG2E) * (j * tk - i * tq).astype(F32)
                m = stats[h]
                m_new = jnp.maximum(m, jnp.max(s, axis=0, keepdims=True) + c)
                row.append((jnp.exp2(m - m_new), jnp.exp2(s - (m_new - c)).astype(BF16)))
                stats[h] = m_new
            probs.append(row)
        accs = [carry[h][1] for h in range(N_HEADS)]
        for n, j in enumerate(js):
            for h in range(N_HEADS):
                alpha, p = probs[n][h]
                accs[h] = alpha * accs[h] + jnp.dot(vaug[j, h], p, preferred_element_type=F32)
        return tuple((stats[h], accs[h]) for h in range(N_HEADS))

    init = tuple((jnp.full((1, 2 * tq), -jnp.inf, F32), jnp.zeros((VAL_AUG, 2 * tq), F32)) for _ in range(N_HEADS))
    pair = lambda jj, carry: blocks([jj * diag_blocks + d for d in range(diag_blocks)], carry, [None] * diag_blocks)
    carry = lax.fori_loop(0, i, pair, init)
    carry = blocks([i * diag_blocks + d for d in range(diag_blocks)], carry, list(range(diag_blocks)))
    ys = []
    for h in range(N_HEADS):
        acc = carry[h][1]
        o = acc[0:hd, :] * (1.0 / acc[hd:hd + 1, :])
        od = o[:, 0:tq] - lam * o[:, tq:2 * tq]
        ms = jnp.mean(od * od, axis=0, keepdims=True)
        ys.append(od * lax.rsqrt(ms + LN_EPS) * subg_ref[0] * (1.0 - lam_init))
    o_ref[...] = jnp.concatenate(ys, axis=0).T.astype(BF16)


def _attn_prompt(qt, k_rows, vt_all, wts, layer, *, batch, seq, lam_init):
    tq, tk = ATT_Q_TILE, ATT_K_TILE
    ns = seq // tq
    nk = seq // tk
    lvec = _layer_spec((1, DK), layer)
    return pl.pallas_call(
        functools.partial(_attn_prompt_kernel, tq=tq, tk=tk, nblk=nk, lam_init=lam_init),
        grid=(batch, ns),
        in_specs=[pl.BlockSpec((1, D_GROUP, tq), lambda b, i: (b, 0, i)),
                  pl.BlockSpec((seq, D_GROUP), lambda b, i: (b, 0)),
                  pl.BlockSpec((1, 1, D_GROUP, seq), lambda b, i: (layer, b, 0, 0)),
                  lvec, lvec, lvec, lvec, _layer_spec((DV, 1), layer)],
        out_specs=pl.BlockSpec((tq, D_GROUP), lambda b, i: (b * ns + i, 0)),
        out_shape=jax.ShapeDtypeStruct((batch * seq, D_GROUP), BF16),
        scratch_shapes=[pltpu.VMEM((nk, N_HEADS, tk, KEY_AUG), BF16), pltpu.VMEM((nk, N_HEADS, VAL_AUG, tk), BF16)],
        compiler_params=pltpu.CompilerParams(
            dimension_semantics=("arbitrary", "arbitrary"), vmem_limit_bytes=VMEM_LIMIT_BYTES),
        name="attn_prompt",
    )(qt, k_rows, vt_all, wts["lambda_q1"], wts["lambda_k1"], wts["lambda_q2"], wts["lambda_k2"],
      wts["attn_sub_g_col"])


def _decode_rows(t_new):
    n_rows = N_HEADS * 2 * t_new
    row1 = lax.broadcasted_iota(jnp.int32, (n_rows, 1), 0)
    t_row = (row1 % t_new).astype(F32)
    slope_row = jnp.zeros((n_rows, 1), F32)
    for h in range(N_HEADS):
        slope_row = jnp.where(row1 // (2 * t_new) == h, _alibi_slope(h), slope_row)
    return n_rows, t_row, slope_row


def _decode_queries(q4, t_new):
    n_rows = N_HEADS * 2 * t_new
    qt = jnp.concatenate([q4 * Q_SCALE] * (2 * N_HEADS), axis=0)
    r = lax.broadcasted_iota(jnp.int32, (n_rows, D_GROUP), 0)
    c = lax.broadcasted_iota(jnp.int32, (n_rows, D_GROUP), 1)
    return jnp.where(c // DK == r // t_new, qt, 0.0).astype(BF16)


def _decode_scores(qbd, k_pages, state, first_pos, q_pos, slope_row):
    m_prev, l_prev, _ = state
    s = jnp.concatenate([jnp.dot(qbd, kp[...].astype(BF16), preferred_element_type=F32) for kp in k_pages], axis=1)
    kpos = (first_pos + lax.broadcasted_iota(jnp.int32, s.shape, 1)).astype(F32)
    s = s - slope_row * (q_pos - kpos)
    m_new = jnp.maximum(m_prev, jnp.max(s, axis=-1, keepdims=True))
    alpha = jnp.exp(m_prev - m_new)
    p = jnp.exp(s - m_new)
    l_new = alpha * l_prev + jnp.sum(p, axis=-1, keepdims=True)
    return m_new, l_new, alpha, p.astype(BF16)


def _decode_values(pb, alpha, v_pages, state, page):
    acc_prev = state[2]
    pv = jnp.zeros(acc_prev.shape, F32)
    for idx, vp in enumerate(v_pages):
        pv = pv + jnp.dot(pb[:, idx * page:(idx + 1) * page], vp[...].T.astype(BF16), preferred_element_type=F32)
    return alpha * acc_prev + pv


def _decode_finish(qbd, state, kn, vn, lam, subg, t_row, slope_row, t_new, lam_init):
    m_prev, l_prev, acc_prev = state
    n_rows = qbd.shape[0]
    kn8 = jnp.concatenate([kn, jnp.zeros((SUBLANES - t_new, D_GROUP), F32)], axis=0).astype(BF16)
    sn = lax.dot_general(qbd, kn8, NT_DIMS, preferred_element_type=F32)
    dist = t_row - lax.broadcasted_iota(jnp.int32, (n_rows, SUBLANES), 1).astype(F32)
    sn = jnp.where(dist >= 0.0, sn - slope_row * dist, -jnp.inf)
    m_fin = jnp.maximum(m_prev, jnp.max(sn, axis=-1, keepdims=True))
    alpha = jnp.exp(m_prev - m_fin)
    pn = jnp.exp(sn - m_fin)
    l_fin = alpha * l_prev + jnp.sum(pn, axis=-1, keepdims=True)
    acc = alpha * acc_prev
    for t in range(t_new):
        acc = acc + pn[:, t:t + 1] * vn[t:t + 1, :]
    o = acc / l_fin
    r = lax.broadcasted_iota(jnp.int32, (n_rows, D_GROUP), 0)
    c = lax.broadcasted_iota(jnp.int32, (n_rows, D_GROUP), 1)
    o = jnp.where(c // DV == r // (2 * t_new), o, 0.0)
    y0 = jnp.zeros((t_new, D_GROUP), F32)
    y1 = jnp.zeros((t_new, D_GROUP), F32)
    for h in range(N_HEADS):
        y0 = y0 + o[h * 2 * t_new:h * 2 * t_new + t_new]
        y1 = y1 + o[h * 2 * t_new + t_new:(h + 1) * 2 * t_new]
    od = y0 - lam * y1
    seg_r = lax.broadcasted_iota(jnp.int32, (D_GROUP, D_GROUP), 0) // DV
    seg_c = lax.broadcasted_iota(jnp.int32, (D_GROUP, D_GROUP), 1) // DV
    seg = jnp.where(seg_r == seg_c, 1.0 / DV, 0.0).astype(F32)
    ms = jnp.dot(od * od, seg, preferred_element_type=F32, precision=lax.Precision.HIGHEST)
    return od * lax.rsqrt(ms + LN_EPS) * subg * (1.0 - lam_init)


def _post_head(x_ref, mix_ref, yc_ref, wo_ref, gpost_ref, gfpre_ref, hn_s):
    ab, cd = 2 * D_GROUP, 3 * D_GROUP
    mixo = jnp.dot(mix_ref[:, 0:ab], wo_ref[0, 0:ab, :], preferred_element_type=F32)
    mixo = mixo + jnp.dot(yc_ref[...].astype(BF16), wo_ref[0, ab:cd, :], preferred_element_type=F32)
    mixo = mixo + jnp.dot(mix_ref[:, ab:cd], wo_ref[0, cd:D_MODEL, :], preferred_element_type=F32)
    x1 = x_ref[...] + _rms(mixo, gpost_ref[0], RMS_EPS)
    hn_s[...] = _rms(x1, gfpre_ref[0], RMS_EPS).astype(BF16)
    return x1


def _ffn_chunk(c, hn_s, wgu_ref, act_s):
    lo = c * FF_CHUNK
    gate = jnp.dot(hn_s[...], wgu_ref[0, :, lo:lo + FF_CHUNK], preferred_element_type=F32)
    up = jnp.dot(hn_s[...], wgu_ref[0, :, D_FF + lo:D_FF + lo + FF_CHUNK], preferred_element_type=F32)
    act_s[:, lo:lo + FF_CHUNK] = (gate * _sigmoid(gate) * up).astype(BF16)


def _post_tail(x1, act_s, wd_ref, gfpost_ref, o_ref):
    ffn = jnp.dot(act_s[...], wd_ref[0], preferred_element_type=F32)
    o_ref[...] = x1 + _rms(ffn, gfpost_ref[0], RMS_EPS)


def _post_specs(tm, layer, row):
    vec = _layer_spec((1, D_MODEL), layer)
    return [pl.BlockSpec((tm, D_MODEL), row),
            pl.BlockSpec((tm, 3 * D_GROUP), row),
            pl.BlockSpec((tm, D_GROUP), row),
            _layer_spec_single((D_MODEL, D_MODEL), layer), vec, vec,
            _layer_spec_single((D_MODEL, 2 * D_FF), layer),
            _layer_spec_single((D_FF, D_MODEL), layer), vec]


def _post_args(x2d, mix3, yc, wts):
    return (x2d, mix3, yc, wts["w_out"], wts["g_mix_post"], wts["g_ffn_pre"], wts["w_gu"], wts["w_down"],
            wts["g_ffn_post"])


def _post_attn_kernel(pt_ref, x_ref, mix_ref, yc_ref, wo_ref, gpost_ref, gfpre_ref, wgu_ref, wd_ref, gfpost_ref,
                      q_ref, kn_ref, vn_ref, lq1_ref, lk1_ref, lq2_ref, lk2_ref, subg_ref, xs_ref, mixs_ref, kc_hbm,
                      vc_hbm, o_ref, oa_ref, xso_ref, hn_s, act_s, kbuf, vbuf, sems, *, layer, db, seqs_per_step,
                      t_new, page, n_chunks, chunk_pages, past_len, lam_init):
    i = pl.program_id(0)
    regions = seqs_per_step * n_chunks
    n_rows, t_row, slope_row = _decode_rows(t_new)

    def page_copies(bb, cc, sl, lookup):
        copies = []
        for idx in range(chunk_pages):
            pid = pt_ref[bb, cc * chunk_pages + idx] if lookup else 0
            copies.append(pltpu.make_async_copy(kc_hbm.at[layer, pid], kbuf.at[sl, idx], sems.at[0, sl]))
            copies.append(pltpu.make_async_copy(vc_hbm.at[layer, pid], vbuf.at[sl, idx], sems.at[1, sl]))
        return copies

    def gather_rows(ref, bb):
        return jnp.concatenate([ref[pl.ds(t * db + bb, 1), :] for t in range(t_new)], axis=0)

    @pl.when(i == 0)
    def _():
        for cp in page_copies(0, 0, 0, True):
            cp.start()

    lam = _diff_lambda(lq1_ref[0], lk1_ref[0], lq2_ref[0], lk2_ref[0], lam_init)
    ffn_order = [list(range(N_FF_CHUNKS))[r::regions] for r in range(regions)]
    x1 = None

    for r in range(seqs_per_step):
        b = i * seqs_per_step + r
        qbd = _decode_queries(gather_rows(q_ref, b), t_new)
        state = (jnp.full((n_rows, 1), -jnp.inf, F32), jnp.zeros((n_rows, 1), F32),
                 jnp.zeros((n_rows, D_GROUP), F32))
        for c in range(n_chunks):
            region = r * n_chunks + c
            slot = lax.rem(i * regions + region, 2)
            if c + 1 < n_chunks:
                for cp in page_copies(b, c + 1, 1 - slot, True):
                    cp.start()
            else:
                @pl.when(b + 1 < db)
                def _():
                    for cp in page_copies(b + 1, 0, 1 - slot, True):
                        cp.start()
            for cp in page_copies(b, c, slot, False):
                cp.wait()
            k_pages = [kbuf.at[slot, idx] for idx in range(chunk_pages)]
            v_pages = [vbuf.at[slot, idx] for idx in range(chunk_pages)]
            m_new, l_new, alpha, pb = _decode_scores(qbd, k_pages, state, c * chunk_pages * page, past_len + t_row,
                                                     slope_row)
            if region == 0:
                x1 = _post_head(x_ref, mix_ref, yc_ref, wo_ref, gpost_ref, gfpre_ref, hn_s)
            state = (m_new, l_new, _decode_values(pb, alpha, v_pages, state, page))
            for f in ffn_order[region]:
                _ffn_chunk(f, hn_s, wgu_ref, act_s)
        y = _decode_finish(qbd, state, gather_rows(kn_ref, b), gather_rows(vn_ref, b), lam, subg_ref[0], t_row,
                           slope_row, t_new, lam_init)
        for t in range(t_new):
            oa_ref[pl.ds(t * db + b, 1), :] = y[t:t + 1, :]

    _post_tail(x1, act_s, wd_ref, gfpost_ref, o_ref)

    @pl.when(i == pl.num_programs(0) - 1)
    def _():
        n_s = xs_ref.shape[0]
        hn_v, act_v = hn_s.at[pl.ds(0, n_s)], act_s.at[pl.ds(0, n_s)]
        x1s = _post_head(xs_ref, mixs_ref, oa_ref, wo_ref, gpost_ref, gfpre_ref, hn_v)
        for f in range(N_FF_CHUNKS):
            _ffn_chunk(f, hn_v, wgu_ref, act_v)
        _post_tail(x1s, act_v, wd_ref, gfpost_ref, xso_ref)


def _post_attn(x2d, mix3, yc, xs, mix3_s, page_table, q, kn, vn, cache_kt, cache_vt, wts, layer, *, tm, db, t_new,
               lam_init):
    m = x2d.shape[0]
    n_steps = m // tm
    n_pages = page_table.shape[1]
    page = cache_kt.shape[-1]
    chunk_pages = min(DECODE_CHUNK_PAGES, n_pages)
    ms = db * t_new
    assert db % n_steps == 0 and n_pages % chunk_pages == 0 and ms <= tm
    row = lambda i, pt: (i, 0)
    full = lambda i, pt: (0, 0)
    lvec = _layer_spec((1, DK), layer)
    in_specs = _post_specs(tm, layer, row)
    in_specs += [pl.BlockSpec((ms, D_GROUP), full)] * 3 + [lvec] * 4 + [_layer_spec((1, D_GROUP), layer)]
    in_specs += [pl.BlockSpec((ms, D_MODEL), full), pl.BlockSpec((ms, 3 * D_GROUP), full)]
    in_specs += [pl.BlockSpec(memory_space=pl.ANY)] * 2
    page_buf = pltpu.VMEM((2, chunk_pages, D_GROUP, page), F32)
    grid_spec = pltpu.PrefetchScalarGridSpec(
        num_scalar_prefetch=1,
        grid=(n_steps,),
        in_specs=in_specs,
        out_specs=[pl.BlockSpec((tm, D_MODEL), row), pl.BlockSpec((ms, D_GROUP), full),
                   pl.BlockSpec((ms, D_MODEL), full)],
        scratch_shapes=[pltpu.VMEM((tm, D_MODEL), BF16), pltpu.VMEM((tm, D_FF), BF16), page_buf, page_buf,
                        pltpu.SemaphoreType.DMA((2, 2))],
    )
    x_new, _, xs_new = pl.pallas_call(
        functools.partial(_post_attn_kernel, layer=layer, db=db, seqs_per_step=db // n_steps, t_new=t_new, page=page,
                          n_chunks=n_pages // chunk_pages, chunk_pages=chunk_pages, past_len=n_pages * page,
                          lam_init=lam_init),
        grid_spec=grid_spec,
        out_shape=[jax.ShapeDtypeStruct((m, D_MODEL), F32), jax.ShapeDtypeStruct((ms, D_GROUP), F32),
                   jax.ShapeDtypeStruct((ms, D_MODEL), F32)],
        compiler_params=pltpu.CompilerParams(
            dimension_semantics=("arbitrary",), vmem_limit_bytes=VMEM_LIMIT_BYTES),
        name="post_attn",
    )(page_table, *_post_args(x2d, mix3, yc, wts), q, kn, vn, wts["lambda_q1"], wts["lambda_k1"], wts["lambda_q2"],
      wts["lambda_k2"], wts["attn_sub_g4"], xs, mix3_s, cache_kt, cache_vt)
    return x_new, xs_new


def _prepare_weights(w_in, w_out, conv_w, conv_b, conv_ln_g, conv_ln_b, sgu_w, sgu_b, sgu_ln_g, sgu_ln_b,
                     lambda_q1, lambda_k1, lambda_q2, lambda_k2, attn_sub_g, pool_w, pool_scale, g_mix_pre,
                     g_mix_post, g_ffn_pre, g_ffn_post, w_gu, w_down, t_new):
    depth = w_in.shape[0]
    hd = D_GROUP // SGU_HEADS
    sgu_w4 = jnp.repeat(jnp.transpose(sgu_w[:, :, :t_new, :t_new], (0, 2, 3, 1)), hd, axis=3)
    eye = jnp.eye(len(POOL_WINDOWS), dtype=F32)
    row = lambda a: a[:, None, :]
    return {
        "w_in": w_in.astype(BF16), "w_out": w_out.astype(BF16), "w_gu": w_gu.astype(BF16),
        "w_down": w_down.astype(BF16),
        "conv_w": conv_w, "conv_b": row(conv_b), "conv_ln_g": row(conv_ln_g), "conv_ln_b": row(conv_ln_b),
        "sgu_w": sgu_w,
        "sgu_b_full": jnp.repeat(jnp.swapaxes(sgu_b, 1, 2), hd, axis=2),
        "sgu_w4": sgu_w4.reshape(depth, t_new * t_new, D_GROUP),
        "sgu_ln_g": row(sgu_ln_g), "sgu_ln_b": row(sgu_ln_b),
        "lambda_q1": row(lambda_q1), "lambda_k1": row(lambda_k1), "lambda_q2": row(lambda_q2),
        "lambda_k2": row(lambda_k2),
        "attn_sub_g_col": attn_sub_g[:, :, None], "attn_sub_g4": jnp.tile(row(attn_sub_g), (1, 1, N_HEADS)),
        "pool_w_bd": jnp.einsum("lgce,gh->lgche", pool_w, eye).reshape(depth, D_GROUP, D_GROUP).astype(BF16),
        "pool_scale": row(pool_scale),
        "g_mix_pre": row(g_mix_pre), "g_mix_post": row(g_mix_post), "g_ffn_pre": row(g_ffn_pre),
        "g_ffn_post": row(g_ffn_post),
    }


def kernel(x_prompt, x_sample, cache_k, cache_v, state_conv, state_pool, page_table, w_in, w_out, conv_w, conv_b, conv_ln_g, conv_ln_b, sgu_w, sgu_b, sgu_ln_g, sgu_ln_b, lambda_q1, lambda_k1, lambda_q2, lambda_k2, attn_sub_g, pool_w, pool_scale, g_mix_pre, g_mix_post, g_ffn_pre, g_ffn_post, w_gu, w_down):
    batch, seq, _ = x_prompt.shape
    db, t_new, _ = x_sample.shape
    depth, n_pool, page = cache_k.shape[0], cache_k.shape[1], cache_k.shape[2]
    past_len = page_table.shape[1] * page
    assert seq % ROW_TILE == 0 and seq % SEQ_TILE == 0 and db % SUBLANES == 0
    assert seq % ATT_Q_TILE == 0 and ATT_Q_TILE % ATT_K_TILE == 0
    assert page == LANES

    wts = _prepare_weights(w_in, w_out, conv_w, conv_b, conv_ln_g, conv_ln_b, sgu_w, sgu_b, sgu_ln_g, sgu_ln_b,
                           lambda_q1, lambda_k1, lambda_q2, lambda_k2, attn_sub_g, pool_w, pool_scale,
                           g_mix_pre, g_mix_post, g_ffn_pre, g_ffn_post, w_gu, w_down, t_new)

    cache_kt = jnp.transpose(cache_k, (0, 1, 3, 4, 5, 2)).reshape(depth, n_pool, D_GROUP, page)
    cache_vt = jnp.transpose(cache_v, (0, 1, 3, 4, 2)).reshape(depth, n_pool, D_GROUP, page)
    ms = db * t_new
    xs = jnp.swapaxes(x_sample, 0, 1).reshape(ms, D_MODEL)
    sc_tm = jnp.swapaxes(state_conv, 1, 2)
    sp_tm = jnp.swapaxes(state_pool, 1, 2)
    xp = x_prompt.reshape(batch * seq, D_MODEL)

    kv_all = None
    cp_l, pp_l = [], []
    ks_l, vs_l, cs_l, ps_l, gs_l = [], [], [], [], []
    for l in range(depth):
        lam_init = 0.8 - 0.6 * math.exp(-0.3 * l)
        mix3, k_rows, qt, kt_all, vt_all, conv_new, pool_new = _proj_mix_prompt(xp, wts, kv_all, l, batch=batch,
                                                                                seq=seq)
        kv_all = (kt_all, vt_all)
        yc = _attn_prompt(qt, k_rows, vt_all, wts, l, batch=batch, seq=seq, lam_init=lam_init)
        cp_l.append(conv_new); pp_l.append(pool_new)
        mix3_s, q, kn, vn, conv_new, pool_new, sgu_v = _proj_mix_sample(xs, sc_tm, sp_tm, wts, l, db=db, t_new=t_new,
                                                                        past_len=past_len)
        xp, xs = _post_attn(xp, mix3, yc, xs, mix3_s, page_table, q, kn, vn, cache_kt, cache_vt, wts, l, tm=ROW_TILE,
                            db=db, t_new=t_new, lam_init=lam_init)
        ks_l.append(kn); vs_l.append(vn); cs_l.append(conv_new); ps_l.append(pool_new); gs_l.append(sgu_v)

    kt_all, vt_all = kv_all
    y_prompt = xp.reshape(batch, seq, D_MODEL)
    y_sample = jnp.swapaxes(xs.reshape(t_new, db, D_MODEL), 0, 1)
    k_prompt = jnp.transpose(kt_all.reshape(depth, batch, N_HEADS, 2, DK, seq), (0, 1, 5, 2, 3, 4))
    v_prompt = jnp.transpose(vt_all.reshape(depth, batch, N_HEADS, DV, seq), (0, 1, 4, 2, 3))
    tm_to_bm = lambda a: jnp.swapaxes(jnp.stack(a).reshape(depth, t_new, db, D_GROUP), 1, 2)
    k_sample = tm_to_bm(ks_l).reshape(depth, db, t_new, N_HEADS, 2, DK)
    v_sample = tm_to_bm(vs_l).reshape(depth, db, t_new, N_HEADS, DV)
    conv_sample = jnp.swapaxes(jnp.stack(cs_l), 1, 2)
    pool_sample = jnp.swapaxes(jnp.stack(ps_l), 1, 2)
    sgu_v_sample = tm_to_bm(gs_l)
    return (y_prompt, y_sample, k_prompt, v_prompt, jnp.stack(cp_l), jnp.stack(pp_l),
            k_sample, v_sample, conv_sample, pool_sample, sgu_v_sample)
```

```python
import functools
import math

import jax
import jax.numpy as jnp
from jax import lax
from jax.experimental import pallas as pl
from jax.experimental.pallas import tpu as pltpu

F32 = jnp.float32
BF16 = jnp.bfloat16

D_MODEL = 1024
D_GROUP = 256
D_IN = 8 * D_GROUP
CONV_W = 31
CONV_HALO = CONV_W - 1
SGU_CHUNK = 128
SGU_HEADS = 4
N_HEADS = 4
DK = 32
DV = 64
POOL_WINDOWS = (2, 4, 8, 16)
POOL_HALO = max(POOL_WINDOWS) - 1
POOL_GD = D_GROUP // len(POOL_WINDOWS)
D_FF = 2816
FF_CHUNK = 256
N_FF_CHUNKS = D_FF // FF_CHUNK
RMS_EPS = 1e-6
LN_EPS = 1e-5
Q_SCALE = DK ** -0.5
LOG2E = math.log2(math.e)

SUBLANES = 8
LANES = 128
VMEM_LIMIT_BYTES = 56 * 1024 * 1024

ROW_TILE = 512
SEQ_TILE = 512
CONV_ROWS = 64
ATT_Q_TILE = 512
ATT_K_TILE = 256
DECODE_CHUNK_PAGES = 32

NT_DIMS = (((1,), (1,)), ((), ()))


def _rms(x, g, eps):
    return x * lax.rsqrt(jnp.mean(x * x, axis=-1, keepdims=True) + eps) * g


def _layer_norm(x, g, b):
    mu = jnp.mean(x, axis=-1, keepdims=True)
    xc = x - mu
    var = jnp.mean(xc * xc, axis=-1, keepdims=True)
    return xc * lax.rsqrt(var + LN_EPS) * g + b


def _gelu(x):
    return 0.5 * x * (1.0 + lax.erf(x * (2.0 ** -0.5)))


def _sigmoid(x):
    return 1.0 / (1.0 + jnp.exp(-x))


def _lane_group_select(pieces, group_width, shape):
    lane = lax.broadcasted_iota(jnp.int32, shape, len(shape) - 1)
    out = pieces[-1]
    for g in range(len(pieces) - 2, -1, -1):
        out = jnp.where(lane < (g + 1) * group_width, pieces[g], out)
    return out


def _layer_spec(shape, layer):
    zeros = (0,) * len(shape)
    return pl.BlockSpec((1,) + tuple(shape), lambda *_: (layer,) + zeros)


def _layer_spec_single(shape, layer):
    zeros = (0,) * len(shape)
    return pl.BlockSpec((1,) + tuple(shape), lambda *_: (layer,) + zeros, pipeline_mode=pl.Buffered(1))


AG_COL, UV_COL, Q_COL, K_COL, V_COL, XD_COL = (0, 2 * D_GROUP, 4 * D_GROUP, 5 * D_GROUP, 6 * D_GROUP, 7 * D_GROUP)


def _conv_post(acc, clg, clb):
    y = _layer_norm(acc, clg, clb)
    return y * _sigmoid(y)


def _pool_select(wins, shape):
    return _lane_group_select(wins, POOL_GD, shape)


HPAD = 32
XPAD = 24


def _proj_mix_kernel(x_ref, g_ref, w_ref, cw_ref, cb_ref, clg_ref, clb_ref, sw_ref, sb_ref, slg_ref, slb_ref,
                     pw_ref, ps_ref, *rest, ts, n_alias):
    (mix_ref, k_ref, qt_ref, kt_ref, vt_ref, convn_ref, pooln_ref, hext, hshift, xext, s2, s4, s8) = rest[n_alias:]
    i = pl.program_id(1)

    @pl.when(i == 0)
    def _():
        hext[0:HPAD, :] = jnp.zeros((HPAD, D_GROUP), F32)
        xext[0:XPAD, :] = jnp.zeros((XPAD, D_GROUP), F32)
        for buf in (s2, s4, s8):
            buf[0:SUBLANES, :] = jnp.zeros((SUBLANES, D_GROUP), F32)

    hn = _rms(x_ref[...], g_ref[0], RMS_EPS).astype(BF16)
    proj = lambda col, width: jnp.dot(hn, w_ref[0, :, col:col + width], preferred_element_type=F32)
    z_ag = proj(AG_COL, 2 * D_GROUP)
    z_uv = proj(UV_COL, 2 * D_GROUP)
    z_qkv = proj(Q_COL, 3 * D_GROUP)
    z_xd = proj(XD_COL, D_GROUP)

    k_rows = z_qkv[:, D_GROUP:2 * D_GROUP]
    k_ref[...] = k_rows
    qt_ref[0] = (z_qkv[:, 0:D_GROUP] * (Q_SCALE * LOG2E)).T.astype(BF16)
    kt_ref[0, 0] = k_rows.T
    vt_ref[0, 0] = z_qkv[:, 2 * D_GROUP:3 * D_GROUP].T

    hext[HPAD:HPAD + ts, :] = z_ag[:, 0:D_GROUP] * _sigmoid(z_ag[:, D_GROUP:2 * D_GROUP])
    slab = ts + HPAD - SUBLANES
    for rho in range(1, SUBLANES):
        hshift[rho - 1] = hext[rho:rho + slab, :]
    for c in range(ts // CONV_ROWS):
        r0 = c * CONV_ROWS
        acc = jnp.broadcast_to(cb_ref[0], (CONV_ROWS, D_GROUP))
        for j in range(CONV_W):
            shift = HPAD - CONV_HALO + j
            q8, rho = shift // SUBLANES * SUBLANES, shift % SUBLANES
            if rho == 0:
                rows = hext[r0 + q8:r0 + q8 + CONV_ROWS, :]
            else:
                rows = hshift[rho - 1, r0 + q8:r0 + q8 + CONV_ROWS, :]
            acc = acc + cw_ref[0, j:j + 1, :] * rows
        mix_ref[r0:r0 + CONV_ROWS, 0:D_GROUP] = _conv_post(acc, clg_ref[0], clb_ref[0]).astype(BF16)

    u = _gelu(z_uv[:, 0:D_GROUP])
    v = _layer_norm(_gelu(z_uv[:, D_GROUP:2 * D_GROUP]), slg_ref[0], slb_ref[0])
    tri_r = lax.broadcasted_iota(jnp.int32, (SGU_CHUNK, SGU_CHUNK), 0)
    tri_c = lax.broadcasted_iota(jnp.int32, (SGU_CHUNK, SGU_CHUNK), 1)
    w_heads = [jnp.where(tri_c <= tri_r, sw_ref[0, g], 0.0).astype(BF16) for g in range(SGU_HEADS)]
    w_stack = jnp.concatenate(w_heads, axis=0)
    hd = D_GROUP // SGU_HEADS
    for c in range(ts // SGU_CHUNK):
        rows = slice(c * SGU_CHUNK, (c + 1) * SGU_CHUNK)
        r = jnp.dot(w_stack, v[rows].astype(BF16), preferred_element_type=F32)
        mixed = _lane_group_select([r[g * SGU_CHUNK:(g + 1) * SGU_CHUNK] for g in range(SGU_HEADS)],
                                   hd, (SGU_CHUNK, D_GROUP))
        mix_ref[rows, D_GROUP:2 * D_GROUP] = (u[rows] * (mixed + sb_ref[0])).astype(BF16)

    n = XPAD + ts
    xext[XPAD:n, :] = z_xd
    s2[SUBLANES:n, :] = xext[SUBLANES:n, :] + xext[SUBLANES - 1:n - 1, :]
    s4[SUBLANES:n, :] = s2[SUBLANES:n, :] + s2[SUBLANES - 2:n - 2, :]
    s8[2 * SUBLANES:n, :] = s4[2 * SUBLANES:n, :] + s4[2 * SUBLANES - 4:n - 4, :]
    win16 = s8[XPAD:n, :] + s8[XPAD - 8:n - 8, :]
    x0 = xext[XPAD:n, :]
    wins = _pool_select([s2[XPAD:n, :], s4[XPAD:n, :], s8[XPAD:n, :], win16], (ts, D_GROUP))
    wvec = _pool_select([jnp.full((1, D_GROUP), float(w), F32) for w in POOL_WINDOWS], (1, D_GROUP))
    pos = (i * ts + lax.broadcasted_iota(jnp.int32, (ts, D_GROUP), 0)).astype(F32)
    d = wins / jnp.minimum(pos + 1.0, wvec) - x0
    y = jnp.dot(d.astype(BF16), pw_ref[0], preferred_element_type=F32) * ps_ref[0]
    mix_ref[:, 2 * D_GROUP:3 * D_GROUP] = y.astype(BF16)

    hext[0:HPAD, :] = hext[ts:ts + HPAD, :]
    xext[SUBLANES:XPAD, :] = xext[ts + SUBLANES:ts + XPAD, :]

    @pl.when(i == pl.num_programs(1) - 1)
    def _():
        convn_ref[0] = hext[HPAD - CONV_HALO:HPAD, :]
        pooln_ref[0] = xext[XPAD - POOL_HALO:XPAD, :]


def _proj_mix_prompt(x2d, wts, kv_prev, layer, *, batch, seq):
    ts = SEQ_TILE
    ns = seq // ts
    m = batch * seq
    depth = wts["w_in"].shape[0]
    row_map = lambda b, i: (b * ns + i, 0)
    vec = _layer_spec((1, D_GROUP), layer)
    ext = lambda pad: pltpu.VMEM((pad + ts, D_GROUP), F32)
    in_specs = [pl.BlockSpec((ts, D_MODEL), row_map), _layer_spec((1, D_MODEL), layer),
                _layer_spec_single((D_MODEL, D_IN), layer),
                _layer_spec((CONV_W, D_GROUP), layer), vec, vec, vec,
                _layer_spec((SGU_HEADS, SGU_CHUNK, SGU_CHUNK), layer),
                _layer_spec((SGU_CHUNK, D_GROUP), layer), vec, vec,
                _layer_spec((D_GROUP, D_GROUP), layer), vec]
    args = [x2d, wts["g_mix_pre"], wts["w_in"], wts["conv_w"], wts["conv_b"], wts["conv_ln_g"], wts["conv_ln_b"],
            wts["sgu_w"], wts["sgu_b_full"], wts["sgu_ln_g"], wts["sgu_ln_b"], wts["pool_w_bd"], wts["pool_scale"]]
    kv_spec = pl.BlockSpec((1, 1, D_GROUP, ts), lambda b, i: (layer, b, 0, i))
    kv_shape = jax.ShapeDtypeStruct((depth, batch, D_GROUP, seq), F32)
    out_specs = [pl.BlockSpec((ts, 3 * D_GROUP), row_map), pl.BlockSpec((ts, D_GROUP), row_map),
                 pl.BlockSpec((1, D_GROUP, ts), lambda b, i: (b, 0, i)), kv_spec, kv_spec,
                 pl.BlockSpec((1, CONV_HALO, D_GROUP), lambda b, i: (b, 0, 0)),
                 pl.BlockSpec((1, POOL_HALO, D_GROUP), lambda b, i: (b, 0, 0))]
    out_shape = [jax.ShapeDtypeStruct((m, 3 * D_GROUP), BF16), jax.ShapeDtypeStruct((m, D_GROUP), F32),
                 jax.ShapeDtypeStruct((batch, D_GROUP, seq), BF16), kv_shape, kv_shape,
                 jax.ShapeDtypeStruct((batch, CONV_HALO, D_GROUP), F32),
                 jax.ShapeDtypeStruct((batch, POOL_HALO, D_GROUP), F32)]
    kv_out_index = 3
    aliases = {}
    n_alias = 0
    if kv_prev is not None:
        n_alias = len(kv_prev)
        for a, buf in enumerate(kv_prev):
            aliases[len(args)] = kv_out_index + a
            in_specs.append(pl.BlockSpec(memory_space=pl.ANY))
            args.append(buf)
    return pl.pallas_call(
        functools.partial(_proj_mix_kernel, ts=ts, n_alias=n_alias),
        grid=(batch, ns),
        in_specs=in_specs,
        out_specs=out_specs,
        out_shape=out_shape,
        input_output_aliases=aliases,
        scratch_shapes=[ext(HPAD), pltpu.VMEM((SUBLANES - 1, ts + HPAD - SUBLANES, D_GROUP), F32),
                        ext(XPAD), ext(XPAD), ext(XPAD), ext(XPAD)],
        compiler_params=pltpu.CompilerParams(
            dimension_semantics=("arbitrary", "arbitrary"), vmem_limit_bytes=VMEM_LIMIT_BYTES),
        name="proj_mix_prompt",
    )(*args)


def _proj_mix_sample_kernel(x_ref, g_ref, w_ref, sc_ref, sp_ref, cw_ref, cb_ref, clg_ref, clb_ref, sw4_ref, sb_ref,
                            slg_ref, slb_ref, pw_ref, ps_ref, mix_ref, q_ref, k_ref, v_ref, convn_ref, pooln_ref,
                            sguv_ref, *, db, t_new, past_len):
    rows = lambda t: slice(t * db, (t + 1) * db)
    h = _rms(x_ref[...], g_ref[0], RMS_EPS).astype(BF16)
    z = jnp.dot(h, w_ref[0], preferred_element_type=F32)
    q_ref[...] = z[:, Q_COL:Q_COL + D_GROUP]
    k_ref[...] = z[:, K_COL:K_COL + D_GROUP]
    v_ref[...] = z[:, V_COL:V_COL + D_GROUP]

    ext = [sc_ref[0, j] for j in range(CONV_HALO)]
    ext += [z[rows(t), AG_COL:AG_COL + D_GROUP] * _sigmoid(z[rows(t), AG_COL + D_GROUP:AG_COL + 2 * D_GROUP])
            for t in range(t_new)]
    for t in range(t_new):
        acc = jnp.broadcast_to(cb_ref[0], (db, D_GROUP))
        for j in range(CONV_W):
            acc = acc + cw_ref[0, j:j + 1, :] * ext[t + j]
        mix_ref[rows(t), 0:D_GROUP] = _conv_post(acc, clg_ref[0], clb_ref[0]).astype(BF16)
    for j in range(CONV_HALO):
        convn_ref[j] = ext[t_new + j]

    u = _gelu(z[:, UV_COL:UV_COL + D_GROUP])
    v = _layer_norm(_gelu(z[:, UV_COL + D_GROUP:UV_COL + 2 * D_GROUP]), slg_ref[0], slb_ref[0])
    sguv_ref[...] = v
    for t in range(t_new):
        mixed = jnp.broadcast_to(sb_ref[0, t:t + 1, :], (db, D_GROUP))
        for s in range(t + 1):
            mixed = mixed + sw4_ref[0, t * t_new + s:t * t_new + s + 1, :] * v[rows(s)]
        mix_ref[rows(t), D_GROUP:2 * D_GROUP] = (u[rows(t)] * mixed).astype(BF16)

    pext = [sp_ref[0, j] for j in range(POOL_HALO)] + [z[rows(t), XD_COL:XD_COL + D_GROUP] for t in range(t_new)]
    ds = []
    for t in range(t_new):
        x0 = pext[POOL_HALO + t]
        acc = x0
        wins = []
        for k in range(1, POOL_HALO + 1):
            acc = acc + pext[POOL_HALO + t - k]
            if k + 1 in POOL_WINDOWS:
                wins.append(acc)
        cnts = [jnp.full((1, D_GROUP), float(min(past_len + t + 1, w)), F32) for w in POOL_WINDOWS]
        cnt = _pool_select(cnts, (1, D_GROUP))
        ds.append(_pool_select(wins, (db, D_GROUP)) / cnt - x0)
    d = jnp.concatenate(ds, axis=0)
    y = jnp.dot(d.astype(BF16), pw_ref[0], preferred_element_type=F32) * ps_ref[0]
    mix_ref[:, 2 * D_GROUP:3 * D_GROUP] = y.astype(BF16)
    for j in range(POOL_HALO):
        pooln_ref[j] = pext[t_new + j]


def _proj_mix_sample(xs, sc_tm, sp_tm, wts, layer, *, db, t_new, past_len):
    m = db * t_new
    full = lambda shape: pl.BlockSpec(shape, lambda i: (0,) * len(shape))
    vec = _layer_spec((1, D_GROUP), layer)
    act = jax.ShapeDtypeStruct((m, D_GROUP), F32)
    return pl.pallas_call(
        functools.partial(_proj_mix_sample_kernel, db=db, t_new=t_new, past_len=past_len),
        grid=(1,),
        in_specs=[full((m, D_MODEL)), _layer_spec((1, D_MODEL), layer), _layer_spec((D_MODEL, D_IN), layer),
                  _layer_spec((CONV_HALO, db, D_GROUP), layer), _layer_spec((POOL_HALO, db, D_GROUP), layer),
                  _layer_spec((CONV_W, D_GROUP), layer), vec, vec, vec,
                  _layer_spec((t_new * t_new, D_GROUP), layer),
                  _layer_spec((SGU_CHUNK, D_GROUP), layer), vec, vec,
                  _layer_spec((D_GROUP, D_GROUP), layer), vec],
        out_specs=[full((m, 3 * D_GROUP)), full((m, D_GROUP)), full((m, D_GROUP)), full((m, D_GROUP)),
                   full((CONV_HALO, db, D_GROUP)), full((POOL_HALO, db, D_GROUP)), full((m, D_GROUP))],
        out_shape=[jax.ShapeDtypeStruct((m, 3 * D_GROUP), BF16), act, act, act,
                   jax.ShapeDtypeStruct((CONV_HALO, db, D_GROUP), F32),
                   jax.ShapeDtypeStruct((POOL_HALO, db, D_GROUP), F32), act],
        compiler_params=pltpu.CompilerParams(
            dimension_semantics=("arbitrary",), vmem_limit_bytes=VMEM_LIMIT_BYTES),
        name="proj_mix_sample",
    )(xs, wts["g_mix_pre"], wts["w_in"], sc_tm, sp_tm, wts["conv_w"], wts["conv_b"], wts["conv_ln_g"],
      wts["conv_ln_b"], wts["sgu_w4"], wts["sgu_b_full"], wts["sgu_ln_g"], wts["sgu_ln_b"], wts["pool_w_bd"],
      wts["pool_scale"])


def _diff_lambda(lq1, lk1, lq2, lk2, lam_init):
    s1 = jnp.sum(lq1 * lk1, axis=-1, keepdims=True)
    s2 = jnp.sum(lq2 * lk2, axis=-1, keepdims=True)
    return jnp.exp(s1) - jnp.exp(s2) + lam_init


def _alibi_slope(h):
    return 2.0 ** (-8.0 * (h + 1) / N_HEADS)


KEY_AUG = LANES
VAL_AUG = 80
N_BIAS = 3


def _attn_prompt_kernel(qt_ref, k_ref, vt_ref, lq1_ref, lk1_ref, lq2_ref, lk2_ref, subg_ref, o_ref, kaug, vaug, *,
                        tq, tk, nblk, lam_init):
    i = pl.program_id(1)
    hd = 2 * DK
    diag_blocks = tq // tk

    @pl.when(i == 0)
    def _():
        key = lax.broadcasted_iota(jnp.int32, (tk, KEY_AUG), 0).astype(F32)
        lane = lax.broadcasted_iota(jnp.int32, (tk, KEY_AUG), 1)
        ones_row = jnp.where(lax.broadcasted_iota(jnp.int32, (VAL_AUG - hd, tk), 0) == 0, 1.0, 0.0).astype(BF16)
        for h in range(N_HEADS):
            bias = (_alibi_slope(h) * LOG2E) * key
            hi = bias.astype(BF16).astype(F32)
            mid = (bias - hi).astype(BF16).astype(F32)
            lo = ((bias - hi) - mid).astype(BF16).astype(F32)
            aug = jnp.where(lane == hd, hi, jnp.where(lane == hd + 1, mid, jnp.where(lane == hd + 2, lo, 0.0)))
            for j in range(nblk):
                kblk = k_ref[j * tk:(j + 1) * tk, h * hd:(h + 1) * hd]
                kpad = jnp.concatenate([kblk, jnp.zeros((tk, KEY_AUG - hd), F32)], axis=1)
                kaug[j, h] = jnp.where(lane < hd, kpad, aug).astype(BF16)
                vaug[j, h, 0:hd, :] = vt_ref[0, 0, h * hd:(h + 1) * hd, j * tk:(j + 1) * tk].astype(BF16)
                vaug[j, h, hd:VAL_AUG, :] = ones_row

    lam = _diff_lambda(lq1_ref[0], lk1_ref[0], lq2_ref[0], lk2_ref[0], lam_init)
    qt = qt_ref[0]
    zero = jnp.zeros((DK, tq), BF16)
    ones_aug = jnp.where(lax.broadcasted_iota(jnp.int32, (KEY_AUG - hd, 2 * tq), 0) < N_BIAS, 1.0, 0.0).astype(BF16)
    qs = []
    for h in range(N_HEADS):
        q1 = qt[h * hd:h * hd + DK]
        q2 = qt[h * hd + DK:(h + 1) * hd]
        body = jnp.concatenate([jnp.concatenate([q1, zero], axis=0), jnp.concatenate([zero, q2], axis=0)], axis=1)
        qs.append(jnp.concatenate([body, ones_aug], axis=0))
    lower = (lax.broadcasted_iota(jnp.int32, (tk, tk), 0) <= lax.broadcasted_iota(jnp.int32, (tk, tk), 1))

    n_cb = 2 * diag_blocks

    def live_cols(d):
        return [cb for cb in range(n_cb) if d is None or cb % diag_blocks >= d]

    def take(x, cbs):
        return x if len(cbs) == n_cb else jnp.concatenate([x[:, cb * tk:(cb + 1) * tk] for cb in cbs], axis=1)

    def put(full, part, cbs):
        if len(cbs) == n_cb:
            return part
        pieces = [part[:, cbs.index(cb) * tk:(cbs.index(cb) + 1) * tk] if cb in cbs else full[:, cb * tk:(cb + 1) * tk]
                  for cb in range(n_cb)]
        return jnp.concatenate(pieces, axis=1)

    def causal_mask(s, d, cbs):
        pieces = [s[:, k * tk:(k + 1) * tk] for k in range(len(cbs))]
        pieces = [jnp.where(lower, blk, -jnp.inf) if cb % diag_blocks == d else blk for blk, cb in zip(pieces, cbs)]
        return jnp.concatenate(pieces, axis=1)

    def blocks(js, carry, diags):
        cols = [live_cols(d) for d in diags]
        scores = [[jnp.dot(kaug[j, h], take(qs[h], cols[n]), preferred_element_type=F32) for h in range(N_HEADS)]
                  for n, j in enumerate(js)]
        stats = [carry[h][0] for h in range(N_HEADS)]
        probs = []
        for n, j in enumerate(js):
            row = []
            for h in range(N_HEADS):
                s = scores[n][h] if diags[n] is None else causal_mask(scores[n][h], diags[n], cols[n])
                c = (_alibi_slope(h) * LOG2E) * (j * tk - i * tq).astype(F32)
                m = take(stats[h], cols[n])
                m_new = jnp.maximum(m, jnp.max(s, axis=0, keepdims=True) + c)
                row.append((jnp.exp2(m - m_new), jnp.exp2(s - (m_new - c)).astype(BF16)))
                stats[h] = put(stats[h], m_new, cols[n])
            probs.append(row)
        accs = [carry[h][1] for h in range(N_HEADS)]
        for n, j in enumerate(js):
            for h in range(N_HEADS):
                alpha, p = probs[n][h]
                part = alpha * take(accs[h], cols[n]) + jnp.dot(vaug[j, h], p, preferred_element_type=F32)
                accs[h] = put(accs[h], part, cols[n])
        return tuple((stats[h], accs[h]) for h in range(N_HEADS))

    init = tuple((jnp.full((1, 2 * tq), -jnp.inf, F32), jnp.zeros((VAL_AUG, 2 * tq), F32)) for _ in range(N_HEADS))
    pair = lambda jj, carry: blocks([jj * diag_blocks + d for d in range(diag_blocks)], carry, [None] * diag_blocks)
    carry = lax.fori_loop(0, i, pair, init)
    carry = blocks([i * diag_blocks + d for d in range(diag_blocks)], carry, list(range(diag_blocks)))
    ys = []
    for h in range(N_HEADS):
        acc = carry[h][1]
        o = acc[0:hd, :] * (1.0 / acc[hd:hd + 1, :])
        od = o[:, 0:tq] - lam * o[:, tq:2 * tq]
        ms = jnp.mean(od * od, axis=0, keepdims=True)
        ys.append(od * lax.rsqrt(ms + LN_EPS) * subg_ref[0] * (1.0 - lam_init))
    o_ref[...] = jnp.concatenate(ys, axis=0).T.astype(BF16)


def _attn_prompt(qt, k_rows, vt_all, wts, layer, *, batch, seq, lam_init):
    tq, tk = ATT_Q_TILE, ATT_K_TILE
    ns = seq // tq
    nk = seq // tk
    lvec = _layer_spec((1, DK), layer)
    return pl.pallas_call(
        functools.partial(_attn_prompt_kernel, tq=tq, tk=tk, nblk=nk, lam_init=lam_init),
        grid=(batch, ns),
        in_specs=[pl.BlockSpec((1, D_GROUP, tq), lambda b, i: (b, 0, i)),
                  pl.BlockSpec((seq, D_GROUP), lambda b, i: (b, 0)),
                  pl.BlockSpec((1, 1, D_GROUP, seq), lambda b, i: (layer, b, 0, 0)),
                  lvec, lvec, lvec, lvec, _layer_spec((DV, 1), layer)],
        out_specs=pl.BlockSpec((tq, D_GROUP), lambda b, i: (b * ns + i, 0)),
        out_shape=jax.ShapeDtypeStruct((batch * seq, D_GROUP), BF16),
        scratch_shapes=[pltpu.VMEM((nk, N_HEADS, tk, KEY_AUG), BF16), pltpu.VMEM((nk, N_HEADS, VAL_AUG, tk), BF16)],
        compiler_params=pltpu.CompilerParams(
            dimension_semantics=("arbitrary", "arbitrary"), vmem_limit_bytes=VMEM_LIMIT_BYTES),
        name="attn_prompt",
    )(qt, k_rows, vt_all, wts["lambda_q1"], wts["lambda_k1"], wts["lambda_q2"], wts["lambda_k2"],
      wts["attn_sub_g_col"])


def _decode_rows(t_new):
    n_rows = N_HEADS * 2 * t_new
    row1 = lax.broadcasted_iota(jnp.int32, (n_rows, 1), 0)
    t_row = (row1 % t_new).astype(F32)
    slope_row = jnp.zeros((n_rows, 1), F32)
    for h in range(N_HEADS):
        slope_row = jnp.where(row1 // (2 * t_new) == h, _alibi_slope(h), slope_row)
    return n_rows, t_row, slope_row


def _decode_queries(q4, t_new):
    n_rows = N_HEADS * 2 * t_new
    qt = jnp.concatenate([q4 * Q_SCALE] * (2 * N_HEADS), axis=0)
    r = lax.broadcasted_iota(jnp.int32, (n_rows, D_GROUP), 0)
    c = lax.broadcasted_iota(jnp.int32, (n_rows, D_GROUP), 1)
    return jnp.where(c // DK == r // t_new, qt, 0.0).astype(BF16)


def _decode_scores(qbd, k_pages, state, first_pos, q_pos, slope_row):
    m_prev, l_prev, _ = state
    s = jnp.concatenate([jnp.dot(qbd, kp[...].astype(BF16), preferred_element_type=F32) for kp in k_pages], axis=1)
    kpos = (first_pos + lax.broadcasted_iota(jnp.int32, s.shape, 1)).astype(F32)
    s = s - slope_row * (q_pos - kpos)
    m_new = jnp.maximum(m_prev, jnp.max(s, axis=-1, keepdims=True))
    alpha = jnp.exp(m_prev - m_new)
    p = jnp.exp(s - m_new)
    l_new = alpha * l_prev + jnp.sum(p, axis=-1, keepdims=True)
    return m_new, l_new, alpha, p.astype(BF16)


def _decode_values(pb, alpha, v_pages, state, page):
    acc_prev = state[2]
    pv = jnp.zeros(acc_prev.shape, F32)
    for idx, vp in enumerate(v_pages):
        pv = pv + jnp.dot(pb[:, idx * page:(idx + 1) * page], vp[...].T.astype(BF16), preferred_element_type=F32)
    return alpha * acc_prev + pv


def _decode_finish(qbd, state, kn, vn, lam, subg, t_row, slope_row, t_new, lam_init):
    m_prev, l_prev, acc_prev = state
    n_rows = qbd.shape[0]
    kn8 = jnp.concatenate([kn, jnp.zeros((SUBLANES - t_new, D_GROUP), F32)], axis=0).astype(BF16)
    sn = lax.dot_general(qbd, kn8, NT_DIMS, preferred_element_type=F32)
    dist = t_row - lax.broadcasted_iota(jnp.int32, (n_rows, SUBLANES), 1).astype(F32)
    sn = jnp.where(dist >= 0.0, sn - slope_row * dist, -jnp.inf)
    m_fin = jnp.maximum(m_prev, jnp.max(sn, axis=-1, keepdims=True))
    alpha = jnp.exp(m_prev - m_fin)
    pn = jnp.exp(sn - m_fin)
    l_fin = alpha * l_prev + jnp.sum(pn, axis=-1, keepdims=True)
    acc = alpha * acc_prev
    for t in range(t_new):
        acc = acc + pn[:, t:t + 1] * vn[t:t + 1, :]
    o = acc / l_fin
    r = lax.broadcasted_iota(jnp.int32, (n_rows, D_GROUP), 0)
    c = lax.broadcasted_iota(jnp.int32, (n_rows, D_GROUP), 1)
    o = jnp.where(c // DV == r // (2 * t_new), o, 0.0)
    y0 = jnp.zeros((t_new, D_GROUP), F32)
    y1 = jnp.zeros((t_new, D_GROUP), F32)
    for h in range(N_HEADS):
        y0 = y0 + o[h * 2 * t_new:h * 2 * t_new + t_new]
        y1 = y1 + o[h * 2 * t_new + t_new:(h + 1) * 2 * t_new]
    od = y0 - lam * y1
    seg_r = lax.broadcasted_iota(jnp.int32, (D_GROUP, D_GROUP), 0) // DV
    seg_c = lax.broadcasted_iota(jnp.int32, (D_GROUP, D_GROUP), 1) // DV
    seg = jnp.where(seg_r == seg_c, 1.0 / DV, 0.0).astype(F32)
    ms = jnp.dot(od * od, seg, preferred_element_type=F32, precision=lax.Precision.HIGHEST)
    return od * lax.rsqrt(ms + LN_EPS) * subg * (1.0 - lam_init)


def _post_head(x_ref, mix_ref, yc_ref, wo_ref, gpost_ref, gfpre_ref, hn_s):
    ab, cd = 2 * D_GROUP, 3 * D_GROUP
    mixo = jnp.dot(mix_ref[:, 0:ab], wo_ref[0, 0:ab, :], preferred_element_type=F32)
    mixo = mixo + jnp.dot(yc_ref[...].astype(BF16), wo_ref[0, ab:cd, :], preferred_element_type=F32)
    mixo = mixo + jnp.dot(mix_ref[:, ab:cd], wo_ref[0, cd:D_MODEL, :], preferred_element_type=F32)
    x1 = x_ref[...] + _rms(mixo, gpost_ref[0], RMS_EPS)
    hn_s[...] = _rms(x1, gfpre_ref[0], RMS_EPS).astype(BF16)
    return x1


def _ffn_chunk(c, hn_s, wgu_ref, act_s):
    lo = c * FF_CHUNK
    gate = jnp.dot(hn_s[...], wgu_ref[0, :, lo:lo + FF_CHUNK], preferred_element_type=F32)
    up = jnp.dot(hn_s[...], wgu_ref[0, :, D_FF + lo:D_FF + lo + FF_CHUNK], preferred_element_type=F32)
    act_s[:, lo:lo + FF_CHUNK] = (gate * _sigmoid(gate) * up).astype(BF16)


def _post_tail(x1, act_s, wd_ref, gfpost_ref, o_ref):
    ffn = jnp.dot(act_s[...], wd_ref[0], preferred_element_type=F32)
    o_ref[...] = x1 + _rms(ffn, gfpost_ref[0], RMS_EPS)


def _post_specs(tm, layer, row):
    vec = _layer_spec((1, D_MODEL), layer)
    return [pl.BlockSpec((tm, D_MODEL), row),
            pl.BlockSpec((tm, 3 * D_GROUP), row),
            pl.BlockSpec((tm, D_GROUP), row),
            _layer_spec_single((D_MODEL, D_MODEL), layer), vec, vec,
            _layer_spec_single((D_MODEL, 2 * D_FF), layer),
            _layer_spec_single((D_FF, D_MODEL), layer), vec]


def _post_args(x2d, mix3, yc, wts):
    return (x2d, mix3, yc, wts["w_out"], wts["g_mix_post"], wts["g_ffn_pre"], wts["w_gu"], wts["w_down"],
            wts["g_ffn_post"])


def _post_attn_kernel(pt_ref, x_ref, mix_ref, yc_ref, wo_ref, gpost_ref, gfpre_ref, wgu_ref, wd_ref, gfpost_ref,
                      q_ref, kn_ref, vn_ref, lq1_ref, lk1_ref, lq2_ref, lk2_ref, subg_ref, xs_ref, mixs_ref, kc_hbm,
                      vc_hbm, o_ref, oa_ref, xso_ref, hn_s, act_s, kbuf, vbuf, sems, *, layer, db, seqs_per_step,
                      t_new, page, n_chunks, chunk_pages, past_len, lam_init):
    i = pl.program_id(0)
    regions = seqs_per_step * n_chunks
    n_rows, t_row, slope_row = _decode_rows(t_new)

    def page_copies(bb, cc, sl, lookup):
        copies = []
        for idx in range(chunk_pages):
            pid = pt_ref[bb, cc * chunk_pages + idx] if lookup else 0
            copies.append(pltpu.make_async_copy(kc_hbm.at[layer, pid], kbuf.at[sl, idx], sems.at[0, sl]))
            copies.append(pltpu.make_async_copy(vc_hbm.at[layer, pid], vbuf.at[sl, idx], sems.at[1, sl]))
        return copies

    def gather_rows(ref, bb):
        return jnp.concatenate([ref[pl.ds(t * db + bb, 1), :] for t in range(t_new)], axis=0)

    @pl.when(i == 0)
    def _():
        for cp in page_copies(0, 0, 0, True):
            cp.start()

    lam = _diff_lambda(lq1_ref[0], lk1_ref[0], lq2_ref[0], lk2_ref[0], lam_init)
    ffn_order = [list(range(N_FF_CHUNKS))[r::regions] for r in range(regions)]
    x1 = None

    for r in range(seqs_per_step):
        b = i * seqs_per_step + r
        qbd = _decode_queries(gather_rows(q_ref, b), t_new)
        state = (jnp.full((n_rows, 1), -jnp.inf, F32), jnp.zeros((n_rows, 1), F32),
                 jnp.zeros((n_rows, D_GROUP), F32))
        for c in range(n_chunks):
            region = r * n_chunks + c
            slot = lax.rem(i * regions + region, 2)
            if c + 1 < n_chunks:
                for cp in page_copies(b, c + 1, 1 - slot, True):
                    cp.start()
            else:
                @pl.when(b + 1 < db)
                def _():
                    for cp in page_copies(b + 1, 0, 1 - slot, True):
                        cp.start()
            for cp in page_copies(b, c, slot, False):
                cp.wait()
            k_pages = [kbuf.at[slot, idx] for idx in range(chunk_pages)]
            v_pages = [vbuf.at[slot, idx] for idx in range(chunk_pages)]
            m_new, l_new, alpha, pb = _decode_scores(qbd, k_pages, state, c * chunk_pages * page, past_len + t_row,
                                                     slope_row)
            if region == 0:
                x1 = _post_head(x_ref, mix_ref, yc_ref, wo_ref, gpost_ref, gfpre_ref, hn_s)
            state = (m_new, l_new, _decode_values(pb, alpha, v_pages, state, page))
            for f in ffn_order[region]:
                _ffn_chunk(f, hn_s, wgu_ref, act_s)
        y = _decode_finish(qbd, state, gather_rows(kn_ref, b), gather_rows(vn_ref, b), lam, subg_ref[0], t_row,
                           slope_row, t_new, lam_init)
        for t in range(t_new):
            oa_ref[pl.ds(t * db + b, 1), :] = y[t:t + 1, :]

    _post_tail(x1, act_s, wd_ref, gfpost_ref, o_ref)

    @pl.when(i == pl.num_programs(0) - 1)
    def _():
        n_s = xs_ref.shape[0]
        hn_v, act_v = hn_s.at[pl.ds(0, n_s)], act_s.at[pl.ds(0, n_s)]
        x1s = _post_head(xs_ref, mixs_ref, oa_ref, wo_ref, gpost_ref, gfpre_ref, hn_v)
        for f in range(N_FF_CHUNKS):
            _ffn_chunk(f, hn_v, wgu_ref, act_v)
        _post_tail(x1s, act_v, wd_ref, gfpost_ref, xso_ref)


def _post_attn(x2d, mix3, yc, xs, mix3_s, page_table, q, kn, vn, cache_kt, cache_vt, wts, layer, *, tm, db, t_new,
               lam_init):
    m = x2d.shape[0]
    n_steps = m // tm
    n_pages = page_table.shape[1]
    page = cache_kt.shape[-1]
    chunk_pages = min(DECODE_CHUNK_PAGES, n_pages)
    ms = db * t_new
    assert db % n_steps == 0 and n_pages % chunk_pages == 0 and ms <= tm
    row = lambda i, pt: (i, 0)
    full = lambda i, pt: (0, 0)
    lvec = _layer_spec((1, DK), layer)
    in_specs = _post_specs(tm, layer, row)
    in_specs += [pl.BlockSpec((ms, D_GROUP), full)] * 3 + [lvec] * 4 + [_layer_spec((1, D_GROUP), layer)]
    in_specs += [pl.BlockSpec((ms, D_MODEL), full), pl.BlockSpec((ms, 3 * D_GROUP), full)]
    in_specs += [pl.BlockSpec(memory_space=pl.ANY)] * 2
    page_buf = pltpu.VMEM((2, chunk_pages, D_GROUP, page), F32)
    grid_spec = pltpu.PrefetchScalarGridSpec(
        num_scalar_prefetch=1,
        grid=(n_steps,),
        in_specs=in_specs,
        out_specs=[pl.BlockSpec((tm, D_MODEL), row), pl.BlockSpec((ms, D_GROUP), full),
                   pl.BlockSpec((ms, D_MODEL), full)],
        scratch_shapes=[pltpu.VMEM((tm, D_MODEL), BF16), pltpu.VMEM((tm, D_FF), BF16), page_buf, page_buf,
                        pltpu.SemaphoreType.DMA((2, 2))],
    )
    x_new, _, xs_new = pl.pallas_call(
        functools.partial(_post_attn_kernel, layer=layer, db=db, seqs_per_step=db // n_steps, t_new=t_new, page=page,
                          n_chunks=n_pages // chunk_pages, chunk_pages=chunk_pages, past_len=n_pages * page,
                          lam_init=lam_init),
        grid_spec=grid_spec,
        out_shape=[jax.ShapeDtypeStruct((m, D_MODEL), F32), jax.ShapeDtypeStruct((ms, D_GROUP), F32),
                   jax.ShapeDtypeStruct((ms, D_MODEL), F32)],
        compiler_params=pltpu.CompilerParams(
            dimension_semantics=("arbitrary",), vmem_limit_bytes=VMEM_LIMIT_BYTES),
        name="post_attn",
    )(page_table, *_post_args(x2d, mix3, yc, wts), q, kn, vn, wts["lambda_q1"], wts["lambda_k1"], wts["lambda_q2"],
      wts["lambda_k2"], wts["attn_sub_g4"], xs, mix3_s, cache_kt, cache_vt)
    return x_new, xs_new


def _prepare_weights(w_in, w_out, conv_w, conv_b, conv_ln_g, conv_ln_b, sgu_w, sgu_b, sgu_ln_g, sgu_ln_b,
                     lambda_q1, lambda_k1, lambda_q2, lambda_k2, attn_sub_g, pool_w, pool_scale, g_mix_pre,
                     g_mix_post, g_ffn_pre, g_ffn_post, w_gu, w_down, t_new):
    depth = w_in.shape[0]
    hd = D_GROUP // SGU_HEADS
    sgu_w4 = jnp.repeat(jnp.transpose(sgu_w[:, :, :t_new, :t_new], (0, 2, 3, 1)), hd, axis=3)
    eye = jnp.eye(len(POOL_WINDOWS), dtype=F32)
    row = lambda a: a[:, None, :]
    return {
        "w_in": w_in.astype(BF16), "w_out": w_out.astype(BF16), "w_gu": w_gu.astype(BF16),
        "w_down": w_down.astype(BF16),
        "conv_w": conv_w, "conv_b": row(conv_b), "conv_ln_g": row(conv_ln_g), "conv_ln_b": row(conv_ln_b),
        "sgu_w": sgu_w,
        "sgu_b_full": jnp.repeat(jnp.swapaxes(sgu_b, 1, 2), hd, axis=2),
        "sgu_w4": sgu_w4.reshape(depth, t_new * t_new, D_GROUP),
        "sgu_ln_g": row(sgu_ln_g), "sgu_ln_b": row(sgu_ln_b),
        "lambda_q1": row(lambda_q1), "lambda_k1": row(lambda_k1), "lambda_q2": row(lambda_q2),
        "lambda_k2": row(lambda_k2),
        "attn_sub_g_col": attn_sub_g[:, :, None], "attn_sub_g4": jnp.tile(row(attn_sub_g), (1, 1, N_HEADS)),
        "pool_w_bd": jnp.einsum("lgce,gh->lgche", pool_w, eye).reshape(depth, D_GROUP, D_GROUP).astype(BF16),
        "pool_scale": row(pool_scale),
        "g_mix_pre": row(g_mix_pre), "g_mix_post": row(g_mix_post), "g_ffn_pre": row(g_ffn_pre),
        "g_ffn_post": row(g_ffn_post),
    }


def kernel(x_prompt, x_sample, cache_k, cache_v, state_conv, state_pool, page_table, w_in, w_out, conv_w, conv_b, conv_ln_g, conv_ln_b, sgu_w, sgu_b, sgu_ln_g, sgu_ln_b, lambda_q1, lambda_k1, lambda_q2, lambda_k2, attn_sub_g, pool_w, pool_scale, g_mix_pre, g_mix_post, g_ffn_pre, g_ffn_post, w_gu, w_down):
    batch, seq, _ = x_prompt.shape
    db, t_new, _ = x_sample.shape
    depth, n_pool, page = cache_k.shape[0], cache_k.shape[1], cache_k.shape[2]
    past_len = page_table.shape[1] * page
    assert seq % ROW_TILE == 0 and seq % SEQ_TILE == 0 and db % SUBLANES == 0
    assert seq % ATT_Q_TILE == 0 and ATT_Q_TILE % ATT_K_TILE == 0
    assert page == LANES

    wts = _prepare_weights(w_in, w_out, conv_w, conv_b, conv_ln_g, conv_ln_b, sgu_w, sgu_b, sgu_ln_g, sgu_ln_b,
                           lambda_q1, lambda_k1, lambda_q2, lambda_k2, attn_sub_g, pool_w, pool_scale,
                           g_mix_pre, g_mix_post, g_ffn_pre, g_ffn_post, w_gu, w_down, t_new)

    cache_kt = jnp.transpose(cache_k, (0, 1, 3, 4, 5, 2)).reshape(depth, n_pool, D_GROUP, page)
    cache_vt = jnp.transpose(cache_v, (0, 1, 3, 4, 2)).reshape(depth, n_pool, D_GROUP, page)
    ms = db * t_new
    xs = jnp.swapaxes(x_sample, 0, 1).reshape(ms, D_MODEL)
    sc_tm = jnp.swapaxes(state_conv, 1, 2)
    sp_tm = jnp.swapaxes(state_pool, 1, 2)
    xp = x_prompt.reshape(batch * seq, D_MODEL)

    kv_all = None
    cp_l, pp_l = [], []
    ks_l, vs_l, cs_l, ps_l, gs_l = [], [], [], [], []
    for l in range(depth):
        lam_init = 0.8 - 0.6 * math.exp(-0.3 * l)
        mix3, k_rows, qt, kt_all, vt_all, conv_new, pool_new = _proj_mix_prompt(xp, wts, kv_all, l, batch=batch,
                                                                                seq=seq)
        kv_all = (kt_all, vt_all)
        yc = _attn_prompt(qt, k_rows, vt_all, wts, l, batch=batch, seq=seq, lam_init=lam_init)
        cp_l.append(conv_new); pp_l.append(pool_new)
        mix3_s, q, kn, vn, conv_new, pool_new, sgu_v = _proj_mix_sample(xs, sc_tm, sp_tm, wts, l, db=db, t_new=t_new,
                                                                        past_len=past_len)
        xp, xs = _post_attn(xp, mix3, yc, xs, mix3_s, page_table, q, kn, vn, cache_kt, cache_vt, wts, l, tm=ROW_TILE,
                            db=db, t_new=t_new, lam_init=lam_init)
        ks_l.append(kn); vs_l.append(vn); cs_l.append(conv_new); ps_l.append(pool_new); gs_l.append(sgu_v)

    kt_all, vt_all = kv_all
    y_prompt = xp.reshape(batch, seq, D_MODEL)
    y_sample = jnp.swapaxes(xs.reshape(t_new, db, D_MODEL), 0, 1)
    k_prompt = jnp.transpose(kt_all.reshape(depth, batch, N_HEADS, 2, DK, seq), (0, 1, 5, 2, 3, 4))
    v_prompt = jnp.transpose(vt_all.reshape(depth, batch, N_HEADS, DV, seq), (0, 1, 4, 2, 3))
    tm_to_bm = lambda a: jnp.swapaxes(jnp.stack(a).reshape(depth, t_new, db, D_GROUP), 1, 2)
    k_sample = tm_to_bm(ks_l).reshape(depth, db, t_new, N_HEADS, 2, DK)
    v_sample = tm_to_bm(vs_l).reshape(depth, db, t_new, N_HEADS, DV)
    conv_sample = jnp.swapaxes(jnp.stack(cs_l), 1, 2)
    pool_sample = jnp.swapaxes(jnp.stack(ps_l), 1, 2)
    sgu_v_sample = tm_to_bm(gs_l)
    return (y_prompt, y_sample, k_prompt, v_prompt, jnp.stack(cp_l), jnp.stack(pp_l),
            k_sample, v_sample, conv_sample, pool_sample, sgu_v_sample)
```

```python
import functools
import math

import jax
import jax.numpy as jnp
from jax import lax
from jax.experimental import pallas as pl
from jax.experimental.pallas import tpu as pltpu

F32 = jnp.float32
BF16 = jnp.bfloat16

D_MODEL = 1024
D_GROUP = 256
D_IN = 8 * D_GROUP
CONV_W = 31
CONV_HALO = CONV_W - 1
SGU_CHUNK = 128
SGU_HEADS = 4
N_HEADS = 4
DK = 32
DV = 64
POOL_WINDOWS = (2, 4, 8, 16)
POOL_HALO = max(POOL_WINDOWS) - 1
POOL_GD = D_GROUP // len(POOL_WINDOWS)
D_FF = 2816
FF_CHUNK = 256
N_FF_CHUNKS = D_FF // FF_CHUNK
RMS_EPS = 1e-6
LN_EPS = 1e-5
Q_SCALE = DK ** -0.5
LOG2E = math.log2(math.e)

SUBLANES = 8
LANES = 128
VMEM_LIMIT_BYTES = 56 * 1024 * 1024

ROW_TILE = 512
SEQ_TILE = 512
CONV_ROWS = 128
ATT_Q_TILE = 512
ATT_K_TILE = 256
DECODE_CHUNK_PAGES = 32

NT_DIMS = (((1,), (1,)), ((), ()))


def _rms(x, g, eps):
    return x * lax.rsqrt(jnp.mean(x * x, axis=-1, keepdims=True) + eps) * g


def _layer_norm(x, g, b):
    mu = jnp.mean(x, axis=-1, keepdims=True)
    xc = x - mu
    var = jnp.mean(xc * xc, axis=-1, keepdims=True)
    return xc * lax.rsqrt(var + LN_EPS) * g + b


def _gelu(x):
    return 0.5 * x * (1.0 + lax.erf(x * (2.0 ** -0.5)))


def _sigmoid(x):
    return 1.0 / (1.0 + jnp.exp(-x))


def _lane_group_select(pieces, group_width, shape):
    lane = lax.broadcasted_iota(jnp.int32, shape, len(shape) - 1)
    out = pieces[-1]
    for g in range(len(pieces) - 2, -1, -1):
        out = jnp.where(lane < (g + 1) * group_width, pieces[g], out)
    return out


def _layer_spec(shape, layer):
    zeros = (0,) * len(shape)
    return pl.BlockSpec((1,) + tuple(shape), lambda *_: (layer,) + zeros)


def _layer_spec_single(shape, layer):
    zeros = (0,) * len(shape)
    return pl.BlockSpec((1,) + tuple(shape), lambda *_: (layer,) + zeros, pipeline_mode=pl.Buffered(1))


AG_COL, UV_COL, Q_COL, K_COL, V_COL, XD_COL = (0, 2 * D_GROUP, 4 * D_GROUP, 5 * D_GROUP, 6 * D_GROUP, 7 * D_GROUP)


def _conv_post(acc, clg, clb):
    y = _layer_norm(acc, clg, clb)
    return y * _sigmoid(y)


def _pool_select(wins, shape):
    return _lane_group_select(wins, POOL_GD, shape)


HPAD = 32
XPAD = 24


def _proj_mix_kernel(x_ref, g_ref, w_ref, cw_ref, cb_ref, clg_ref, clb_ref, sw_ref, sb_ref, slg_ref, slb_ref,
                     pw_ref, ps_ref, *rest, ts, n_alias):
    (mix_ref, k_ref, qt_ref, kt_ref, vt_ref, convn_ref, pooln_ref, hext, hshift, xext, s2, s4, s8) = rest[n_alias:]
    i = pl.program_id(1)

    @pl.when(i == 0)
    def _():
        hext[0:HPAD, :] = jnp.zeros((HPAD, D_GROUP), F32)
        xext[0:XPAD, :] = jnp.zeros((XPAD, D_GROUP), F32)
        for buf in (s2, s4, s8):
            buf[0:SUBLANES, :] = jnp.zeros((SUBLANES, D_GROUP), F32)

    hn = _rms(x_ref[...], g_ref[0], RMS_EPS).astype(BF16)
    proj = lambda col, width: jnp.dot(hn, w_ref[0, :, col:col + width], preferred_element_type=F32)
    z_ag = proj(AG_COL, 2 * D_GROUP)
    z_uv = proj(UV_COL, 2 * D_GROUP)
    z_qkv = proj(Q_COL, 3 * D_GROUP)
    z_xd = proj(XD_COL, D_GROUP)

    k_rows = z_qkv[:, D_GROUP:2 * D_GROUP]
    k_ref[...] = k_rows
    qt_ref[0] = (z_qkv[:, 0:D_GROUP] * (Q_SCALE * LOG2E)).T.astype(BF16)
    kt_ref[0, 0] = k_rows.T
    vt_ref[0, 0] = z_qkv[:, 2 * D_GROUP:3 * D_GROUP].T

    hext[HPAD:HPAD + ts, :] = z_ag[:, 0:D_GROUP] * _sigmoid(z_ag[:, D_GROUP:2 * D_GROUP])
    slab = ts + HPAD - SUBLANES
    for rho in range(1, SUBLANES):
        hshift[rho - 1] = hext[rho:rho + slab, :]
    for c in range(ts // CONV_ROWS):
        r0 = c * CONV_ROWS
        acc = jnp.broadcast_to(cb_ref[0], (CONV_ROWS, D_GROUP))
        for j in range(CONV_W):
            shift = HPAD - CONV_HALO + j
            q8, rho = shift // SUBLANES * SUBLANES, shift % SUBLANES
            if rho == 0:
                rows = hext[r0 + q8:r0 + q8 + CONV_ROWS, :]
            else:
                rows = hshift[rho - 1, r0 + q8:r0 + q8 + CONV_ROWS, :]
            acc = acc + cw_ref[0, j:j + 1, :] * rows
        mix_ref[r0:r0 + CONV_ROWS, 0:D_GROUP] = _conv_post(acc, clg_ref[0], clb_ref[0]).astype(BF16)

    u = _gelu(z_uv[:, 0:D_GROUP])
    v = _layer_norm(_gelu(z_uv[:, D_GROUP:2 * D_GROUP]), slg_ref[0], slb_ref[0])
    tri_r = lax.broadcasted_iota(jnp.int32, (SGU_CHUNK, SGU_CHUNK), 0)
    tri_c = lax.broadcasted_iota(jnp.int32, (SGU_CHUNK, SGU_CHUNK), 1)
    w_heads = [jnp.where(tri_c <= tri_r, sw_ref[0, g], 0.0).astype(BF16) for g in range(SGU_HEADS)]
    w_stack = jnp.concatenate(w_heads, axis=0)
    hd = D_GROUP // SGU_HEADS
    for c in range(ts // SGU_CHUNK):
        rows = slice(c * SGU_CHUNK, (c + 1) * SGU_CHUNK)
        r = jnp.dot(w_stack, v[rows].astype(BF16), preferred_element_type=F32)
        mixed = _lane_group_select([r[g * SGU_CHUNK:(g + 1) * SGU_CHUNK] for g in range(SGU_HEADS)],
                                   hd, (SGU_CHUNK, D_GROUP))
        mix_ref[rows, D_GROUP:2 * D_GROUP] = (u[rows] * (mixed + sb_ref[0])).astype(BF16)

    n = XPAD + ts
    xext[XPAD:n, :] = z_xd
    s2[SUBLANES:n, :] = xext[SUBLANES:n, :] + xext[SUBLANES - 1:n - 1, :]
    s4[SUBLANES:n, :] = s2[SUBLANES:n, :] + s2[SUBLANES - 2:n - 2, :]
    s8[2 * SUBLANES:n, :] = s4[2 * SUBLANES:n, :] + s4[2 * SUBLANES - 4:n - 4, :]
    win16 = s8[XPAD:n, :] + s8[XPAD - 8:n - 8, :]
    x0 = xext[XPAD:n, :]
    wins = _pool_select([s2[XPAD:n, :], s4[XPAD:n, :], s8[XPAD:n, :], win16], (ts, D_GROUP))
    wvec = _pool_select([jnp.full((1, D_GROUP), float(w), F32) for w in POOL_WINDOWS], (1, D_GROUP))
    pos = (i * ts + lax.broadcasted_iota(jnp.int32, (ts, D_GROUP), 0)).astype(F32)
    d = wins / jnp.minimum(pos + 1.0, wvec) - x0
    y = jnp.dot(d.astype(BF16), pw_ref[0], preferred_element_type=F32) * ps_ref[0]
    mix_ref[:, 2 * D_GROUP:3 * D_GROUP] = y.astype(BF16)

    hext[0:HPAD, :] = hext[ts:ts + HPAD, :]
    xext[SUBLANES:XPAD, :] = xext[ts + SUBLANES:ts + XPAD, :]

    @pl.when(i == pl.num_programs(1) - 1)
    def _():
        convn_ref[0] = hext[HPAD - CONV_HALO:HPAD, :]
        pooln_ref[0] = xext[XPAD - POOL_HALO:XPAD, :]


def _proj_mix_prompt(x2d, wts, kv_prev, layer, *, batch, seq):
    ts = SEQ_TILE
    ns = seq // ts
    m = batch * seq
    depth = wts["w_in"].shape[0]
    row_map = lambda b, i: (b * ns + i, 0)
    vec = _layer_spec((1, D_GROUP), layer)
    ext = lambda pad: pltpu.VMEM((pad + ts, D_GROUP), F32)
    in_specs = [pl.BlockSpec((ts, D_MODEL), row_map), _layer_spec((1, D_MODEL), layer),
                _layer_spec_single((D_MODEL, D_IN), layer),
                _layer_spec((CONV_W, D_GROUP), layer), vec, vec, vec,
                _layer_spec((SGU_HEADS, SGU_CHUNK, SGU_CHUNK), layer),
                _layer_spec((SGU_CHUNK, D_GROUP), layer), vec, vec,
                _layer_spec((D_GROUP, D_GROUP), layer), vec]
    args = [x2d, wts["g_mix_pre"], wts["w_in"], wts["conv_w"], wts["conv_b"], wts["conv_ln_g"], wts["conv_ln_b"],
            wts["sgu_w"], wts["sgu_b_full"], wts["sgu_ln_g"], wts["sgu_ln_b"], wts["pool_w_bd"], wts["pool_scale"]]
    kv_spec = pl.BlockSpec((1, 1, D_GROUP, ts), lambda b, i: (layer, b, 0, i))
    kv_shape = jax.ShapeDtypeStruct((depth, batch, D_GROUP, seq), F32)
    out_specs = [pl.BlockSpec((ts, 3 * D_GROUP), row_map), pl.BlockSpec((ts, D_GROUP), row_map),
                 pl.BlockSpec((1, D_GROUP, ts), lambda b, i: (b, 0, i)), kv_spec, kv_spec,
                 pl.BlockSpec((1, CONV_HALO, D_GROUP), lambda b, i: (b, 0, 0)),
                 pl.BlockSpec((1, POOL_HALO, D_GROUP), lambda b, i: (b, 0, 0))]
    out_shape = [jax.ShapeDtypeStruct((m, 3 * D_GROUP), BF16), jax.ShapeDtypeStruct((m, D_GROUP), F32),
                 jax.ShapeDtypeStruct((batch, D_GROUP, seq), BF16), kv_shape, kv_shape,
                 jax.ShapeDtypeStruct((batch, CONV_HALO, D_GROUP), F32),
                 jax.ShapeDtypeStruct((batch, POOL_HALO, D_GROUP), F32)]
    kv_out_index = 3
    aliases = {}
    n_alias = 0
    if kv_prev is not None:
        n_alias = len(kv_prev)
        for a, buf in enumerate(kv_prev):
            aliases[len(args)] = kv_out_index + a
            in_specs.append(pl.BlockSpec(memory_space=pl.ANY))
            args.append(buf)
    return pl.pallas_call(
        functools.partial(_proj_mix_kernel, ts=ts, n_alias=n_alias),
        grid=(batch, ns),
        in_specs=in_specs,
        out_specs=out_specs,
        out_shape=out_shape,
        input_output_aliases=aliases,
        scratch_shapes=[ext(HPAD), pltpu.VMEM((SUBLANES - 1, ts + HPAD - SUBLANES, D_GROUP), F32),
                        ext(XPAD), ext(XPAD), ext(XPAD), ext(XPAD)],
        compiler_params=pltpu.CompilerParams(
            dimension_semantics=("arbitrary", "arbitrary"), vmem_limit_bytes=VMEM_LIMIT_BYTES),
        name="proj_mix_prompt",
    )(*args)


def _proj_mix_sample_kernel(x_ref, g_ref, w_ref, sc_ref, sp_ref, cw_ref, cb_ref, clg_ref, clb_ref, sw4_ref, sb_ref,
                            slg_ref, slb_ref, pw_ref, ps_ref, mix_ref, q_ref, k_ref, v_ref, convn_ref, pooln_ref,
                            sguv_ref, *, db, t_new, past_len):
    rows = lambda t: slice(t * db, (t + 1) * db)
    h = _rms(x_ref[...], g_ref[0], RMS_EPS).astype(BF16)
    z = jnp.dot(h, w_ref[0], preferred_element_type=F32)
    q_ref[...] = z[:, Q_COL:Q_COL + D_GROUP]
    k_ref[...] = z[:, K_COL:K_COL + D_GROUP]
    v_ref[...] = z[:, V_COL:V_COL + D_GROUP]

    ext = [sc_ref[0, j] for j in range(CONV_HALO)]
    ext += [z[rows(t), AG_COL:AG_COL + D_GROUP] * _sigmoid(z[rows(t), AG_COL + D_GROUP:AG_COL + 2 * D_GROUP])
            for t in range(t_new)]
    for t in range(t_new):
        acc = jnp.broadcast_to(cb_ref[0], (db, D_GROUP))
        for j in range(CONV_W):
            acc = acc + cw_ref[0, j:j + 1, :] * ext[t + j]
        mix_ref[rows(t), 0:D_GROUP] = _conv_post(acc, clg_ref[0], clb_ref[0]).astype(BF16)
    for j in range(CONV_HALO):
        convn_ref[j] = ext[t_new + j]

    u = _gelu(z[:, UV_COL:UV_COL + D_GROUP])
    v = _layer_norm(_gelu(z[:, UV_COL + D_GROUP:UV_COL + 2 * D_GROUP]), slg_ref[0], slb_ref[0])
    sguv_ref[...] = v
    for t in range(t_new):
        mixed = jnp.broadcast_to(sb_ref[0, t:t + 1, :], (db, D_GROUP))
        for s in range(t + 1):
            mixed = mixed + sw4_ref[0, t * t_new + s:t * t_new + s + 1, :] * v[rows(s)]
        mix_ref[rows(t), D_GROUP:2 * D_GROUP] = (u[rows(t)] * mixed).astype(BF16)

    pext = [sp_ref[0, j] for j in range(POOL_HALO)] + [z[rows(t), XD_COL:XD_COL + D_GROUP] for t in range(t_new)]
    ds = []
    for t in range(t_new):
        x0 = pext[POOL_HALO + t]
        acc = x0
        wins = []
        for k in range(1, POOL_HALO + 1):
            acc = acc + pext[POOL_HALO + t - k]
            if k + 1 in POOL_WINDOWS:
                wins.append(acc)
        cnts = [jnp.full((1, D_GROUP), float(min(past_len + t + 1, w)), F32) for w in POOL_WINDOWS]
        cnt = _pool_select(cnts, (1, D_GROUP))
        ds.append(_pool_select(wins, (db, D_GROUP)) / cnt - x0)
    d = jnp.concatenate(ds, axis=0)
    y = jnp.dot(d.astype(BF16), pw_ref[0], preferred_element_type=F32) * ps_ref[0]
    mix_ref[:, 2 * D_GROUP:3 * D_GROUP] = y.astype(BF16)
    for j in range(POOL_HALO):
        pooln_ref[j] = pext[t_new + j]


def _proj_mix_sample(xs, sc_tm, sp_tm, wts, layer, *, db, t_new, past_len):
    m = db * t_new
    full = lambda shape: pl.BlockSpec(shape, lambda i: (0,) * len(shape))
    vec = _layer_spec((1, D_GROUP), layer)
    act = jax.ShapeDtypeStruct((m, D_GROUP), F32)
    return pl.pallas_call(
        functools.partial(_proj_mix_sample_kernel, db=db, t_new=t_new, past_len=past_len),
        grid=(1,),
        in_specs=[full((m, D_MODEL)), _layer_spec((1, D_MODEL), layer), _layer_spec((D_MODEL, D_IN), layer),
                  _layer_spec((CONV_HALO, db, D_GROUP), layer), _layer_spec((POOL_HALO, db, D_GROUP), layer),
                  _layer_spec((CONV_W, D_GROUP), layer), vec, vec, vec,
                  _layer_spec((t_new * t_new, D_GROUP), layer),
                  _layer_spec((SGU_CHUNK, D_GROUP), layer), vec, vec,
                  _layer_spec((D_GROUP, D_GROUP), layer), vec],
        out_specs=[full((m, 3 * D_GROUP)), full((m, D_GROUP)), full((m, D_GROUP)), full((m, D_GROUP)),
                   full((CONV_HALO, db, D_GROUP)), full((POOL_HALO, db, D_GROUP)), full((m, D_GROUP))],
        out_shape=[jax.ShapeDtypeStruct((m, 3 * D_GROUP), BF16), act, act, act,
                   jax.ShapeDtypeStruct((CONV_HALO, db, D_GROUP), F32),
                   jax.ShapeDtypeStruct((POOL_HALO, db, D_GROUP), F32), act],
        compiler_params=pltpu.CompilerParams(
            dimension_semantics=("arbitrary",), vmem_limit_bytes=VMEM_LIMIT_BYTES),
        name="proj_mix_sample",
    )(xs, wts["g_mix_pre"], wts["w_in"], sc_tm, sp_tm, wts["conv_w"], wts["conv_b"], wts["conv_ln_g"],
      wts["conv_ln_b"], wts["sgu_w4"], wts["sgu_b_full"], wts["sgu_ln_g"], wts["sgu_ln_b"], wts["pool_w_bd"],
      wts["pool_scale"])


def _diff_lambda(lq1, lk1, lq2, lk2, lam_init):
    s1 = jnp.sum(lq1 * lk1, axis=-1, keepdims=True)
    s2 = jnp.sum(lq2 * lk2, axis=-1, keepdims=True)
    return jnp.exp(s1) - jnp.exp(s2) + lam_init


def _alibi_slope(h):
    return 2.0 ** (-8.0 * (h + 1) / N_HEADS)


KEY_AUG = LANES
VAL_AUG = 80
N_BIAS = 3


def _attn_prompt_kernel(qt_ref, k_ref, vt_ref, lq1_ref, lk1_ref, lq2_ref, lk2_ref, subg_ref, o_ref, kaug, vaug, *,
                        tq, tk, nblk, lam_init):
    i = pl.program_id(1)
    hd = 2 * DK
    diag_blocks = tq // tk

    @pl.when(i == 0)
    def _():
        key = lax.broadcasted_iota(jnp.int32, (tk, KEY_AUG), 0).astype(F32)
        lane = lax.broadcasted_iota(jnp.int32, (tk, KEY_AUG), 1)
        ones_row = jnp.where(lax.broadcasted_iota(jnp.int32, (VAL_AUG - hd, tk), 0) == 0, 1.0, 0.0).astype(BF16)
        for h in range(N_HEADS):
            bias = (_alibi_slope(h) * LOG2E) * key
            hi = bias.astype(BF16).astype(F32)
            mid = (bias - hi).astype(BF16).astype(F32)
            lo = ((bias - hi) - mid).astype(BF16).astype(F32)
            aug = jnp.where(lane == hd, hi, jnp.where(lane == hd + 1, mid, jnp.where(lane == hd + 2, lo, 0.0)))
            for j in range(nblk):
                kblk = k_ref[j * tk:(j + 1) * tk, h * hd:(h + 1) * hd]
                kpad = jnp.concatenate([kblk, jnp.zeros((tk, KEY_AUG - hd), F32)], axis=1)
                kaug[j, h] = jnp.where(lane < hd, kpad, aug).astype(BF16)
                vaug[j, h, 0:hd, :] = vt_ref[0, 0, h * hd:(h + 1) * hd, j * tk:(j + 1) * tk].astype(BF16)
                vaug[j, h, hd:VAL_AUG, :] = ones_row

    lam = _diff_lambda(lq1_ref[0], lk1_ref[0], lq2_ref[0], lk2_ref[0], lam_init)
    qt = qt_ref[0]
    zero = jnp.zeros((DK, tq), BF16)
    ones_aug = jnp.where(lax.broadcasted_iota(jnp.int32, (KEY_AUG - hd, 2 * tq), 0) < N_BIAS, 1.0, 0.0).astype(BF16)
    qs = []
    for h in range(N_HEADS):
        q1 = qt[h * hd:h * hd + DK]
        q2 = qt[h * hd + DK:(h + 1) * hd]
        body = jnp.concatenate([jnp.concatenate([q1, zero], axis=0), jnp.concatenate([zero, q2], axis=0)], axis=1)
        qs.append(jnp.concatenate([body, ones_aug], axis=0))
    lower = (lax.broadcasted_iota(jnp.int32, (tk, tk), 0) <= lax.broadcasted_iota(jnp.int32, (tk, tk), 1))

    n_cb = 2 * diag_blocks

    def live_cols(d):
        return [cb for cb in range(n_cb) if d is None or cb % diag_blocks >= d]

    def take(x, cbs):
        return x if len(cbs) == n_cb else jnp.concatenate([x[:, cb * tk:(cb + 1) * tk] for cb in cbs], axis=1)

    def put(full, part, cbs):
        if len(cbs) == n_cb:
            return part
        pieces = [part[:, cbs.index(cb) * tk:(cbs.index(cb) + 1) * tk] if cb in cbs else full[:, cb * tk:(cb + 1) * tk]
                  for cb in range(n_cb)]
        return jnp.concatenate(pieces, axis=1)

    def causal_mask(s, d, cbs):
        pieces = [s[:, k * tk:(k + 1) * tk] for k in range(len(cbs))]
        pieces = [jnp.where(lower, blk, -jnp.inf) if cb % diag_blocks == d else blk for blk, cb in zip(pieces, cbs)]
        return jnp.concatenate(pieces, axis=1)

    def blocks(js, carry, diags):
        cols = [live_cols(d) for d in diags]
        scores = [[jnp.dot(kaug[j, h], take(qs[h], cols[n]), preferred_element_type=F32) for h in range(N_HEADS)]
                  for n, j in enumerate(js)]
        stats = [carry[h][0] for h in range(N_HEADS)]
        probs = []
        for n, j in enumerate(js):
            row = []
            for h in range(N_HEADS):
                s = scores[n][h] if diags[n] is None else causal_mask(scores[n][h], diags[n], cols[n])
                c = (_alibi_slope(h) * LOG2E) * (j * tk - i * tq).astype(F32)
                m = take(stats[h], cols[n])
                m_new = jnp.maximum(m, jnp.max(s, axis=0, keepdims=True) + c)
                row.append((jnp.exp2(m - m_new), jnp.exp2(s - (m_new - c)).astype(BF16)))
                stats[h] = put(stats[h], m_new, cols[n])
            probs.append(row)
        accs = [carry[h][1] for h in range(N_HEADS)]
        for n, j in enumerate(js):
            for h in range(N_HEADS):
                alpha, p = probs[n][h]
                part = alpha * take(accs[h], cols[n]) + jnp.dot(vaug[j, h], p, preferred_element_type=F32)
                accs[h] = put(accs[h], part, cols[n])
        return tuple((stats[h], accs[h]) for h in range(N_HEADS))

    init = tuple((jnp.full((1, 2 * tq), -jnp.inf, F32), jnp.zeros((VAL_AUG, 2 * tq), F32)) for _ in range(N_HEADS))
    pair = lambda jj, carry: blocks([jj * diag_blocks + d for d in range(diag_blocks)], carry, [None] * diag_blocks)
    carry = lax.fori_loop(0, i, pair, init)
    carry = blocks([i * diag_blocks + d for d in range(diag_blocks)], carry, list(range(diag_blocks)))
    ys = []
    for h in range(N_HEADS):
        acc = carry[h][1]
        o = acc[0:hd, :] * (1.0 / acc[hd:hd + 1, :])
        od = o[:, 0:tq] - lam * o[:, tq:2 * tq]
        ms = jnp.mean(od * od, axis=0, keepdims=True)
        ys.append(od * lax.rsqrt(ms + LN_EPS) * subg_ref[0] * (1.0 - lam_init))
    o_ref[...] = jnp.concatenate(ys, axis=0).T.astype(BF16)


def _attn_prompt(qt, k_rows, vt_all, wts, layer, *, batch, seq, lam_init):
    tq, tk = ATT_Q_TILE, ATT_K_TILE
    ns = seq // tq
    nk = seq // tk
    lvec = _layer_spec((1, DK), layer)
    return pl.pallas_call(
        functools.partial(_attn_prompt_kernel, tq=tq, tk=tk, nblk=nk, lam_init=lam_init),
        grid=(batch, ns),
        in_specs=[pl.BlockSpec((1, D_GROUP, tq), lambda b, i: (b, 0, i)),
                  pl.BlockSpec((seq, D_GROUP), lambda b, i: (b, 0)),
                  pl.BlockSpec((1, 1, D_GROUP, seq), lambda b, i: (layer, b, 0, 0)),
                  lvec, lvec, lvec, lvec, _layer_spec((DV, 1), layer)],
        out_specs=pl.BlockSpec((tq, D_GROUP), lambda b, i: (b * ns + i, 0)),
        out_shape=jax.ShapeDtypeStruct((batch * seq, D_GROUP), BF16),
        scratch_shapes=[pltpu.VMEM((nk, N_HEADS, tk, KEY_AUG), BF16), pltpu.VMEM((nk, N_HEADS, VAL_AUG, tk), BF16)],
        compiler_params=pltpu.CompilerParams(
            dimension_semantics=("arbitrary", "arbitrary"), vmem_limit_bytes=VMEM_LIMIT_BYTES),
        name="attn_prompt",
    )(qt, k_rows, vt_all, wts["lambda_q1"], wts["lambda_k1"], wts["lambda_q2"], wts["lambda_k2"],
      wts["attn_sub_g_col"])


def _decode_rows(t_new):
    n_rows = N_HEADS * 2 * t_new
    row1 = lax.broadcasted_iota(jnp.int32, (n_rows, 1), 0)
    t_row = (row1 % t_new).astype(F32)
    slope_row = jnp.zeros((n_rows, 1), F32)
    for h in range(N_HEADS):
        slope_row = jnp.where(row1 // (2 * t_new) == h, _alibi_slope(h), slope_row)
    return n_rows, t_row, slope_row


def _decode_queries(q4, t_new):
    n_rows = N_HEADS * 2 * t_new
    qt = jnp.concatenate([q4 * Q_SCALE] * (2 * N_HEADS), axis=0)
    r = lax.broadcasted_iota(jnp.int32, (n_rows, D_GROUP), 0)
    c = lax.broadcasted_iota(jnp.int32, (n_rows, D_GROUP), 1)
    return jnp.where(c // DK == r // t_new, qt, 0.0).astype(BF16)


def _decode_scores(qbd, k_pages, state, first_pos, q_pos, slope_row):
    m_prev, l_prev, _ = state
    s = jnp.concatenate([jnp.dot(qbd, kp[...].astype(BF16), preferred_element_type=F32) for kp in k_pages], axis=1)
    kpos = (first_pos + lax.broadcasted_iota(jnp.int32, s.shape, 1)).astype(F32)
    s = s - slope_row * (q_pos - kpos)
    m_new = jnp.maximum(m_prev, jnp.max(s, axis=-1, keepdims=True))
    alpha = jnp.exp(m_prev - m_new)
    p = jnp.exp(s - m_new)
    l_new = alpha * l_prev + jnp.sum(p, axis=-1, keepdims=True)
    return m_new, l_new, alpha, p.astype(BF16)


def _decode_values(pb, alpha, v_pages, state, page):
    acc_prev = state[2]
    pv = jnp.zeros(acc_prev.shape, F32)
    for idx, vp in enumerate(v_pages):
        pv = pv + jnp.dot(pb[:, idx * page:(idx + 1) * page], vp[...].T.astype(BF16), preferred_element_type=F32)
    return alpha * acc_prev + pv


def _decode_finish(qbd, state, kn, vn, lam, subg, t_row, slope_row, t_new, lam_init):
    m_prev, l_prev, acc_prev = state
    n_rows = qbd.shape[0]
    kn8 = jnp.concatenate([kn, jnp.zeros((SUBLANES - t_new, D_GROUP), F32)], axis=0).astype(BF16)
    sn = lax.dot_general(qbd, kn8, NT_DIMS, preferred_element_type=F32)
    dist = t_row - lax.broadcasted_iota(jnp.int32, (n_rows, SUBLANES), 1).astype(F32)
    sn = jnp.where(dist >= 0.0, sn - slope_row * dist, -jnp.inf)
    m_fin = jnp.maximum(m_prev, jnp.max(sn, axis=-1, keepdims=True))
    alpha = jnp.exp(m_prev - m_fin)
    pn = jnp.exp(sn - m_fin)
    l_fin = alpha * l_prev + jnp.sum(pn, axis=-1, keepdims=True)
    acc = alpha * acc_prev
    for t in range(t_new):
        acc = acc + pn[:, t:t + 1] * vn[t:t + 1, :]
    o = acc / l_fin
    r = lax.broadcasted_iota(jnp.int32, (n_rows, D_GROUP), 0)
    c = lax.broadcasted_iota(jnp.int32, (n_rows, D_GROUP), 1)
    o = jnp.where(c // DV == r // (2 * t_new), o, 0.0)
    y0 = jnp.zeros((t_new, D_GROUP), F32)
    y1 = jnp.zeros((t_new, D_GROUP), F32)
    for h in range(N_HEADS):
        y0 = y0 + o[h * 2 * t_new:h * 2 * t_new + t_new]
        y1 = y1 + o[h * 2 * t_new + t_new:(h + 1) * 2 * t_new]
    od = y0 - lam * y1
    seg_r = lax.broadcasted_iota(jnp.int32, (D_GROUP, D_GROUP), 0) // DV
    seg_c = lax.broadcasted_iota(jnp.int32, (D_GROUP, D_GROUP), 1) // DV
    seg = jnp.where(seg_r == seg_c, 1.0 / DV, 0.0).astype(F32)
    ms = jnp.dot(od * od, seg, preferred_element_type=F32, precision=lax.Precision.HIGHEST)
    return od * lax.rsqrt(ms + LN_EPS) * subg * (1.0 - lam_init)


def _post_head(x_ref, mix_ref, yc_ref, wo_ref, gpost_ref, gfpre_ref, hn_s):
    ab, cd = 2 * D_GROUP, 3 * D_GROUP
    mixo = jnp.dot(mix_ref[:, 0:ab], wo_ref[0, 0:ab, :], preferred_element_type=F32)
    mixo = mixo + jnp.dot(yc_ref[...].astype(BF16), wo_ref[0, ab:cd, :], preferred_element_type=F32)
    mixo = mixo + jnp.dot(mix_ref[:, ab:cd], wo_ref[0, cd:D_MODEL, :], preferred_element_type=F32)
    x1 = x_ref[...] + _rms(mixo, gpost_ref[0], RMS_EPS)
    hn_s[...] = _rms(x1, gfpre_ref[0], RMS_EPS).astype(BF16)
    return x1


def _ffn_chunk(c, hn_s, wgu_ref, act_s):
    lo = c * FF_CHUNK
    gate = jnp.dot(hn_s[...], wgu_ref[0, :, lo:lo + FF_CHUNK], preferred_element_type=F32)
    up = jnp.dot(hn_s[...], wgu_ref[0, :, D_FF + lo:D_FF + lo + FF_CHUNK], preferred_element_type=F32)
    act_s[:, lo:lo + FF_CHUNK] = (gate * _sigmoid(gate) * up).astype(BF16)


def _post_tail(x1, act_s, wd_ref, gfpost_ref, o_ref):
    ffn = jnp.dot(act_s[...], wd_ref[0], preferred_element_type=F32)
    o_ref[...] = x1 + _rms(ffn, gfpost_ref[0], RMS_EPS)


def _post_specs(tm, layer, row):
    vec = _layer_spec((1, D_MODEL), layer)
    return [pl.BlockSpec((tm, D_MODEL), row),
            pl.BlockSpec((tm, 3 * D_GROUP), row),
            pl.BlockSpec((tm, D_GROUP), row),
            _layer_spec_single((D_MODEL, D_MODEL), layer), vec, vec,
            _layer_spec_single((D_MODEL, 2 * D_FF), layer),
            _layer_spec_single((D_FF, D_MODEL), layer), vec]


def _post_args(x2d, mix3, yc, wts):
    return (x2d, mix3, yc, wts["w_out"], wts["g_mix_post"], wts["g_ffn_pre"], wts["w_gu"], wts["w_down"],
            wts["g_ffn_post"])


def _post_attn_kernel(pt_ref, x_ref, mix_ref, yc_ref, wo_ref, gpost_ref, gfpre_ref, wgu_ref, wd_ref, gfpost_ref,
                      q_ref, kn_ref, vn_ref, lq1_ref, lk1_ref, lq2_ref, lk2_ref, subg_ref, xs_ref, mixs_ref, kc_hbm,
                      vc_hbm, o_ref, oa_ref, xso_ref, hn_s, act_s, kbuf, vbuf, sems, *, layer, db, seqs_per_step,
                      t_new, page, n_chunks, chunk_pages, past_len, lam_init):
    i = pl.program_id(0)
    regions = seqs_per_step * n_chunks
    n_rows, t_row, slope_row = _decode_rows(t_new)

    def page_copies(bb, cc, sl, lookup):
        copies = []
        for idx in range(chunk_pages):
            pid = pt_ref[bb, cc * chunk_pages + idx] if lookup else 0
            copies.append(pltpu.make_async_copy(kc_hbm.at[layer, pid], kbuf.at[sl, idx], sems.at[0, sl]))
            copies.append(pltpu.make_async_copy(vc_hbm.at[layer, pid], vbuf.at[sl, idx], sems.at[1, sl]))
        return copies

    def gather_rows(ref, bb):
        return jnp.concatenate([ref[pl.ds(t * db + bb, 1), :] for t in range(t_new)], axis=0)

    @pl.when(i == 0)
    def _():
        for cp in page_copies(0, 0, 0, True):
            cp.start()

    lam = _diff_lambda(lq1_ref[0], lk1_ref[0], lq2_ref[0], lk2_ref[0], lam_init)
    ffn_order = [list(range(N_FF_CHUNKS))[r::regions] for r in range(regions)]
    x1 = None

    for r in range(seqs_per_step):
        b = i * seqs_per_step + r
        qbd = _decode_queries(gather_rows(q_ref, b), t_new)
        state = (jnp.full((n_rows, 1), -jnp.inf, F32), jnp.zeros((n_rows, 1), F32),
                 jnp.zeros((n_rows, D_GROUP), F32))
        for c in range(n_chunks):
            region = r * n_chunks + c
            slot = lax.rem(i * regions + region, 2)
            if c + 1 < n_chunks:
                for cp in page_copies(b, c + 1, 1 - slot, True):
                    cp.start()
            else:
                @pl.when(b + 1 < db)
                def _():
                    for cp in page_copies(b + 1, 0, 1 - slot, True):
                        cp.start()
            for cp in page_copies(b, c, slot, False):
                cp.wait()
            k_pages = [kbuf.at[slot, idx] for idx in range(chunk_pages)]
            v_pages = [vbuf.at[slot, idx] for idx in range(chunk_pages)]
            m_new, l_new, alpha, pb = _decode_scores(qbd, k_pages, state, c * chunk_pages * page, past_len + t_row,
                                                     slope_row)
            if region == 0:
                x1 = _post_head(x_ref, mix_ref, yc_ref, wo_ref, gpost_ref, gfpre_ref, hn_s)
            state = (m_new, l_new, _decode_values(pb, alpha, v_pages, state, page))
            for f in ffn_order[region]:
                _ffn_chunk(f, hn_s, wgu_ref, act_s)
        y = _decode_finish(qbd, state, gather_rows(kn_ref, b), gather_rows(vn_ref, b), lam, subg_ref[0], t_row,
                           slope_row, t_new, lam_init)
        for t in range(t_new):
            oa_ref[pl.ds(t * db + b, 1), :] = y[t:t + 1, :]

    _post_tail(x1, act_s, wd_ref, gfpost_ref, o_ref)

    @pl.when(i == pl.num_programs(0) - 1)
    def _():
        n_s = xs_ref.shape[0]
        hn_v, act_v = hn_s.at[pl.ds(0, n_s)], act_s.at[pl.ds(0, n_s)]
        x1s = _post_head(xs_ref, mixs_ref, oa_ref, wo_ref, gpost_ref, gfpre_ref, hn_v)
        for f in range(N_FF_CHUNKS):
            _ffn_chunk(f, hn_v, wgu_ref, act_v)
        _post_tail(x1s, act_v, wd_ref, gfpost_ref, xso_ref)


def _post_attn(x2d, mix3, yc, xs, mix3_s, page_table, q, kn, vn, cache_kt, cache_vt, wts, layer, *, tm, db, t_new,
               lam_init):
    m = x2d.shape[0]
    n_steps = m // tm
    n_pages = page_table.shape[1]
    page = cache_kt.shape[-1]
    chunk_pages = min(DECODE_CHUNK_PAGES, n_pages)
    ms = db * t_new
    assert db % n_steps == 0 and n_pages % chunk_pages == 0 and ms <= tm
    row = lambda i, pt: (i, 0)
    full = lambda i, pt: (0, 0)
    lvec = _layer_spec((1, DK), layer)
    in_specs = _post_specs(tm, layer, row)
    in_specs += [pl.BlockSpec((ms, D_GROUP), full)] * 3 + [lvec] * 4 + [_layer_spec((1, D_GROUP), layer)]
    in_specs += [pl.BlockSpec((ms, D_MODEL), full), pl.BlockSpec((ms, 3 * D_GROUP), full)]
    in_specs += [pl.BlockSpec(memory_space=pl.ANY)] * 2
    page_buf = pltpu.VMEM((2, chunk_pages, D_GROUP, page), F32)
    grid_spec = pltpu.PrefetchScalarGridSpec(
        num_scalar_prefetch=1,
        grid=(n_steps,),
        in_specs=in_specs,
        out_specs=[pl.BlockSpec((tm, D_MODEL), row), pl.BlockSpec((ms, D_GROUP), full),
                   pl.BlockSpec((ms, D_MODEL), full)],
        scratch_shapes=[pltpu.VMEM((tm, D_MODEL), BF16), pltpu.VMEM((tm, D_FF), BF16), page_buf, page_buf,
                        pltpu.SemaphoreType.DMA((2, 2))],
    )
    x_new, _, xs_new = pl.pallas_call(
        functools.partial(_post_attn_kernel, layer=layer, db=db, seqs_per_step=db // n_steps, t_new=t_new, page=page,
                          n_chunks=n_pages // chunk_pages, chunk_pages=chunk_pages, past_len=n_pages * page,
                          lam_init=lam_init),
        grid_spec=grid_spec,
        out_shape=[jax.ShapeDtypeStruct((m, D_MODEL), F32), jax.ShapeDtypeStruct((ms, D_GROUP), F32),
                   jax.ShapeDtypeStruct((ms, D_MODEL), F32)],
        compiler_params=pltpu.CompilerParams(
            dimension_semantics=("arbitrary",), vmem_limit_bytes=VMEM_LIMIT_BYTES),
        name="post_attn",
    )(page_table, *_post_args(x2d, mix3, yc, wts), q, kn, vn, wts["lambda_q1"], wts["lambda_k1"], wts["lambda_q2"],
      wts["lambda_k2"], wts["attn_sub_g4"], xs, mix3_s, cache_kt, cache_vt)
    return x_new, xs_new


def _prepare_weights(w_in, w_out, conv_w, conv_b, conv_ln_g, conv_ln_b, sgu_w, sgu_b, sgu_ln_g, sgu_ln_b,
                     lambda_q1, lambda_k1, lambda_q2, lambda_k2, attn_sub_g, pool_w, pool_scale, g_mix_pre,
                     g_mix_post, g_ffn_pre, g_ffn_post, w_gu, w_down, t_new):
    depth = w_in.shape[0]
    hd = D_GROUP // SGU_HEADS
    sgu_w4 = jnp.repeat(jnp.transpose(sgu_w[:, :, :t_new, :t_new], (0, 2, 3, 1)), hd, axis=3)
    eye = jnp.eye(len(POOL_WINDOWS), dtype=F32)
    row = lambda a: a[:, None, :]
    return {
        "w_in": w_in.astype(BF16), "w_out": w_out.astype(BF16), "w_gu": w_gu.astype(BF16),
        "w_down": w_down.astype(BF16),
        "conv_w": conv_w, "conv_b": row(conv_b), "conv_ln_g": row(conv_ln_g), "conv_ln_b": row(conv_ln_b),
        "sgu_w": sgu_w,
        "sgu_b_full": jnp.repeat(jnp.swapaxes(sgu_b, 1, 2), hd, axis=2),
        "sgu_w4": sgu_w4.reshape(depth, t_new * t_new, D_GROUP),
        "sgu_ln_g": row(sgu_ln_g), "sgu_ln_b": row(sgu_ln_b),
        "lambda_q1": row(lambda_q1), "lambda_k1": row(lambda_k1), "lambda_q2": row(lambda_q2),
        "lambda_k2": row(lambda_k2),
        "attn_sub_g_col": attn_sub_g[:, :, None], "attn_sub_g4": jnp.tile(row(attn_sub_g), (1, 1, N_HEADS)),
        "pool_w_bd": jnp.einsum("lgce,gh->lgche", pool_w, eye).reshape(depth, D_GROUP, D_GROUP).astype(BF16),
        "pool_scale": row(pool_scale),
        "g_mix_pre": row(g_mix_pre), "g_mix_post": row(g_mix_post), "g_ffn_pre": row(g_ffn_pre),
        "g_ffn_post": row(g_ffn_post),
    }


def kernel(x_prompt, x_sample, cache_k, cache_v, state_conv, state_pool, page_table, w_in, w_out, conv_w, conv_b, conv_ln_g, conv_ln_b, sgu_w, sgu_b, sgu_ln_g, sgu_ln_b, lambda_q1, lambda_k1, lambda_q2, lambda_k2, attn_sub_g, pool_w, pool_scale, g_mix_pre, g_mix_post, g_ffn_pre, g_ffn_post, w_gu, w_down):
    batch, seq, _ = x_prompt.shape
    db, t_new, _ = x_sample.shape
    depth, n_pool, page = cache_k.shape[0], cache_k.shape[1], cache_k.shape[2]
    past_len = page_table.shape[1] * page
    assert seq % ROW_TILE == 0 and seq % SEQ_TILE == 0 and db % SUBLANES == 0
    assert seq % ATT_Q_TILE == 0 and ATT_Q_TILE % ATT_K_TILE == 0
    assert page == LANES

    wts = _prepare_weights(w_in, w_out, conv_w, conv_b, conv_ln_g, conv_ln_b, sgu_w, sgu_b, sgu_ln_g, sgu_ln_b,
                           lambda_q1, lambda_k1, lambda_q2, lambda_k2, attn_sub_g, pool_w, pool_scale,
                           g_mix_pre, g_mix_post, g_ffn_pre, g_ffn_post, w_gu, w_down, t_new)

    cache_kt = jnp.transpose(cache_k, (0, 1, 3, 4, 5, 2)).reshape(depth, n_pool, D_GROUP, page)
    cache_vt = jnp.transpose(cache_v, (0, 1, 3, 4, 2)).reshape(depth, n_pool, D_GROUP, page)
    ms = db * t_new
    xs = jnp.swapaxes(x_sample, 0, 1).reshape(ms, D_MODEL)
    sc_tm = jnp.swapaxes(state_conv, 1, 2)
    sp_tm = jnp.swapaxes(state_pool, 1, 2)
    xp = x_prompt.reshape(batch * seq, D_MODEL)

    kv_all = None
    cp_l, pp_l = [], []
    ks_l, vs_l, cs_l, ps_l, gs_l = [], [], [], [], []
    for l in range(depth):
        lam_init = 0.8 - 0.6 * math.exp(-0.3 * l)
        mix3, k_rows, qt, kt_all, vt_all, conv_new, pool_new = _proj_mix_prompt(xp, wts, kv_all, l, batch=batch,
                                                                                seq=seq)
        kv_all = (kt_all, vt_all)
        yc = _attn_prompt(qt, k_rows, vt_all, wts, l, batch=batch, seq=seq, lam_init=lam_init)
        cp_l.append(conv_new); pp_l.append(pool_new)
        mix3_s, q, kn, vn, conv_new, pool_new, sgu_v = _proj_mix_sample(xs, sc_tm, sp_tm, wts, l, db=db, t_new=t_new,
                                                                        past_len=past_len)
        xp, xs = _post_attn(xp, mix3, yc, xs, mix3_s, page_table, q, kn, vn, cache_kt, cache_vt, wts, l, tm=ROW_TILE,
                            db=db, t_new=t_new, lam_init=lam_init)
        ks_l.append(kn); vs_l.append(vn); cs_l.append(conv_new); ps_l.append(pool_new); gs_l.append(sgu_v)

    kt_all, vt_all = kv_all
    y_prompt = xp.reshape(batch, seq, D_MODEL)
    y_sample = jnp.swapaxes(xs.reshape(t_new, db, D_MODEL), 0, 1)
    k_prompt = jnp.transpose(kt_all.reshape(depth, batch, N_HEADS, 2, DK, seq), (0, 1, 5, 2, 3, 4))
    v_prompt = jnp.transpose(vt_all.reshape(depth, batch, N_HEADS, DV, seq), (0, 1, 4, 2, 3))
    tm_to_bm = lambda a: jnp.swapaxes(jnp.stack(a).reshape(depth, t_new, db, D_GROUP), 1, 2)
    k_sample = tm_to_bm(ks_l).reshape(depth, db, t_new, N_HEADS, 2, DK)
    v_sample = tm_to_bm(vs_l).reshape(depth, db, t_new, N_HEADS, DV)
    conv_sample = jnp.swapaxes(jnp.stack(cs_l), 1, 2)
    pool_sample = jnp.swapaxes(jnp.stack(ps_l), 1, 2)
    sgu_v_sample = tm_to_bm(gs_l)
    return (y_prompt, y_sample, k_prompt, v_prompt, jnp.stack(cp_l), jnp.stack(pp_l),
            k_sample, v_sample, conv_sample, pool_sample, sgu_v_sample)
```
